```python
import jax
import jax.numpy as jnp
from jax import lax
import numpy as np

D_MODEL = 2048
BATCH = 16
SEQ = 256
DEPTH = 4
DEC_BATCH = 4
DEC_SEQ = 4096
PAST_LEN = 512

GRID_W = 64
HEAD_DIM = 128
MIX_W = D_MODEL
HALF_W = MIX_W // 2
H_A = HALF_W // HEAD_DIM
KV_A = H_A // 4
WINDOW = 128
BAND_BLK = 128
H_B = 4
DK_B = HALF_W // (2 * H_B)
DV_B = HALF_W // H_B
GLA_RANK = 16
GLA_NORMALIZER = 16.0
GLA_CHUNK = 64
H_C = HALF_W // HEAD_DIM
NA_KH = 8
NA_KW = 16
U_D = HALF_W
G_D = 4
D_CHUNK = 128
ROPE_BASE = 10000.0
N_EVEN = (DEPTH + 1) // 2
N_ODD = DEPTH // 2
EVEN_SPLITS = (H_A * HEAD_DIM, KV_A * HEAD_DIM, KV_A * HEAD_DIM,
               H_B * DK_B, H_B * DK_B, H_B * DV_B, 2 * GLA_RANK, MIX_W)
ODD_SPLITS = (H_C * HEAD_DIM, H_C * HEAD_DIM, H_C * HEAD_DIM, U_D, U_D, MIX_W)
IN_EVEN = sum(EVEN_SPLITS)
IN_ODD = sum(ODD_SPLITS)
NEG = -1e30

kernel_name = 'hybrid_diffusion_prefix_trunk_step'


def rms_norm(x, g, eps=1e-6):
    xf = x.astype(jnp.float32)
    y = xf * lax.rsqrt(jnp.mean(xf * xf, axis=-1, keepdims=True) + eps)
    return (y * g.astype(jnp.float32)).astype(x.dtype)


def split_cols(h, sizes):
    idx = [int(s) for s in np.cumsum(sizes)[:-1]]
    return jnp.split(h, idx, axis=-1)


def modulation(cond, w, b):
    m = jax.nn.silu(cond) @ w + b
    return jnp.split(m, 3, axis=-1)


def rope_2d(x):
    T = x.shape[1]
    t = jnp.arange(T)
    pos = jnp.stack([t // GRID_W, t % GRID_W], axis=-1).astype(jnp.float32)
    nf = HEAD_DIM // 4
    inv = ROPE_BASE ** (-jnp.arange(nf, dtype=jnp.float32) / nf)
    ang = pos[:, :, None] * inv
    cos = jnp.cos(ang)[None, :, None]
    sin = jnp.sin(ang)[None, :, None]
    xr = x.astype(jnp.float32).reshape(x.shape[:-1] + (2, 2, nf))
    a, b = xr[..., 0, :], xr[..., 1, :]
    out = jnp.stack([a * cos - b * sin, a * sin + b * cos], axis=-2)
    return out.reshape(x.shape).astype(x.dtype)


def ctx_attention(q, k, v, sink):
    B, L, H, _ = q.shape
    kv = k.shape[2]
    g = H // kv
    qg = q.reshape(B, L, kv, g, HEAD_DIM)
    s = jnp.einsum('bqkgd,blkd->bkgql', qg, k).astype(jnp.float32) * HEAD_DIM ** -0.5
    if sink is not None:
        sk = jnp.broadcast_to(sink.astype(jnp.float32).reshape(1, kv, g, 1, 1), s.shape[:-1] + (1,))
        s = jnp.concatenate([s, sk], axis=-1)
    p = jax.nn.softmax(s, axis=-1)[..., :L]
    o = jnp.einsum('bkgql,blkd->bqkgd', p.astype(v.dtype), v)
    return o.reshape(B, L, H * HEAD_DIM)


def window_sink_attention(q, k, v, ck, cv, sink):
    B, T = q.shape[:2]
    L = ck.shape[1]
    nb = T // BAND_BLK
    g = H_A // KV_A
    scale = HEAD_DIM ** -0.5
    qb = q.reshape(B, nb, BAND_BLK, KV_A, g, HEAD_DIM)

    def band(a):
        ap = jnp.pad(a, ((0, 0), (BAND_BLK, BAND_BLK), (0, 0), (0, 0)))
        ap = ap.reshape(B, nb + 2, BAND_BLK, KV_A, HEAD_DIM)
        return jnp.concatenate([ap[:, :-2], ap[:, 1:-1], ap[:, 2:]], axis=2)

    kb, vb = band(k), band(v)
    s_band = jnp.einsum('bnqkgd,bnjkd->bnkgqj', qb, kb).astype(jnp.float32) * scale
    blk = jnp.arange(nb)[:, None]
    qpos = blk * BAND_BLK + jnp.arange(BAND_BLK)[None]
    kpos = (blk - 1) * BAND_BLK + jnp.arange(3 * BAND_BLK)[None]
    valid = ((kpos >= 0) & (kpos < T))[:, None, :] & \
        (jnp.abs(qpos[:, :, None] - kpos[:, None, :]) <= WINDOW)
    s_band = jnp.where(valid[None, :, None, None], s_band, NEG)
    s_ctx = jnp.einsum('bnqkgd,blkd->bnkgql', qb, ck).astype(jnp.float32) * scale
    sk = jnp.broadcast_to(sink.astype(jnp.float32).reshape(1, 1, KV_A, g, 1, 1), s_ctx.shape[:-1] + (1,))
    p = jax.nn.softmax(jnp.concatenate([s_band, s_ctx, sk], axis=-1), axis=-1)
    pb = p[..., :3 * BAND_BLK]
    pc = p[..., 3 * BAND_BLK:3 * BAND_BLK + L]
    o = jnp.einsum('bnkgqj,bnjkd->bnqkgd', pb.astype(v.dtype), vb) + \
        jnp.einsum('bnkgql,blkd->bnqkgd', pc.astype(cv.dtype), cv)
    return o.reshape(B, T, H_A * HEAD_DIM)


def gla_scan(q, k, v, g, s0):
    B, T, H, _ = q.shape
    DV = v.shape[-1]
    nc = T // GLA_CHUNK

    def chunks(a):
        return a.astype(jnp.float32).reshape(B, nc, GLA_CHUNK, H, a.shape[-1]).transpose(1, 0, 3, 2, 4)

    tri = jnp.tril(jnp.ones((GLA_CHUNK, GLA_CHUNK), dtype=bool))

    def step(S, inp):
        qc, kc, vc, gc = inp
        bcum = jnp.cumsum(gc, axis=2)
        blast = bcum[:, :, -1:]
        qi = qc * jnp.exp(bcum)
        ki = kc * jnp.exp(-bcum)
        ks = kc * jnp.exp(blast - bcum)
        att = jnp.where(tri, jnp.einsum('bhid,bhjd->bhij', qi, ki), 0.0)
        o = jnp.einsum('bhij,bhjv->bhiv', att, vc) + jnp.einsum('bhid,bhdv->bhiv', qi, S)
        S = jnp.exp(blast[:, :, 0])[..., None] * S + jnp.einsum('bhjd,bhjv->bhdv', ks, vc)
        return S, o

    S, o = lax.scan(step, s0.astype(jnp.float32), (chunks(q), chunks(k), chunks(v), chunks(g)))
    o = o.transpose(1, 0, 3, 2, 4).reshape(B, T, H, DV)
    return o, S


def gla_branch(qb, kb, vb, rb, wg, bg, gnorm, s0):
    B, T = qb.shape[:2]
    q = qb.reshape(B, T, H_B, DK_B) * DK_B ** -0.5
    k = kb.reshape(B, T, H_B, DK_B)
    v = vb.reshape(B, T, H_B, DV_B)
    r = rb.reshape(B, T, 2, GLA_RANK)
    logit = jnp.einsum('btdr,drk->btdk', r, wg) + bg
    g = (jax.nn.log_sigmoid(logit.astype(jnp.float32)) / GLA_NORMALIZER).reshape(B, T, 2, H_B, DK_B)
    o_f, s_f = gla_scan(q, k, v, g[:, :, 0], s0[:, 0])
    o_b, s_b = gla_scan(jnp.flip(q, 1), jnp.flip(k, 1), jnp.flip(v, 1), jnp.flip(g[:, :, 1], 1), s0[:, 1])
    o = rms_norm(o_f + jnp.flip(o_b, 1), gnorm)
    return o.reshape(B, T, H_B * DV_B).astype(qb.dtype), jnp.stack([s_f, s_b], axis=1)


def neighbourhood_attention(q, k, v, ck, cv, bias_tab):
    B, T = q.shape[:2]
    rows = T // GRID_W
    kh = min(NA_KH, rows)
    scale = HEAD_DIM ** -0.5
    r = jnp.arange(rows)
    rs = jnp.clip(r - kh // 2, 0, rows - kh)
    row_idx = rs[:, None] + jnp.arange(kh)[None]
    qg = q.reshape(B, rows, GRID_W, H_C, HEAD_DIM)
    kg = k.reshape(B, rows, GRID_W, H_C, HEAD_DIM)[:, row_idx]
    vg = v.reshape(B, rows, GRID_W, H_C, HEAD_DIM)[:, row_idx]
    s_nb = jnp.einsum('brqhd,brkwhd->brhqkw', qg, kg).astype(jnp.float32) * scale
    col = jnp.arange(GRID_W)
    cs = jnp.clip(col - NA_KW // 2, 0, GRID_W - NA_KW)
    col_ok = (col[None, :] >= cs[:, None]) & (col[None, :] < cs[:, None] + NA_KW)
    dr = row_idx - r[:, None] + (NA_KH - 1)
    dc = jnp.clip(col[None, :] - col[:, None], -(NA_KW - 1), NA_KW - 1) + (NA_KW - 1)
    bias = bias_tab[:, dr[:, None, :, None], dc[None, :, None, :]]
    bias = bias.transpose(1, 0, 2, 3, 4).astype(jnp.float32)
    s_nb = jnp.where(col_ok[:, None, :], s_nb + bias, NEG)
    s_nb = s_nb.reshape(B, rows, H_C, GRID_W, kh * GRID_W)
    s_ctx = jnp.einsum('brqhd,blhd->brhql', qg, ck).astype(jnp.float32) * scale
    p = jax.nn.softmax(jnp.concatenate([s_nb, s_ctx], axis=-1), axis=-1)
    pn = p[..., :kh * GRID_W].reshape(B, rows, H_C, GRID_W, kh, GRID_W)
    pc = p[..., kh * GRID_W:]
    o = jnp.einsum('brhqkw,brkwhd->brqhd', pn.astype(v.dtype), vg) + \
        jnp.einsum('brhql,blhd->brqhd', pc.astype(cv.dtype), cv)
    return o.reshape(B, T, H_C * HEAD_DIM)


def spatial_gating(u, v, gnorm, ws, b):
    B, T = u.shape[:2]
    nc = T // D_CHUNK
    vf = v.astype(jnp.float32)
    mu = jnp.mean(vf, axis=-1, keepdims=True)
    var = jnp.mean(jnp.square(vf - mu), axis=-1, keepdims=True)
    vn = ((vf - mu) * lax.rsqrt(var + 1e-5) * gnorm.astype(jnp.float32)).astype(v.dtype)
    vc = vn.reshape(B, nc, D_CHUNK, G_D, U_D // G_D)
    sp = jnp.einsum('gij,bnjgc->bnigc', ws, vc) + b.T[None, None, :, :, None]
    return u * sp.reshape(B, T, U_D)


def even_mixer(h, w_in, sink, wg, bg, gnorm, cache):
    B, T = h.shape[:2]
    qa, ka, va, qb, kb, vb, rb, gp = split_cols(h @ w_in, EVEN_SPLITS)
    qa = qa.reshape(B, T, H_A, HEAD_DIM)
    ka = ka.reshape(B, T, KV_A, HEAD_DIM)
    va = va.reshape(B, T, KV_A, HEAD_DIM)
    if cache is None:
        ya = ctx_attention(qa, ka, va, sink)
        s0 = jnp.zeros((B, 2, H_B, DK_B, DV_B), jnp.float32)
    else:
        ck, cv, s0 = cache
        ya = window_sink_attention(rope_2d(qa), rope_2d(ka), va, ck, cv, sink)
    yb, s_fin = gla_branch(qb, kb, vb, rb, wg, bg, gnorm, s0)
    y = jnp.concatenate([ya, yb.astype(ya.dtype)], axis=-1) * jax.nn.silu(gp)
    side = (ka, va, s_fin) if cache is None else None
    return y, side


def odd_mixer(h, w_in, bias_tab, gnorm, ws, b, cache):
    B, T = h.shape[:2]
    qc, kc, vc, u, v, gp = split_cols(h @ w_in, ODD_SPLITS)
    qc = qc.reshape(B, T, H_C, HEAD_DIM)
    kc = kc.reshape(B, T, H_C, HEAD_DIM)
    vc = vc.reshape(B, T, H_C, HEAD_DIM)
    if cache is None:
        yc = ctx_attention(qc, kc, vc, None)
    else:
        yc = neighbourhood_attention(qc, kc, vc, cache[0], cache[1], bias_tab)
    yd = spatial_gating(u, v, gnorm, ws, b)
    y = jnp.concatenate([yc, yd.astype(yc.dtype)], axis=-1) * jax.nn.silu(gp)
    side = (kc, vc) if cache is None else None
    return y, side


def setup_inputs(seed: int = 0) -> dict:
    key = jax.random.key(seed)
    ks = jax.random.split(key, 24)
    f32 = jnp.float32

    def nrm(k, shape, s=1.0):
        return jax.random.normal(k, shape, f32) * s

    return {
        'x_prompt': nrm(ks[0], (BATCH, SEQ, D_MODEL)),
        'x_sample': nrm(ks[1], (DEC_BATCH, DEC_SEQ, D_MODEL)),
        'cache_attn_k': nrm(ks[2], (DEC_BATCH, N_EVEN, PAST_LEN, KV_A, HEAD_DIM)),
        'cache_attn_v': nrm(ks[3], (DEC_BATCH, N_EVEN, PAST_LEN, KV_A, HEAD_DIM)),
        'state_gla': nrm(ks[4], (DEC_BATCH, N_EVEN, 2, H_B, DK_B, DV_B), 0.5),
        'cache_na_k': nrm(ks[5], (DEC_BATCH, N_ODD, PAST_LEN, H_C, HEAD_DIM)),
        'cache_na_v': nrm(ks[6], (DEC_BATCH, N_ODD, PAST_LEN, H_C, HEAD_DIM)),
        'c': nrm(ks[7], (DEC_BATCH, D_MODEL)),
        'c_ctx': nrm(ks[8], (D_MODEL,)),
        'ada_w': nrm(ks[9], (DEPTH, D_MODEL, 3 * D_MODEL), 0.5 * D_MODEL ** -0.5),
        'ada_b': nrm(ks[10], (DEPTH, 3 * D_MODEL), 0.02),
        'norm_g': 1.0 + nrm(ks[11], (DEPTH, D_MODEL), 0.05),
        'w_in_even': nrm(ks[12], (N_EVEN, D_MODEL, IN_EVEN), D_MODEL ** -0.5),
        'w_in_odd': nrm(ks[13], (N_ODD, D_MODEL, IN_ODD), D_MODEL ** -0.5),
        'w_out': nrm(ks[14], (DEPTH, MIX_W, D_MODEL), MIX_W ** -0.5),
        'attn_sink': nrm(ks[15], (N_EVEN, H_A), 0.5),
        'gla_wg': nrm(ks[16], (N_EVEN, 2, GLA_RANK, H_B * DK_B), GLA_RANK ** -0.5),
        'gla_bg': nrm(ks[17], (N_EVEN, 2, H_B * DK_B), 0.1),
        'gla_norm_g': 1.0 + nrm(ks[18], (N_EVEN, DV_B), 0.05),
        'na_bias': nrm(ks[19], (N_ODD, H_C, 2 * NA_KH - 1, 2 * NA_KW - 1), 0.2),
        'gmlp_norm_g': 1.0 + nrm(ks[20], (N_ODD, U_D), 0.05),
        'gmlp_ws': nrm(ks[21], (N_ODD, G_D, D_CHUNK, D_CHUNK), D_CHUNK ** -0.5),
        'gmlp_b': 1.0 + nrm(ks[22], (N_ODD, G_D, D_CHUNK), 0.1),
        'final_norm_g': 1.0 + nrm(ks[23], (D_MODEL,), 0.05),
    }


def reference(x_prompt, x_sample, cache_attn_k, cache_attn_v, state_gla, cache_na_k, cache_na_v,
              c, c_ctx, ada_w, ada_b, norm_g, w_in_even, w_in_odd, w_out, attn_sink,
              gla_wg, gla_bg, gla_norm_g, na_bias, gmlp_norm_g, gmlp_ws, gmlp_b, final_norm_g):
    x = x_prompt
    ks_a, vs_a, ss_b, ks_c, vs_c = [], [], [], [], []
    for l in range(DEPTH):
        shift, scale, gate = modulation(c_ctx, ada_w[l], ada_b[l])
        h = rms_norm(x, norm_g[l]) * (1.0 + scale) + shift
        if l % 2 == 0:
            e = l // 2
            y, side = even_mixer(h, w_in_even[e], attn_sink[e], gla_wg[e], gla_bg[e], gla_norm_g[e], None)
            ks_a.append(side[0])
            vs_a.append(side[1])
            ss_b.append(side[2])
        else:
            o = l // 2
            y, side = odd_mixer(h, w_in_odd[o], na_bias[o], gmlp_norm_g[o], gmlp_ws[o], gmlp_b[o], None)
            ks_c.append(side[0])
            vs_c.append(side[1])
        x = x + gate * (y @ w_out[l])
    y_prompt = rms_norm(x, final_norm_g)
    new_attn_k = jnp.stack(ks_a, axis=1)
    new_attn_v = jnp.stack(vs_a, axis=1)
    new_gla_state = jnp.stack(ss_b, axis=1)
    new_na_k = jnp.stack(ks_c, axis=1)
    new_na_v = jnp.stack(vs_c, axis=1)

    x = x_sample
    for l in range(DEPTH):
        shift, scale, gate = modulation(c, ada_w[l], ada_b[l])
        h = rms_norm(x, norm_g[l]) * (1.0 + scale[:, None]) + shift[:, None]
        if l % 2 == 0:
            e = l // 2
            y, _ = even_mixer(h, w_in_even[e], attn_sink[e], gla_wg[e], gla_bg[e], gla_norm_g[e],
                              (cache_attn_k[:, e], cache_attn_v[:, e], state_gla[:, e]))
        else:
            o = l // 2
            y, _ = odd_mixer(h, w_in_odd[o], na_bias[o], gmlp_norm_g[o], gmlp_ws[o], gmlp_b[o],
                             (cache_na_k[:, o], cache_na_v[:, o]))
        x = x + gate[:, None] * (y @ w_out[l])
    y_sample = rms_norm(x, final_norm_g)
    return (y_prompt, y_sample, new_attn_k, new_attn_v, new_gla_state, new_na_k, new_na_v)
```

```python
import functools

import numpy as np
import jax
import jax.numpy as jnp
from jax import lax
from jax.experimental import pallas as pl
from jax.experimental.pallas import tpu as pltpu

F32 = jnp.float32
BF16 = jnp.bfloat16

D_MODEL = 2048
HEAD_DIM = 128
GRID_W = 64
HALF_W = D_MODEL // 2
H_A = 8
KV_A = 2
WINDOW = 128
BAND_BLK = 128
H_B = 4
DK_B = 128
DV_B = 256
GLA_RANK = 16
GLA_NORMALIZER = 16.0
GLA_CHUNK = 64
H_C = 8
NA_KH = 8
NA_KW = 16
G_D = 4
D_CHUNK = 128
ROPE_BASE = 10000.0
NEG = -1e30
ATTN_SCALE = HEAD_DIM ** -0.5

EV_QA, EV_GPA, EV_VB, EV_GPB = 0, 1024, 2048, 3072
EV_QB, EV_KB, EV_KA, EV_VA, EV_RB = 4096, 4608, 5120, 5376, 5632
N_EVEN_COLS = 5760
OD_QC, OD_KC, OD_VC, OD_U, OD_V, OD_GPC, OD_GPD = 0, 1024, 2048, 3072, 4096, 5120, 6144
N_ODD_COLS = 7168

VMEM_LIMIT = 52 * 1024 * 1024


def _cparams(*sem):
    return pltpu.CompilerParams(dimension_semantics=sem, vmem_limit_bytes=VMEM_LIMIT)


def _silu(x):
    return x / (1.0 + jnp.exp(-x))


def _dot(a, b):
    return jnp.dot(a, b, preferred_element_type=F32)


def _dot_nt(a, b):
    return lax.dot_general(a, b, (((1,), (1,)), ((), ())), preferred_element_type=F32)


def _rms(x, g, eps=1e-6):
    return x * lax.rsqrt(jnp.mean(x * x, axis=-1, keepdims=True) + eps) * g


def _mod_kernel(cond_ref, w_ref, b_ref, o_ref):
    a = _silu(cond_ref[...]).astype(BF16)
    o_ref[0] = _dot(a, w_ref[0].astype(BF16)) + b_ref[0]


def _modulation(cond, ada_w, ada_b):
    depth, d, n = ada_w.shape
    rows = cond.shape[0]
    tn = 1024
    return pl.pallas_call(
        _mod_kernel,
        grid=(depth, n // tn),
        in_specs=[
            pl.BlockSpec((rows, d), lambda l, j: (0, 0)),
            pl.BlockSpec((1, d, tn), lambda l, j: (l, 0, j)),
            pl.BlockSpec((1, 1, tn), lambda l, j: (l, 0, j)),
        ],
        out_specs=pl.BlockSpec((1, rows, tn), lambda l, j: (l, 0, j)),
        out_shape=jax.ShapeDtypeStruct((depth, rows, n), F32),
        compiler_params=_cparams("parallel", "parallel"),
        name="modulation",
    )(cond, ada_w, ada_b.reshape(depth, 1, n))


def _prenorm_kernel(x_ref, g_ref, mod_ref, h_ref):
    d = x_ref.shape[-1]
    y = _rms(x_ref[...], g_ref[...])
    shift = mod_ref[0, :, 0:d]
    scale = mod_ref[0, :, d:2 * d]
    h_ref[...] = (y * (1.0 + scale) + shift).astype(BF16)


def _prenorm(x, g, mod, grp0, rows_per_grp, tm):
    m, d = x.shape
    tpg = rows_per_grp // tm
    return pl.pallas_call(
        _prenorm_kernel,
        grid=(m // tm,),
        in_specs=[
            pl.BlockSpec((tm, d), lambda i: (i, 0)),
            pl.BlockSpec((1, d), lambda i: (0, 0)),
            pl.BlockSpec((1, 1, 3 * d), lambda i: (grp0 + i // tpg, 0, 0)),
        ],
        out_specs=pl.BlockSpec((tm, d), lambda i: (i, 0)),
        out_shape=jax.ShapeDtypeStruct((m, d), BF16),
        compiler_params=_cparams("parallel"),
        name="prenorm",
    )(x, g.reshape(1, d), mod)


def _matmul_kernel(h_ref, w_ref, o_ref):
    o_ref[...] = _dot(h_ref[...], w_ref[...]).astype(o_ref.dtype)


def _in_proj(h, w, out_dtype, tm, tn):
    m, k = h.shape
    n = w.shape[1]
    return pl.pallas_call(
        _matmul_kernel,
        grid=(m // tm, n // tn),
        in_specs=[
            pl.BlockSpec((tm, k), lambda i, j: (i, 0)),
            pl.BlockSpec((k, tn), lambda i, j: (0, j)),
        ],
        out_specs=pl.BlockSpec((tm, tn), lambda i, j: (i, j)),
        out_shape=jax.ShapeDtypeStruct((m, n), out_dtype),
        compiler_params=_cparams("parallel", "parallel"),
        name="in_proj",
    )(h, w)


def _outproj_kernel(ya_ref, yb_ref, w_ref, x_ref, mod_ref, g_ref, modn_ref, *out_refs, final):
    d = x_ref.shape[-1]
    half = ya_ref.shape[-1]
    acc = _dot(ya_ref[...], w_ref[0:half, :]) + _dot(yb_ref[...], w_ref[half:2 * half, :])
    gate = mod_ref[0, :, 2 * d:3 * d]
    xn = x_ref[...] + gate * acc
    if final:
        out_refs[0][...] = _rms(xn, g_ref[...])
    else:
        out_refs[0][...] = xn
        shift = modn_ref[0, :, 0:d]
        scale = modn_ref[0, :, d:2 * d]
        out_refs[1][...] = (_rms(xn, g_ref[...]) * (1.0 + scale) + shift).astype(BF16)


def _out_proj(ya, yb, w, x, mod, g_next, mod_next, grp0, rows_per_grp, tm, final):
    m, d = x.shape
    half = ya.shape[1]
    tpg = rows_per_grp // tm
    modspec = pl.BlockSpec((1, 1, 3 * d), lambda i: (grp0 + i // tpg, 0, 0))
    rowspec = pl.BlockSpec((tm, d), lambda i: (i, 0))
    halfspec = pl.BlockSpec((tm, half), lambda i: (i, 0))
    if final:
        out_shape = [jax.ShapeDtypeStruct((m, d), F32)]
        out_specs = [rowspec]
    else:
        out_shape = [jax.ShapeDtypeStruct((m, d), F32), jax.ShapeDtypeStruct((m, d), BF16)]
        out_specs = [rowspec, rowspec]
    return pl.pallas_call(
        functools.partial(_outproj_kernel, final=final),
        grid=(m // tm,),
        in_specs=[
            halfspec, halfspec,
            pl.BlockSpec((2 * half, d), lambda i: (0, 0)),
            rowspec, modspec,
            pl.BlockSpec((1, d), lambda i: (0, 0)),
            modspec,
        ],
        out_specs=out_specs,
        out_shape=out_shape,
        compiler_params=_cparams("parallel"),
        name="out_proj",
    )(ya, yb, w, x, mod, g_next.reshape(1, d), mod_next)


def _ctx_attn_kernel(*refs, n_heads, n_kv, has_sink):
    if has_sink:
        sink_ref, q_ref, k_ref, v_ref, gp_ref, o_ref = refs
    else:
        q_ref, k_ref, v_ref, gp_ref, o_ref = refs
    L = q_ref.shape[0]
    grp = n_heads // n_kv
    outs = [None] * n_heads
    for kv in range(n_kv):
        heads = list(range(kv * grp, (kv + 1) * grp))
        qg = jnp.concatenate(
            [q_ref[:, h * HEAD_DIM:(h + 1) * HEAD_DIM].astype(BF16) for h in heads], axis=0)
        k = k_ref[:, kv * HEAD_DIM:(kv + 1) * HEAD_DIM].astype(BF16)
        v = v_ref[:, kv * HEAD_DIM:(kv + 1) * HEAD_DIM].astype(BF16)
        s = _dot_nt(qg, k) * ATTN_SCALE
        m = jnp.max(s, axis=-1, keepdims=True)
        if has_sink:
            sk = jnp.concatenate(
                [jnp.full((L, 1), sink_ref[h], F32) for h in heads], axis=0)
            m = jnp.maximum(m, sk)
        e = jnp.exp(s - m)
        den = jnp.sum(e, axis=-1, keepdims=True)
        if has_sink:
            den = den + jnp.exp(sk - m)
        o = _dot(e.astype(BF16), v) / den
        for i, h in enumerate(heads):
            outs[h] = o[i * L:(i + 1) * L, :]
    y = jnp.concatenate(outs, axis=1)
    o_ref[...] = (y * _silu(gp_ref[...].astype(F32))).astype(BF16)


def _ctx_attention(proj, sink, batch, L, q_col, k_col, v_col, gp_col, n_heads, n_kv):
    qw = n_heads * HEAD_DIM
    kw = n_kv * HEAD_DIM
    has_sink = sink is not None
    in_specs = [
        pl.BlockSpec((L, qw), lambda b: (b, q_col // qw)),
        pl.BlockSpec((L, kw), lambda b: (b, k_col // kw)),
        pl.BlockSpec((L, kw), lambda b: (b, v_col // kw)),
        pl.BlockSpec((L, qw), lambda b: (b, gp_col // qw)),
    ]
    args = [proj, proj, proj, proj]
    if has_sink:
        in_specs = [pl.BlockSpec(memory_space=pltpu.SMEM)] + in_specs
        args = [sink] + args
    return pl.pallas_call(
        functools.partial(_ctx_attn_kernel, n_heads=n_heads, n_kv=n_kv, has_sink=has_sink),
        grid=(batch,),
        in_specs=in_specs,
        out_specs=pl.BlockSpec((L, qw), lambda b: (b, 0)),
        out_shape=jax.ShapeDtypeStruct((batch * L, qw), BF16),
        compiler_params=_cparams("parallel"),
        name="ctx_attention",
    )(*args)


def _rope(x, cos, sin_signed):
    lane = lax.broadcasted_iota(jnp.int32, x.shape, 1)
    first = (lane % 64) < 32
    swapped = jnp.where(first, pltpu.roll(x, 96, 1), pltpu.roll(x, 32, 1))
    return x * cos + swapped * sin_signed


def _win_attn_kernel(sink_ref, q_ref, kp_ref, kc_ref, kn_ref, vp_ref, vc_ref, vn_ref,
                     cq_ref, sq_ref, cp_ref, sp_ref, cn_ref, sn_ref,
                     ck_ref, cv_ref, gp_ref, o_ref, *, nb):
    n = pl.program_id(1)
    blk = BAND_BLK
    grp = H_A // KV_A
    cq, sq = cq_ref[...], sq_ref[...]
    iq = lax.broadcasted_iota(jnp.int32, (grp * blk, 3 * blk), 0) % blk
    jk = lax.broadcasted_iota(jnp.int32, (grp * blk, 3 * blk), 1)
    rel = iq - jk + blk
    ok = (rel <= WINDOW) & (rel >= -WINDOW)
    ok = ok & ((jk >= blk) | (n > 0)) & ((jk < 2 * blk) | (n < nb - 1))
    outs = [None] * H_A
    for kv in range(KV_A):
        sl = slice(kv * HEAD_DIM, (kv + 1) * HEAD_DIM)
        kb = jnp.concatenate([
            _rope(kp_ref[:, sl].astype(F32), cp_ref[...], sp_ref[...]),
            _rope(kc_ref[:, sl].astype(F32), cq, sq),
            _rope(kn_ref[:, sl].astype(F32), cn_ref[...], sn_ref[...]),
        ], axis=0).astype(BF16)
        vb = jnp.concatenate([vp_ref[:, sl], vc_ref[:, sl], vn_ref[:, sl]], axis=0).astype(BF16)
        heads = list(range(kv * grp, (kv + 1) * grp))
        qg = jnp.concatenate([
            _rope(q_ref[:, h * HEAD_DIM:(h + 1) * HEAD_DIM].astype(F32), cq, sq) for h in heads
        ], axis=0).astype(BF16)
        ck = ck_ref[:, sl].astype(BF16)
        cv = cv_ref[:, sl].astype(BF16)
        s_band = jnp.where(ok, _dot_nt(qg, kb) * ATTN_SCALE, NEG)
        s_ctx = _dot_nt(qg, ck) * ATTN_SCALE
        sk = jnp.concatenate([jnp.full((blk, 1), sink_ref[h], F32) for h in heads], axis=0)
        m = jnp.maximum(jnp.maximum(jnp.max(s_band, axis=-1, keepdims=True),
                                    jnp.max(s_ctx, axis=-1, keepdims=True)), sk)
        e_band = jnp.exp(s_band - m)
        e_ctx = jnp.exp(s_ctx - m)
        den = (jnp.sum(e_band, axis=-1, keepdims=True) + jnp.sum(e_ctx, axis=-1, keepdims=True)
               + jnp.exp(sk - m))
        o = (_dot(e_band.astype(BF16), vb) + _dot(e_ctx.astype(BF16), cv)) / den
        for i, h in enumerate(heads):
            outs[h] = o[i * blk:(i + 1) * blk, :]
    y = jnp.concatenate(outs, axis=1)
    o_ref[...] = (y * _silu(gp_ref[...].astype(F32))).astype(BF16)


def _window_attention(proj, sink, cache_k, cache_v, layer, cos_tab, sin_tab, batch, T):
    blk = BAND_BLK
    nb = T // blk
    qw = H_A * HEAD_DIM
    kw = KV_A * HEAD_DIM
    P = cache_k.shape[2]

    def row(b, n):
        return b * nb + n

    def prev(n):
        return jnp.maximum(n - 1, 0)

    def nxt(n):
        return jnp.minimum(n + 1, nb - 1)

    kcol, vcol = EV_KA // kw, EV_VA // kw
    tab = lambda f: pl.BlockSpec((blk, HEAD_DIM), lambda b, n: (f(n), 0))
    ident = lambda n: n
    in_specs = [
        pl.BlockSpec(memory_space=pltpu.SMEM),
        pl.BlockSpec((blk, qw), lambda b, n: (row(b, n), EV_QA // qw)),
        pl.BlockSpec((blk, kw), lambda b, n: (row(b, prev(n)), kcol)),
        pl.BlockSpec((blk, kw), lambda b, n: (row(b, n), kcol)),
        pl.BlockSpec((blk, kw), lambda b, n: (row(b, nxt(n)), kcol)),
        pl.BlockSpec((blk, kw), lambda b, n: (row(b, prev(n)), vcol)),
        pl.BlockSpec((blk, kw), lambda b, n: (row(b, n), vcol)),
        pl.BlockSpec((blk, kw), lambda b, n: (row(b, nxt(n)), vcol)),
        tab(ident), tab(ident), tab(prev), tab(prev), tab(nxt), tab(nxt),
        pl.BlockSpec((None, None, P, kw), lambda b, n: (b, layer, 0, 0)),
        pl.BlockSpec((None, None, P, kw), lambda b, n: (b, layer, 0, 0)),
        pl.BlockSpec((blk, qw), lambda b, n: (row(b, n), EV_GPA // qw)),
    ]
    return pl.pallas_call(
        functools.partial(_win_attn_kernel, nb=nb),
        grid=(batch, nb),
        in_specs=in_specs,
        out_specs=pl.BlockSpec((blk, qw), lambda b, n: (row(b, n), 0)),
        out_shape=jax.ShapeDtypeStruct((batch * T, qw), BF16),
        compiler_params=_cparams("parallel", "parallel"),
        name="window_attention",
    )(sink, proj, proj, proj, proj, proj, proj, proj,
      cos_tab, sin_tab, cos_tab, sin_tab, cos_tab, sin_tab,
      cache_k, cache_v, proj)


def _log_sigmoid(x):
    return jnp.minimum(x, 0.0) - jnp.log1p(jnp.exp(-jnp.abs(x)))


def _cumsum_rows(tri, g):
    g_hi = g.astype(BF16)
    r1 = g - g_hi.astype(F32)
    g_mid = r1.astype(BF16)
    g_lo = (r1 - g_mid.astype(F32)).astype(BF16)
    return _dot(tri, g_hi) + _dot(tri, g_mid) + _dot(tri, g_lo)


def _gla_kernel(*refs, nc, has_s0):
    if has_s0:
        (qf, kf, vf, rf, qr, kr, vr, rr, wg_ref, bg_ref, s0_ref,
         of_ref, ob_ref, sfin_ref, s_scr) = refs
    else:
        (qf, kf, vf, rf, qr, kr, vr, rr, wg_ref, bg_ref,
         of_ref, ob_ref, sfin_ref, s_scr) = refs
    c = pl.program_id(1)
    C = GLA_CHUNK

    @pl.when(c == 0)
    def _():
        if has_s0:
            s_scr[...] = s0_ref[...]
        else:
            s_scr[...] = jnp.zeros(s_scr.shape, F32)

    ri = lax.broadcasted_iota(jnp.int32, (C, C), 0)
    ci = lax.broadcasted_iota(jnp.int32, (C, C), 1)
    hw = H_B * DK_B
    for d, (q_ref, k_ref, v_ref, r_ref, o_ref) in enumerate(
            ((qf, kf, vf, rf, of_ref), (qr, kr, vr, rr, ob_ref))):
        keep = (ci <= ri) if d == 0 else (ci >= ri)
        tri = jnp.where(keep, 1.0, 0.0).astype(BF16)
        logit = _dot(r_ref[...].astype(BF16), wg_ref[:, d * hw:(d + 1) * hw]) + bg_ref[:, d * hw:(d + 1) * hw]
        g = _log_sigmoid(logit) / GLA_NORMALIZER
        bc = _cumsum_rows(tri, g)
        total = bc[C - 1:C, :] if d == 0 else bc[0:1, :]
        e_pos = jnp.exp(bc)
        e_neg = jnp.exp(-bc)
        e_rem = jnp.exp(total - bc)
        e_tot = jnp.exp(total)
        for h in range(H_B):
            sl = slice(h * DK_B, (h + 1) * DK_B)
            vl = slice(h * DV_B, (h + 1) * DV_B)
            qh = q_ref[:, sl].astype(F32) * (DK_B ** -0.5)
            kh = k_ref[:, sl].astype(F32)
            vh = v_ref[:, vl].astype(BF16)
            qi = (qh * e_pos[:, sl]).astype(BF16)
            ki = (kh * e_neg[:, sl]).astype(BF16)
            ks = kh * e_rem[:, sl]
            att = jnp.where(keep, _dot_nt(qi, ki), 0.0).astype(BF16)
            S = s_scr[d, h]
            o_ref[:, vl] = _dot(att, vh) + _dot(qi, S.astype(BF16))
            stacked = jnp.concatenate([ks, jnp.broadcast_to(e_tot[:, sl], (C, DK_B))], axis=0)
            tr = stacked.T
            ks_t = tr[:, 0:C].astype(BF16)
            decay = tr[:, C:C + 1]
            s_scr[d, h] = decay * S + _dot(ks_t, vh)

    @pl.when(c == nc - 1)
    def _():
        sfin_ref[...] = s_scr[...]


def _gla(proj, wg_full, bg_full, s0, layer, batch, T):
    C = GLA_CHUNK
    nc = T // C
    m = batch * T
    qw, vw, rw = H_B * DK_B, H_B * DV_B, 128

    def fwd(b, c):
        return b * nc + c

    def bwd(b, c):
        return b * nc + (nc - 1 - c)

    def specs(f):
        return [
            pl.BlockSpec((C, qw), lambda b, c: (f(b, c), EV_QB // qw)),
            pl.BlockSpec((C, qw), lambda b, c: (f(b, c), EV_KB // qw)),
            pl.BlockSpec((C, vw), lambda b, c: (f(b, c), EV_VB // vw)),
            pl.BlockSpec((C, rw), lambda b, c: (f(b, c), EV_RB // rw)),
        ]

    in_specs = specs(fwd) + specs(bwd) + [
        pl.BlockSpec((rw, 2 * qw), lambda b, c: (0, 0)),
        pl.BlockSpec((1, 2 * qw), lambda b, c: (0, 0)),
    ]
    args = [proj] * 8 + [wg_full, bg_full]
    has_s0 = s0 is not None
    if has_s0:
        in_specs.append(pl.BlockSpec((None, None, 2, H_B, DK_B, DV_B),
                                     lambda b, c: (b, layer, 0, 0, 0, 0)))
        args.append(s0)
    return pl.pallas_call(
        functools.partial(_gla_kernel, nc=nc, has_s0=has_s0),
        grid=(batch, nc),
        in_specs=in_specs,
        out_specs=[
            pl.BlockSpec((C, vw), lambda b, c: (fwd(b, c), 0)),
            pl.BlockSpec((C, vw), lambda b, c: (bwd(b, c), 0)),
            pl.BlockSpec((None, 2, H_B, DK_B, DV_B), lambda b, c: (b, 0, 0, 0, 0)),
        ],
        out_shape=[
            jax.ShapeDtypeStruct((m, vw), F32),
            jax.ShapeDtypeStruct((m, vw), F32),
            jax.ShapeDtypeStruct((batch, 2, H_B, DK_B, DV_B), F32),
        ],
        scratch_shapes=[pltpu.VMEM((2, H_B, DK_B, DV_B), F32)],
        compiler_params=_cparams("parallel", "arbitrary"),
        name="gla_scan",
    )(*args)


def _gla_combine_kernel(of_ref, ob_ref, g_ref, gp_ref, o_ref):
    g = g_ref[...]
    for h in range(H_B):
        vl = slice(h * DV_B, (h + 1) * DV_B)
        o = _rms(of_ref[:, vl] + ob_ref[:, vl], g)
        o_ref[:, vl] = (o * _silu(gp_ref[:, vl].astype(F32))).astype(BF16)


def _gla_combine(o_f, o_b, gnorm, proj, tm):
    m, vw = o_f.shape
    spec = pl.BlockSpec((tm, vw), lambda i: (i, 0))
    return pl.pallas_call(
        _gla_combine_kernel,
        grid=(m // tm,),
        in_specs=[spec, spec,
                  pl.BlockSpec((1, DV_B), lambda i: (0, 0)),
                  pl.BlockSpec((tm, vw), lambda i: (i, EV_GPB // vw))],
        out_specs=spec,
        out_shape=jax.ShapeDtypeStruct((m, vw), BF16),
        compiler_params=_cparams("parallel"),
        name="gla_combine",
    )(o_f, o_b, gnorm.reshape(1, DV_B), proj)


def _na_bias_kernel(u_ref, o_ref):
    dlt = pl.program_id(0)
    W = GRID_W
    lane = lax.broadcasted_iota(jnp.int32, (W, 128), 1)
    for h in range(H_C):
        pieces = []
        for i in range(0, NA_KH, 2):
            ua = jnp.broadcast_to(u_ref[h, pl.ds(i - dlt + NA_KH - 1, 1), :], (W, 128))
            ub = jnp.broadcast_to(u_ref[h, pl.ds(i + 1 - dlt + NA_KH - 1, 1), :], (W, 128))
            ra = pltpu.roll(ua, 65, 1, stride=1, stride_axis=0)
            rb = pltpu.roll(ub, 1, 1, stride=1, stride_axis=0)
            pieces.append(jnp.where(lane < W, ra, rb))
        o_ref[0, h] = jnp.concatenate(pieces, axis=1)


def _na_bias_expand(u_tab):
    W = GRID_W
    return pl.pallas_call(
        _na_bias_kernel,
        grid=(NA_KH,),
        in_specs=[pl.BlockSpec(u_tab.shape, lambda i: (0, 0, 0))],
        out_specs=pl.BlockSpec((1, H_C, W, NA_KH * W), lambda i: (i, 0, 0, 0)),
        out_shape=jax.ShapeDtypeStruct((NA_KH, H_C, W, NA_KH * W), F32),
        compiler_params=_cparams("parallel"),
        name="na_bias_expand",
    )(u_tab)


def _na_kernel(*refs):
    q_ref = refs[0]
    k_refs = refs[1:1 + NA_KH]
    v_refs = refs[1 + NA_KH:1 + 2 * NA_KH]
    bias_ref, ck_ref, cv_ref, gp_ref, o_ref = refs[1 + 2 * NA_KH:]
    W = GRID_W
    nk = NA_KH * W
    qcol = lax.broadcasted_iota(jnp.int32, (W, nk), 0)
    kcol = lax.broadcasted_iota(jnp.int32, (W, nk), 1) % W
    cs = jnp.clip(qcol - NA_KW // 2, 0, W - NA_KW)
    col_ok = (kcol >= cs) & (kcol < cs + NA_KW)
    outs = []
    for h in range(H_C):
        sl = slice(h * HEAD_DIM, (h + 1) * HEAD_DIM)
        q = q_ref[:, sl].astype(BF16)
        kw = jnp.concatenate([r[:, sl] for r in k_refs], axis=0).astype(BF16)
        vw = jnp.concatenate([r[:, sl] for r in v_refs], axis=0).astype(BF16)
        ck = ck_ref[:, sl].astype(BF16)
        cv = cv_ref[:, sl].astype(BF16)
        s_nb = jnp.where(col_ok, _dot_nt(q, kw) * ATTN_SCALE + bias_ref[0, h], NEG)
        s_ctx = _dot_nt(q, ck) * ATTN_SCALE
        m = jnp.maximum(jnp.max(s_nb, axis=-1, keepdims=True), jnp.max(s_ctx, axis=-1, keepdims=True))
        e_nb = jnp.exp(s_nb - m)
        e_ctx = jnp.exp(s_ctx - m)
        den = jnp.sum(e_nb, axis=-1, keepdims=True) + jnp.sum(e_ctx, axis=-1, keepdims=True)
        outs.append((_dot(e_nb.astype(BF16), vw) + _dot(e_ctx.astype(BF16), cv)) / den)
    y = jnp.concatenate(outs, axis=1)
    o_ref[...] = (y * _silu(gp_ref[...].astype(F32))).astype(BF16)


def _neighbourhood_attention(proj, bias_exp, cache_k, cache_v, layer, batch, T):
    W = GRID_W
    rows = T // W
    kh = NA_KH
    qw = H_C * HEAD_DIM
    P = cache_k.shape[2]

    def rs(r):
        return jnp.clip(r - kh // 2, 0, rows - kh)

    def win_spec(col, i):
        return pl.BlockSpec((W, qw), lambda b, r: (b * rows + rs(r) + i, col // qw))

    in_specs = ([pl.BlockSpec((W, qw), lambda b, r: (b * rows + r, OD_QC // qw))]
                + [win_spec(OD_KC, i) for i in range(kh)]
                + [win_spec(OD_VC, i) for i in range(kh)]
                + [pl.BlockSpec((1, H_C, W, kh * W), lambda b, r: (r - rs(r), 0, 0, 0)),
                   pl.BlockSpec((None, None, P, qw), lambda b, r: (b, layer, 0, 0)),
                   pl.BlockSpec((None, None, P, qw), lambda b, r: (b, layer, 0, 0)),
                   pl.BlockSpec((W, qw), lambda b, r: (b * rows + r, OD_GPC // qw))])
    return pl.pallas_call(
        _na_kernel,
        grid=(batch, rows),
        in_specs=in_specs,
        out_specs=pl.BlockSpec((W, qw), lambda b, r: (b * rows + r, 0)),
        out_shape=jax.ShapeDtypeStruct((batch * T, qw), BF16),
        compiler_params=_cparams("parallel", "parallel"),
        name="neighbourhood_attention",
    )(proj, *([proj] * (2 * kh)), bias_exp, cache_k, cache_v, proj)


def _gmlp_kernel(u_ref, v_ref, gp_ref, g_ref, ws_ref, bt_ref, o_ref, *, chunks):
    gw = u_ref.shape[-1] // G_D
    for ch in range(chunks):
        rs_ = slice(ch * D_CHUNK, (ch + 1) * D_CHUNK)
        v = v_ref[rs_, :].astype(F32)
        mu = jnp.mean(v, axis=-1, keepdims=True)
        var = jnp.mean(jnp.square(v - mu), axis=-1, keepdims=True)
        vn = ((v - mu) * lax.rsqrt(var + 1e-5) * g_ref[...]).astype(BF16)
        for g in range(G_D):
            cl = slice(g * gw, (g + 1) * gw)
            sp = _dot(ws_ref[g], vn[:, cl]) + bt_ref[:, g:g + 1]
            y = u_ref[rs_, cl].astype(F32) * sp
            o_ref[rs_, cl] = (y * _silu(gp_ref[rs_, cl].astype(F32))).astype(BF16)


def _gmlp(proj, gnorm, ws_bf, b_t, tm):
    m = proj.shape[0]
    uw = HALF_W
    return pl.pallas_call(
        functools.partial(_gmlp_kernel, chunks=tm // D_CHUNK),
        grid=(m // tm,),
        in_specs=[
            pl.BlockSpec((tm, uw), lambda i: (i, OD_U // uw)),
            pl.BlockSpec((tm, uw), lambda i: (i, OD_V // uw)),
            pl.BlockSpec((tm, uw), lambda i: (i, OD_GPD // uw)),
            pl.BlockSpec((1, uw), lambda i: (0, 0)),
            pl.BlockSpec(ws_bf.shape, lambda i: (0, 0, 0)),
            pl.BlockSpec(b_t.shape, lambda i: (0, 0)),
        ],
        out_specs=pl.BlockSpec((tm, uw), lambda i: (i, 0)),
        out_shape=jax.ShapeDtypeStruct((m, uw), BF16),
        compiler_params=_cparams("parallel"),
        name="gmlp",
    )(proj, proj, proj, gnorm.reshape(1, uw), ws_bf, b_t)


def _rope_tables(T):
    t = np.arange(T)
    nf = HEAD_DIM // 4
    inv = ROPE_BASE ** (-jnp.arange(nf, dtype=F32) / nf)
    pos = jnp.stack([t // GRID_W, t % GRID_W], axis=-1).astype(F32)
    ang = pos[:, :, None] * inv
    cos, sin = jnp.cos(ang), jnp.sin(ang)
    cos_tab = jnp.concatenate([cos, cos], axis=-1).reshape(T, HEAD_DIM)
    sin_tab = jnp.concatenate([-sin, sin], axis=-1).reshape(T, HEAD_DIM)
    return cos_tab, sin_tab


def _even_weight(w):
    k = w.shape[0]
    gp0 = 3616
    return jnp.concatenate([
        w[:, 0:1024], w[:, gp0:gp0 + 1024], w[:, 2560:3584], w[:, gp0 + 1024:gp0 + 2048],
        w[:, 1536:2048], w[:, 2048:2560], w[:, 1024:1280], w[:, 1280:1536],
        w[:, 3584:3616], jnp.zeros((k, N_EVEN_COLS - EV_RB - 2 * GLA_RANK), w.dtype),
    ], axis=1).astype(BF16)


def kernel(x_prompt, x_sample, cache_attn_k, cache_attn_v, state_gla, cache_na_k, cache_na_v,
           c, c_ctx, ada_w, ada_b, norm_g, w_in_even, w_in_odd, w_out, attn_sink,
           gla_wg, gla_bg, gla_norm_g, na_bias, gmlp_norm_g, gmlp_ws, gmlp_b, final_norm_g):
    B, L, D = x_prompt.shape
    Bd, T, _ = x_sample.shape
    depth = ada_w.shape[0]
    n_even = w_in_even.shape[0]
    n_odd = w_in_odd.shape[0]
    P = cache_attn_k.shape[2]
    Mc, Ml = B * L, Bd * T

    n_rows = 8 * ((1 + Bd + 7) // 8)
    cond = jnp.concatenate([c_ctx[None, :], c, jnp.zeros((n_rows - 1 - Bd, D), F32)], axis=0)
    mod = _modulation(cond, ada_w, ada_b)
    mod = mod.reshape(depth, n_rows, 1, 3 * D)

    w_even = [_even_weight(w_in_even[e]) for e in range(n_even)]
    w_odd = [w_in_odd[o].astype(BF16) for o in range(n_odd)]
    w_o = [w_out[l].astype(BF16) for l in range(depth)]
    hw = H_B * DK_B
    wg_full, bg_full = [], []
    for e in range(n_even):
        wg = jnp.zeros((128, 2 * hw), F32)
        wg = wg.at[0:GLA_RANK, 0:hw].set(gla_wg[e, 0])
        wg = wg.at[GLA_RANK:2 * GLA_RANK, hw:2 * hw].set(gla_wg[e, 1])
        wg_full.append(wg.astype(BF16))
        bg_full.append(gla_bg[e].reshape(1, 2 * hw))
    pad_l = 63 - (NA_KW - 1)
    u_tabs = [jnp.pad(na_bias[o], ((0, 0), (0, 0), (pad_l, 128 - pad_l - (2 * NA_KW - 1))), mode="edge")
              for o in range(n_odd)]
    ws_bf = [gmlp_ws[o].astype(BF16) for o in range(n_odd)]
    b_t = [gmlp_b[o].T for o in range(n_odd)]
    cos_tab, sin_tab = _rope_tables(T)
    ck_a = cache_attn_k.reshape(Bd, n_even, P, KV_A * HEAD_DIM)
    cv_a = cache_attn_v.reshape(Bd, n_even, P, KV_A * HEAD_DIM)
    ck_c = cache_na_k.reshape(Bd, n_odd, P, H_C * HEAD_DIM)
    cv_c = cache_na_v.reshape(Bd, n_odd, P, H_C * HEAD_DIM)

    def run_group(x, is_ctx):
        batch, seq = (B, L) if is_ctx else (Bd, T)
        m = batch * seq
        grp0 = 0 if is_ctx else 1
        rows_per_grp = m if is_ctx else seq
        tm_mm = min(1024, rows_per_grp)
        tm_op = min(512, rows_per_grp)
        proj_dtype = F32 if is_ctx else BF16
        sides = dict(ka=[], va=[], st=[], kc=[], vc=[])
        h = _prenorm(x, norm_g[0], mod[0], grp0, rows_per_grp, tm_op)
        out = None
        for l in range(depth):
            if l % 2 == 0:
                e = l // 2
                proj = _in_proj(h, w_even[e], proj_dtype, tm_mm, 640)
                if is_ctx:
                    ya = _ctx_attention(proj, attn_sink[e], batch, seq, EV_QA, EV_KA, EV_VA, EV_GPA,
                                        H_A, KV_A)
                    o_f, o_b, s_fin = _gla(proj, wg_full[e], bg_full[e], None, e, batch, seq)
                    sides["ka"].append(proj[:, EV_KA:EV_KA + KV_A * HEAD_DIM])
                    sides["va"].append(proj[:, EV_VA:EV_VA + KV_A * HEAD_DIM])
                    sides["st"].append(s_fin)
                else:
                    ya = _window_attention(proj, attn_sink[e], ck_a, cv_a, e, cos_tab, sin_tab, batch, seq)
                    o_f, o_b, _ = _gla(proj, wg_full[e], bg_full[e], state_gla, e, batch, seq)
                yb = _gla_combine(o_f, o_b, gla_norm_g[e], proj, tm_op)
            else:
                o = l // 2
                proj = _in_proj(h, w_odd[o], proj_dtype, tm_mm, 512)
                if is_ctx:
                    ya = _ctx_attention(proj, None, batch, seq, OD_QC, OD_KC, OD_VC, OD_GPC, H_C, H_C)
                    sides["kc"].append(proj[:, OD_KC:OD_KC + H_C * HEAD_DIM])
                    sides["vc"].append(proj[:, OD_VC:OD_VC + H_C * HEAD_DIM])
                else:
                    bias_exp = _na_bias_expand(u_tabs[o])
                    ya = _neighbourhood_attention(proj, bias_exp, ck_c, cv_c, o, batch, seq)
                yb = _gmlp(proj, gmlp_norm_g[o], ws_bf[o], b_t[o], tm_op)
            final = l == depth - 1
            g_next = final_norm_g if final else norm_g[l + 1]
            mod_next = mod[l] if final else mod[l + 1]
            res = _out_proj(ya, yb, w_o[l], x, mod[l], g_next, mod_next, grp0, rows_per_grp, tm_op, final)
            if final:
                out = res[0]
            else:
                x, h = res
        return out, sides

    y_prompt, sides = run_group(x_prompt.reshape(Mc, D), True)
    y_sample, _ = run_group(x_sample.reshape(Ml, D), False)

    new_attn_k = jnp.stack([a.reshape(B, L, KV_A, HEAD_DIM) for a in sides["ka"]], axis=1)
    new_attn_v = jnp.stack([a.reshape(B, L, KV_A, HEAD_DIM) for a in sides["va"]], axis=1)
    new_gla_state = jnp.stack(sides["st"], axis=1)
    new_na_k = jnp.stack([a.reshape(B, L, H_C, HEAD_DIM) for a in sides["kc"]], axis=1)
    new_na_v = jnp.stack([a.reshape(B, L, H_C, HEAD_DIM) for a in sides["vc"]], axis=1)
    return (y_prompt.reshape(B, L, D), y_sample.reshape(Bd, T, D),
            new_attn_k, new_attn_v, new_gla_state, new_na_k, new_na_v)
```

```python
import functools

import numpy as np
import jax
import jax.numpy as jnp
from jax import lax
from jax.experimental import pallas as pl
from jax.experimental.pallas import tpu as pltpu

F32 = jnp.float32
BF16 = jnp.bfloat16

D_MODEL = 2048
HEAD_DIM = 128
GRID_W = 64
HALF_W = D_MODEL // 2
H_A = 8
KV_A = 2
WINDOW = 128
BAND_BLK = 128
H_B = 4
DK_B = 128
DV_B = 256
GLA_RANK = 16
GLA_NORMALIZER = 16.0
GLA_CHUNK = 64
GLA_GROUP = 4
H_C = 8
NA_KH = 8
NA_KW = 16
NA_QR = 4
NA_WR = 12
G_D = 4
D_CHUNK = 128
ROPE_BASE = 10000.0
NEG = -1e30
LOG2E = 1.4426950408889634
Q_FOLD = HEAD_DIM ** -0.5 * LOG2E

EV_QA, EV_GPA, EV_VB, EV_GPB = 0, 1024, 2048, 3072
EV_QB, EV_KB, EV_KA, EV_VA, EV_RB = 4096, 4608, 5120, 5376, 5632
N_EVEN_COLS = 5760
OD_QC, OD_KC, OD_VC, OD_U, OD_V, OD_GPC, OD_GPD = 0, 1024, 2048, 3072, 4096, 5120, 6144
N_ODD_COLS = 7168

VMEM_LIMIT = 52 * 1024 * 1024


def _cparams(*sem):
    return pltpu.CompilerParams(dimension_semantics=sem, vmem_limit_bytes=VMEM_LIMIT)


def _silu(x):
    return x / (1.0 + jnp.exp(-x))


def _dot(a, b):
    return jnp.dot(a, b, preferred_element_type=F32)


def _dot_nt(a, b):
    return lax.dot_general(a, b, (((1,), (1,)), ((), ())), preferred_element_type=F32)


def _rms(x, g, eps=1e-6):
    return x * lax.rsqrt(jnp.mean(x * x, axis=-1, keepdims=True) + eps) * g


def _mod_kernel(cond_ref, w_ref, b_ref, o_ref):
    a = _silu(cond_ref[...]).astype(BF16)
    o_ref[0] = _dot(a, w_ref[0].astype(BF16)) + b_ref[0]


def _modulation(cond, ada_w, ada_b):
    depth, d, n = ada_w.shape
    rows = cond.shape[0]
    tn = 1024
    return pl.pallas_call(
        _mod_kernel,
        grid=(depth, n // tn),
        in_specs=[
            pl.BlockSpec((rows, d), lambda l, j: (0, 0)),
            pl.BlockSpec((1, d, tn), lambda l, j: (l, 0, j)),
            pl.BlockSpec((1, 1, tn), lambda l, j: (l, 0, j)),
        ],
        out_specs=pl.BlockSpec((1, rows, tn), lambda l, j: (l, 0, j)),
        out_shape=jax.ShapeDtypeStruct((depth, rows, n), F32),
        compiler_params=_cparams("parallel", "parallel"),
        name="modulation",
    )(cond, ada_w, ada_b.reshape(depth, 1, n))


def _prenorm_kernel(x_ref, g_ref, mod_ref, h_ref):
    d = x_ref.shape[-1]
    y = _rms(x_ref[...], g_ref[...])
    shift = mod_ref[0, :, 0:d]
    scale = mod_ref[0, :, d:2 * d]
    h_ref[...] = (y * (1.0 + scale) + shift).astype(BF16)


def _prenorm(x, g, mod, grp0, rows_per_grp, tm):
    m, d = x.shape
    tpg = rows_per_grp // tm
    return pl.pallas_call(
        _prenorm_kernel,
        grid=(m // tm,),
        in_specs=[
            pl.BlockSpec((tm, d), lambda i: (i, 0)),
            pl.BlockSpec((1, d), lambda i: (0, 0)),
            pl.BlockSpec((1, 1, 3 * d), lambda i: (grp0 + i // tpg, 0, 0)),
        ],
        out_specs=pl.BlockSpec((tm, d), lambda i: (i, 0)),
        out_shape=jax.ShapeDtypeStruct((m, d), BF16),
        compiler_params=_cparams("parallel"),
        name="prenorm",
    )(x, g.reshape(1, d), mod)


def _matmul_kernel(h_ref, w_ref, o_ref):
    o_ref[...] = _dot(h_ref[...], w_ref[...]).astype(o_ref.dtype)


def _in_proj(h, w, out_dtype, tm, tn):
    m, k = h.shape
    n = w.shape[1]
    return pl.pallas_call(
        _matmul_kernel,
        grid=(m // tm, n // tn),
        in_specs=[
            pl.BlockSpec((tm, k), lambda i, j: (i, 0)),
            pl.BlockSpec((k, tn), lambda i, j: (0, j)),
        ],
        out_specs=pl.BlockSpec((tm, tn), lambda i, j: (i, j)),
        out_shape=jax.ShapeDtypeStruct((m, n), out_dtype),
        compiler_params=_cparams("parallel", "parallel"),
        name="in_proj",
    )(h, w)


def _outproj_kernel(ya_ref, yb_ref, w_ref, x_ref, mod_ref, g_ref, modn_ref, *out_refs, final):
    d = x_ref.shape[-1]
    half = ya_ref.shape[-1]
    acc = _dot(ya_ref[...], w_ref[0:half, :]) + _dot(yb_ref[...], w_ref[half:2 * half, :])
    gate = mod_ref[0, :, 2 * d:3 * d]
    xn = x_ref[...] + gate * acc
    if final:
        out_refs[0][...] = _rms(xn, g_ref[...])
    else:
        out_refs[0][...] = xn
        shift = modn_ref[0, :, 0:d]
        scale = modn_ref[0, :, d:2 * d]
        out_refs[1][...] = (_rms(xn, g_ref[...]) * (1.0 + scale) + shift).astype(BF16)


def _out_proj(ya, yb, w, x, mod, g_next, mod_next, grp0, rows_per_grp, tm, final):
    m, d = x.shape
    half = ya.shape[1]
    tpg = rows_per_grp // tm
    modspec = pl.BlockSpec((1, 1, 3 * d), lambda i: (grp0 + i // tpg, 0, 0))
    rowspec = pl.BlockSpec((tm, d), lambda i: (i, 0))
    halfspec = pl.BlockSpec((tm, half), lambda i: (i, 0))
    if final:
        out_shape = [jax.ShapeDtypeStruct((m, d), F32)]
        out_specs = [rowspec]
    else:
        out_shape = [jax.ShapeDtypeStruct((m, d), F32), jax.ShapeDtypeStruct((m, d), BF16)]
        out_specs = [rowspec, rowspec]
    return pl.pallas_call(
        functools.partial(_outproj_kernel, final=final),
        grid=(m // tm,),
        in_specs=[
            halfspec, halfspec,
            pl.BlockSpec((2 * half, d), lambda i: (0, 0)),
            rowspec, modspec,
            pl.BlockSpec((1, d), lambda i: (0, 0)),
            modspec,
        ],
        out_specs=out_specs,
        out_shape=out_shape,
        compiler_params=_cparams("parallel"),
        name="out_proj",
    )(ya, yb, w, x, mod, g_next.reshape(1, d), mod_next)


def _ctx_attn_kernel(*refs, n_heads, n_kv, has_sink):
    if has_sink:
        sink_ref, q_ref, k_ref, v_ref, gp_ref, o_ref = refs
    else:
        q_ref, k_ref, v_ref, gp_ref, o_ref = refs
    L = q_ref.shape[0]
    grp = n_heads // n_kv
    outs = [None] * n_heads
    for kv in range(n_kv):
        heads = list(range(kv * grp, (kv + 1) * grp))
        qg = jnp.concatenate(
            [q_ref[:, h * HEAD_DIM:(h + 1) * HEAD_DIM].astype(BF16) for h in heads], axis=0)
        k = k_ref[:, kv * HEAD_DIM:(kv + 1) * HEAD_DIM].astype(BF16)
        v = v_ref[:, kv * HEAD_DIM:(kv + 1) * HEAD_DIM].astype(BF16)
        s = _dot_nt(qg, k)
        m = jnp.max(s, axis=-1, keepdims=True)
        if has_sink:
            sk = jnp.concatenate(
                [jnp.full((L, 1), sink_ref[h] * LOG2E, F32) for h in heads], axis=0)
            m = jnp.maximum(m, sk)
        e = jnp.exp2(s - m)
        den = jnp.sum(e, axis=-1, keepdims=True)
        if has_sink:
            den = den + jnp.exp2(sk - m)
        o = _dot(e.astype(BF16), v) / den
        for i, h in enumerate(heads):
            outs[h] = o[i * L:(i + 1) * L, :]
    y = jnp.concatenate(outs, axis=1)
    o_ref[...] = (y * _silu(gp_ref[...].astype(F32))).astype(BF16)


def _ctx_attention(proj, sink, batch, L, q_col, k_col, v_col, gp_col, n_heads, n_kv):
    qw = n_heads * HEAD_DIM
    kw = n_kv * HEAD_DIM
    has_sink = sink is not None
    in_specs = [
        pl.BlockSpec((L, qw), lambda b: (b, q_col // qw)),
        pl.BlockSpec((L, kw), lambda b: (b, k_col // kw)),
        pl.BlockSpec((L, kw), lambda b: (b, v_col // kw)),
        pl.BlockSpec((L, qw), lambda b: (b, gp_col // qw)),
    ]
    args = [proj, proj, proj, proj]
    if has_sink:
        in_specs = [pl.BlockSpec(memory_space=pltpu.SMEM)] + in_specs
        args = [sink] + args
    return pl.pallas_call(
        functools.partial(_ctx_attn_kernel, n_heads=n_heads, n_kv=n_kv, has_sink=has_sink),
        grid=(batch,),
        in_specs=in_specs,
        out_specs=pl.BlockSpec((L, qw), lambda b: (b, 0)),
        out_shape=jax.ShapeDtypeStruct((batch * L, qw), BF16),
        compiler_params=_cparams("parallel"),
        name="ctx_attention",
    )(*args)


def _rope(x, cos, sin_signed):
    lane = lax.broadcasted_iota(jnp.int32, x.shape, 1)
    first = (lane % 64) < 32
    swapped = jnp.where(first, pltpu.roll(x, 96, 1), pltpu.roll(x, 32, 1))
    return x * cos + swapped * sin_signed


def _win_attn_kernel(sink_ref, q_ref, kp_ref, kc_ref, kn_ref, vp_ref, vc_ref, vn_ref,
                     cq_ref, sq_ref, cp_ref, sp_ref, cn_ref, sn_ref, mask_ref,
                     ck_ref, cv_ref, gp_ref, o_ref, *, nb):
    n = pl.program_id(1)
    blk = BAND_BLK
    grp = H_A // KV_A
    cq, sq = cq_ref[...], sq_ref[...]
    no_prev = jnp.where(n > 0, 0.0, NEG)
    no_next = jnp.where(n < nb - 1, 0.0, NEG)
    band_mask = jnp.concatenate([mask_ref[:, 0:blk] + no_prev, mask_ref[:, blk:2 * blk],
                                 mask_ref[:, 2 * blk:3 * blk] + no_next], axis=1)
    band_mask = jnp.concatenate([band_mask] * grp, axis=0)
    outs = [None] * H_A
    for kv in range(KV_A):
        sl = slice(kv * HEAD_DIM, (kv + 1) * HEAD_DIM)
        kb = jnp.concatenate([
            _rope(kp_ref[:, sl].astype(F32), cp_ref[...], sp_ref[...]),
            _rope(kc_ref[:, sl].astype(F32), cq, sq),
            _rope(kn_ref[:, sl].astype(F32), cn_ref[...], sn_ref[...]),
        ], axis=0).astype(BF16)
        vb = jnp.concatenate([vp_ref[:, sl], vc_ref[:, sl], vn_ref[:, sl]], axis=0).astype(BF16)
        heads = list(range(kv * grp, (kv + 1) * grp))
        qg = jnp.concatenate([
            _rope(q_ref[:, h * HEAD_DIM:(h + 1) * HEAD_DIM].astype(F32), cq, sq) for h in heads
        ], axis=0).astype(BF16)
        ck = ck_ref[:, sl]
        cv = cv_ref[:, sl]
        s_band = _dot_nt(qg, kb) + band_mask
        s_ctx = _dot_nt(qg, ck)
        sk = jnp.concatenate([jnp.full((blk, 1), sink_ref[h] * LOG2E, F32) for h in heads], axis=0)
        m = jnp.maximum(jnp.maximum(jnp.max(s_band, axis=-1, keepdims=True),
                                    jnp.max(s_ctx, axis=-1, keepdims=True)), sk)
        e_band = jnp.exp2(s_band - m)
        e_ctx = jnp.exp2(s_ctx - m)
        den = (jnp.sum(e_band, axis=-1, keepdims=True) + jnp.sum(e_ctx, axis=-1, keepdims=True)
               + jnp.exp2(sk - m))
        o = (_dot(e_band.astype(BF16), vb) + _dot(e_ctx.astype(BF16), cv)) / den
        for i, h in enumerate(heads):
            outs[h] = o[i * blk:(i + 1) * blk, :]
    y = jnp.concatenate(outs, axis=1)
    o_ref[...] = (y * _silu(gp_ref[...].astype(F32))).astype(BF16)


def _window_attention(proj, sink, cache_k, cache_v, layer, cos_tab, sin_tab, batch, T):
    blk = BAND_BLK
    nb = T // blk
    qw = H_A * HEAD_DIM
    kw = KV_A * HEAD_DIM
    P = cache_k.shape[2]

    def row(b, n):
        return b * nb + n

    def prev(n):
        return jnp.maximum(n - 1, 0)

    def nxt(n):
        return jnp.minimum(n + 1, nb - 1)

    kcol, vcol = EV_KA // kw, EV_VA // kw
    tab = lambda f: pl.BlockSpec((blk, HEAD_DIM), lambda b, n: (f(n), 0))
    ident = lambda n: n
    in_specs = [
        pl.BlockSpec(memory_space=pltpu.SMEM),
        pl.BlockSpec((blk, qw), lambda b, n: (row(b, n), EV_QA // qw)),
        pl.BlockSpec((blk, kw), lambda b, n: (row(b, prev(n)), kcol)),
        pl.BlockSpec((blk, kw), lambda b, n: (row(b, n), kcol)),
        pl.BlockSpec((blk, kw), lambda b, n: (row(b, nxt(n)), kcol)),
        pl.BlockSpec((blk, kw), lambda b, n: (row(b, prev(n)), vcol)),
        pl.BlockSpec((blk, kw), lambda b, n: (row(b, n), vcol)),
        pl.BlockSpec((blk, kw), lambda b, n: (row(b, nxt(n)), vcol)),
        tab(ident), tab(ident), tab(prev), tab(prev), tab(nxt), tab(nxt),
        pl.BlockSpec((blk, 3 * blk), lambda b, n: (0, 0)),
        pl.BlockSpec((None, None, P, kw), lambda b, n: (b, layer, 0, 0)),
        pl.BlockSpec((None, None, P, kw), lambda b, n: (b, layer, 0, 0)),
        pl.BlockSpec((blk, qw), lambda b, n: (row(b, n), EV_GPA // qw)),
    ]
    return pl.pallas_call(
        functools.partial(_win_attn_kernel, nb=nb),
        grid=(batch, nb),
        in_specs=in_specs,
        out_specs=pl.BlockSpec((blk, qw), lambda b, n: (row(b, n), 0)),
        out_shape=jax.ShapeDtypeStruct((batch * T, qw), BF16),
        compiler_params=_cparams("parallel", "parallel"),
        name="window_attention",
    )(sink, proj, proj, proj, proj, proj, proj, proj,
      cos_tab, sin_tab, cos_tab, sin_tab, cos_tab, sin_tab, _band_mask(),
      cache_k, cache_v, proj)


def _band_mask():
    iq = np.arange(BAND_BLK)[:, None]
    jk = np.arange(3 * BAND_BLK)[None, :]
    rel = iq - jk + BAND_BLK
    return jnp.asarray(np.where(np.abs(rel) <= WINDOW, 0.0, NEG), F32)


def _log_sigmoid(x):
    return jnp.minimum(x, 0.0) - jnp.log1p(jnp.exp(-jnp.abs(x)))


def _cumsum_rows(tri, g):
    g_hi = g.astype(BF16)
    r1 = g - g_hi.astype(F32)
    g_mid = r1.astype(BF16)
    g_lo = (r1 - g_mid.astype(F32)).astype(BF16)
    return _dot(tri, g_hi) + _dot(tri, g_mid) + _dot(tri, g_lo)


def _gla_kernel(*refs, nsteps, has_s0):
    if has_s0:
        (qf, kf, vf, rf, qr, kr, vr, rr, wg_ref, bg_ref, s0_ref,
         of_ref, ob_ref, sfin_ref, s_scr) = refs
    else:
        (qf, kf, vf, rf, qr, kr, vr, rr, wg_ref, bg_ref,
         of_ref, ob_ref, sfin_ref, s_scr) = refs
    step = pl.program_id(1)
    C = GLA_CHUNK
    G = qf.shape[0] // C
    R = G * C

    @pl.when(step == 0)
    def _():
        if has_s0:
            s_scr[...] = s0_ref[...]
        else:
            s_scr[...] = jnp.zeros(s_scr.shape, F32)

    ri = lax.broadcasted_iota(jnp.int32, (R, R), 0)
    ci = lax.broadcasted_iota(jnp.int32, (R, R), 1)
    same_chunk = (ri // C) == (ci // C)
    hw = H_B * DK_B
    for d, (q_ref, k_ref, v_ref, r_ref, o_ref) in enumerate(
            ((qf, kf, vf, rf, of_ref), (qr, kr, vr, rr, ob_ref))):
        keep = same_chunk & ((ci <= ri) if d == 0 else (ci >= ri))
        tri = jnp.where(keep, 1.0, 0.0).astype(BF16)
        logit = _dot(r_ref[...].astype(BF16), wg_ref[:, d * hw:(d + 1) * hw]) + bg_ref[:, d * hw:(d + 1) * hw]
        g = _log_sigmoid(logit) / GLA_NORMALIZER
        bc = _cumsum_rows(tri, g)
        edge = C - 1 if d == 0 else 0
        totals = [bc[j * C + edge:j * C + edge + 1, :] for j in range(G)]
        tot_full = jnp.concatenate([jnp.broadcast_to(t, (C, hw)) for t in totals], axis=0)
        e_pos = jnp.exp(bc)
        e_neg = jnp.exp(-bc)
        e_rem = jnp.exp(tot_full - bc)
        e_tot = [jnp.exp(t) for t in totals]
        order = range(G) if d == 0 else range(G - 1, -1, -1)
        for h in range(H_B):
            sl = slice(h * DK_B, (h + 1) * DK_B)
            vl = slice(h * DV_B, (h + 1) * DV_B)
            qh = q_ref[:, sl].astype(F32) * (DK_B ** -0.5)
            kh = k_ref[:, sl].astype(F32)
            vh = v_ref[:, vl].astype(BF16)
            qi = (qh * e_pos[:, sl]).astype(BF16)
            ki = (kh * e_neg[:, sl]).astype(BF16)
            ks = kh * e_rem[:, sl]
            att = jnp.where(keep, _dot_nt(qi, ki), 0.0).astype(BF16)
            o_intra = _dot(att, vh)
            S = s_scr[d, h]
            for j in order:
                rows = slice(j * C, (j + 1) * C)
                o_ref[rows, vl] = o_intra[rows, :] + _dot(qi[rows, :], S.astype(BF16))
                stacked = jnp.concatenate(
                    [ks[rows, :], jnp.broadcast_to(e_tot[j][:, sl], (C, DK_B))], axis=0)
                tr = stacked.T
                S = tr[:, C:C + 1] * S + _dot(tr[:, 0:C].astype(BF16), vh[rows, :])
            s_scr[d, h] = S

    @pl.when(step == nsteps - 1)
    def _():
        sfin_ref[...] = s_scr[...]


def _gla(proj, wg_full, bg_full, s0, layer, batch, T):
    C = GLA_GROUP * GLA_CHUNK
    nc = T // C
    m = batch * T
    qw, vw, rw = H_B * DK_B, H_B * DV_B, 128

    def fwd(b, c):
        return b * nc + c

    def bwd(b, c):
        return b * nc + (nc - 1 - c)

    def specs(f):
        return [
            pl.BlockSpec((C, qw), lambda b, c: (f(b, c), EV_QB // qw)),
            pl.BlockSpec((C, qw), lambda b, c: (f(b, c), EV_KB // qw)),
            pl.BlockSpec((C, vw), lambda b, c: (f(b, c), EV_VB // vw)),
            pl.BlockSpec((C, rw), lambda b, c: (f(b, c), EV_RB // rw)),
        ]

    in_specs = specs(fwd) + specs(bwd) + [
        pl.BlockSpec((rw, 2 * qw), lambda b, c: (0, 0)),
        pl.BlockSpec((1, 2 * qw), lambda b, c: (0, 0)),
    ]
    args = [proj] * 8 + [wg_full, bg_full]
    has_s0 = s0 is not None
    if has_s0:
        in_specs.append(pl.BlockSpec((None, None, 2, H_B, DK_B, DV_B),
                                     lambda b, c: (b, layer, 0, 0, 0, 0)))
        args.append(s0)
    return pl.pallas_call(
        functools.partial(_gla_kernel, nsteps=nc, has_s0=has_s0),
        grid=(batch, nc),
        in_specs=in_specs,
        out_specs=[
            pl.BlockSpec((C, vw), lambda b, c: (fwd(b, c), 0)),
            pl.BlockSpec((C, vw), lambda b, c: (bwd(b, c), 0)),
            pl.BlockSpec((None, 2, H_B, DK_B, DV_B), lambda b, c: (b, 0, 0, 0, 0)),
        ],
        out_shape=[
            jax.ShapeDtypeStruct((m, vw), F32),
            jax.ShapeDtypeStruct((m, vw), F32),
            jax.ShapeDtypeStruct((batch, 2, H_B, DK_B, DV_B), F32),
        ],
        scratch_shapes=[pltpu.VMEM((2, H_B, DK_B, DV_B), F32)],
        compiler_params=_cparams("parallel", "arbitrary"),
        name="gla_scan",
    )(*args)


def _gla_combine_kernel(of_ref, ob_ref, g_ref, gp_ref, o_ref):
    g = g_ref[...]
    for h in range(H_B):
        vl = slice(h * DV_B, (h + 1) * DV_B)
        o = _rms(of_ref[:, vl] + ob_ref[:, vl], g)
        o_ref[:, vl] = (o * _silu(gp_ref[:, vl].astype(F32))).astype(BF16)


def _gla_combine(o_f, o_b, gnorm, proj, tm):
    m, vw = o_f.shape
    spec = pl.BlockSpec((tm, vw), lambda i: (i, 0))
    return pl.pallas_call(
        _gla_combine_kernel,
        grid=(m // tm,),
        in_specs=[spec, spec,
                  pl.BlockSpec((1, DV_B), lambda i: (0, 0)),
                  pl.BlockSpec((tm, vw), lambda i: (i, EV_GPB // vw))],
        out_specs=spec,
        out_shape=jax.ShapeDtypeStruct((m, vw), BF16),
        compiler_params=_cparams("parallel"),
        name="gla_combine",
    )(o_f, o_b, gnorm.reshape(1, DV_B), proj)


def _na_bias_kernel(u_ref, o_ref):
    place = pl.program_id(0)
    W = GRID_W
    lane = lax.broadcasted_iota(jnp.int32, (W, 128), 1)
    qcol = lax.broadcasted_iota(jnp.int32, (W, 128), 0)
    kcol = lane % W
    cs = jnp.clip(qcol - NA_KW // 2, 0, W - NA_KW)
    col_ok = (kcol >= cs) & (kcol < cs + NA_KW)
    for t in range(NA_QR):
        first = jnp.where(place == 0, 0, jnp.where(place == 1, t, NA_WR - NA_KH))

        def piece(i, base_shift):
            dr = i - t + (NA_KH - 1) - (NA_WR - NA_KH) * place
            in_rows = (i >= first) & (i < first + NA_KH)
            u = jnp.broadcast_to(u_ref[0, pl.ds(jnp.clip(dr, 0, 2 * NA_KH - 2), 1), :], (W, 128))
            rolled = pltpu.roll(u, base_shift, 1, stride=1, stride_axis=0)
            return rolled + jnp.where(in_rows, 0.0, NEG)

        tiles = []
        for i in range(0, NA_WR, 2):
            tile = jnp.where(lane < W, piece(i, 65), piece(i + 1, 1))
            tiles.append(jnp.where(col_ok, tile, NEG) * LOG2E)
        o_ref[0, 0, t * W:(t + 1) * W, :] = jnp.concatenate(tiles, axis=1)


def _na_bias_expand(u_tab):
    W = GRID_W
    n_tab = u_tab.shape[1]
    return pl.pallas_call(
        _na_bias_kernel,
        grid=(3, H_C),
        in_specs=[pl.BlockSpec((1, n_tab, 128), lambda p, h: (h, 0, 0))],
        out_specs=pl.BlockSpec((1, 1, NA_QR * W, NA_WR * W), lambda p, h: (p, h, 0, 0)),
        out_shape=jax.ShapeDtypeStruct((3, H_C, NA_QR * W, NA_WR * W), F32),
        compiler_params=_cparams("parallel", "parallel"),
        name="na_bias_expand",
    )(u_tab)


def _na_kernel(*refs):
    nblk = NA_WR // NA_QR
    q_ref = refs[0]
    k_refs = refs[1:1 + nblk]
    v_refs = refs[1 + nblk:1 + 2 * nblk]
    bias_ref, ck_ref, cv_ref, gp_ref, o_ref = refs[1 + 2 * nblk:]
    outs = []
    for h in range(H_C):
        sl = slice(h * HEAD_DIM, (h + 1) * HEAD_DIM)
        q = q_ref[:, sl]
        kw = jnp.concatenate([r[:, sl] for r in k_refs], axis=0)
        vw = jnp.concatenate([r[:, sl] for r in v_refs], axis=0)
        s_nb = _dot_nt(q, kw) + bias_ref[0, h]
        s_ctx = _dot_nt(q, ck_ref[:, sl])
        m = jnp.maximum(jnp.max(s_nb, axis=-1, keepdims=True), jnp.max(s_ctx, axis=-1, keepdims=True))
        e_nb = jnp.exp2(s_nb - m)
        e_ctx = jnp.exp2(s_ctx - m)
        den = jnp.sum(e_nb, axis=-1, keepdims=True) + jnp.sum(e_ctx, axis=-1, keepdims=True)
        outs.append((_dot(e_nb.astype(BF16), vw) + _dot(e_ctx.astype(BF16), cv_ref[:, sl])) / den)
    y = jnp.concatenate(outs, axis=1)
    o_ref[...] = (y * _silu(gp_ref[...].astype(F32))).astype(BF16)


def _neighbourhood_attention(proj, bias_exp, cache_k, cache_v, layer, batch, T):
    W = GRID_W
    rows = T // W
    assert rows % NA_QR == 0 and rows >= NA_WR and NA_WR % NA_QR == 0
    steps = rows // NA_QR
    nblk = NA_WR // NA_QR
    rq = NA_QR * W
    qw = H_C * HEAD_DIM
    P = cache_k.shape[2]

    def win0(p):
        return jnp.clip(p - 1, 0, steps - nblk)

    def place(p):
        return jnp.where(p == 0, 0, jnp.where(p == steps - 1, 2, 1))

    def win_spec(col, i):
        return pl.BlockSpec((rq, qw), lambda b, p: (b * steps + win0(p) + i, col // qw))

    in_specs = ([pl.BlockSpec((rq, qw), lambda b, p: (b * steps + p, OD_QC // qw))]
                + [win_spec(OD_KC, i) for i in range(nblk)]
                + [win_spec(OD_VC, i) for i in range(nblk)]
                + [pl.BlockSpec((1, H_C, rq, NA_WR * W), lambda b, p: (place(p), 0, 0, 0)),
                   pl.BlockSpec((None, None, P, qw), lambda b, p: (b, layer, 0, 0)),
                   pl.BlockSpec((None, None, P, qw), lambda b, p: (b, layer, 0, 0)),
                   pl.BlockSpec((rq, qw), lambda b, p: (b * steps + p, OD_GPC // qw))])
    return pl.pallas_call(
        _na_kernel,
        grid=(batch, steps),
        in_specs=in_specs,
        out_specs=pl.BlockSpec((rq, qw), lambda b, p: (b * steps + p, 0)),
        out_shape=jax.ShapeDtypeStruct((batch * T, qw), BF16),
        compiler_params=_cparams("parallel", "parallel"),
        name="neighbourhood_attention",
    )(proj, *([proj] * (2 * nblk)), bias_exp, cache_k, cache_v, proj)


def _gmlp_kernel(u_ref, v_ref, gp_ref, g_ref, ws_ref, bt_ref, o_ref, *, chunks):
    gw = u_ref.shape[-1] // G_D
    for ch in range(chunks):
        rs_ = slice(ch * D_CHUNK, (ch + 1) * D_CHUNK)
        v = v_ref[rs_, :].astype(F32)
        mu = jnp.mean(v, axis=-1, keepdims=True)
        var = jnp.mean(jnp.square(v - mu), axis=-1, keepdims=True)
        vn = ((v - mu) * lax.rsqrt(var + 1e-5) * g_ref[...]).astype(BF16)
        for g in range(G_D):
            cl = slice(g * gw, (g + 1) * gw)
            sp = _dot(ws_ref[g], vn[:, cl]) + bt_ref[:, g:g + 1]
            y = u_ref[rs_, cl].astype(F32) * sp
            o_ref[rs_, cl] = (y * _silu(gp_ref[rs_, cl].astype(F32))).astype(BF16)


def _gmlp(proj, gnorm, ws_bf, b_t, tm):
    m = proj.shape[0]
    uw = HALF_W
    return pl.pallas_call(
        functools.partial(_gmlp_kernel, chunks=tm // D_CHUNK),
        grid=(m // tm,),
        in_specs=[
            pl.BlockSpec((tm, uw), lambda i: (i, OD_U // uw)),
            pl.BlockSpec((tm, uw), lambda i: (i, OD_V // uw)),
            pl.BlockSpec((tm, uw), lambda i: (i, OD_GPD // uw)),
            pl.BlockSpec((1, uw), lambda i: (0, 0)),
            pl.BlockSpec(ws_bf.shape, lambda i: (0, 0, 0)),
            pl.BlockSpec(b_t.shape, lambda i: (0, 0)),
        ],
        out_specs=pl.BlockSpec((tm, uw), lambda i: (i, 0)),
        out_shape=jax.ShapeDtypeStruct((m, uw), BF16),
        compiler_params=_cparams("parallel"),
        name="gmlp",
    )(proj, proj, proj, gnorm.reshape(1, uw), ws_bf, b_t)


def _rope_tables(T):
    t = np.arange(T)
    nf = HEAD_DIM // 4
    inv = ROPE_BASE ** (-jnp.arange(nf, dtype=F32) / nf)
    pos = jnp.stack([t // GRID_W, t % GRID_W], axis=-1).astype(F32)
    ang = pos[:, :, None] * inv
    cos, sin = jnp.cos(ang), jnp.sin(ang)
    cos_tab = jnp.concatenate([cos, cos], axis=-1).reshape(T, HEAD_DIM)
    sin_tab = jnp.concatenate([-sin, sin], axis=-1).reshape(T, HEAD_DIM)
    return cos_tab, sin_tab


def _even_weight(w):
    k = w.shape[0]
    gp0 = 3616
    return jnp.concatenate([
        w[:, 0:1024] * Q_FOLD, w[:, gp0:gp0 + 1024], w[:, 2560:3584], w[:, gp0 + 1024:gp0 + 2048],
        w[:, 1536:2048], w[:, 2048:2560], w[:, 1024:1280], w[:, 1280:1536],
        w[:, 3584:3616], jnp.zeros((k, N_EVEN_COLS - EV_RB - 2 * GLA_RANK), w.dtype),
    ], axis=1).astype(BF16)


def kernel(x_prompt, x_sample, cache_attn_k, cache_attn_v, state_gla, cache_na_k, cache_na_v,
           c, c_ctx, ada_w, ada_b, norm_g, w_in_even, w_in_odd, w_out, attn_sink,
           gla_wg, gla_bg, gla_norm_g, na_bias, gmlp_norm_g, gmlp_ws, gmlp_b, final_norm_g):
    B, L, D = x_prompt.shape
    Bd, T, _ = x_sample.shape
    depth = ada_w.shape[0]
    n_even = w_in_even.shape[0]
    n_odd = w_in_odd.shape[0]
    P = cache_attn_k.shape[2]
    Mc, Ml = B * L, Bd * T

    n_rows = 8 * ((1 + Bd + 7) // 8)
    cond = jnp.concatenate([c_ctx[None, :], c, jnp.zeros((n_rows - 1 - Bd, D), F32)], axis=0)
    mod = _modulation(cond, ada_w, ada_b)
    mod = mod.reshape(depth, n_rows, 1, 3 * D)

    w_even = [_even_weight(w_in_even[e]) for e in range(n_even)]
    qw_c = H_C * HEAD_DIM
    w_odd = [jnp.concatenate([w_in_odd[o][:, :qw_c] * Q_FOLD, w_in_odd[o][:, qw_c:]], axis=1).astype(BF16)
             for o in range(n_odd)]
    w_o = [w_out[l].astype(BF16) for l in range(depth)]
    hw = H_B * DK_B
    wg_full, bg_full = [], []
    for e in range(n_even):
        wg = jnp.zeros((128, 2 * hw), F32)
        wg = wg.at[0:GLA_RANK, 0:hw].set(gla_wg[e, 0])
        wg = wg.at[GLA_RANK:2 * GLA_RANK, hw:2 * hw].set(gla_wg[e, 1])
        wg_full.append(wg.astype(BF16))
        bg_full.append(gla_bg[e].reshape(1, 2 * hw))
    pad_l = 63 - (NA_KW - 1)
    u_tabs = [jnp.pad(na_bias[o], ((0, 0), (0, 0), (pad_l, 128 - pad_l - (2 * NA_KW - 1))), mode="edge")
              for o in range(n_odd)]
    ws_bf = [gmlp_ws[o].astype(BF16) for o in range(n_odd)]
    b_t = [gmlp_b[o].T for o in range(n_odd)]
    cos_tab, sin_tab = _rope_tables(T)
    ck_a = cache_attn_k.reshape(Bd, n_even, P, KV_A * HEAD_DIM).astype(BF16)
    cv_a = cache_attn_v.reshape(Bd, n_even, P, KV_A * HEAD_DIM).astype(BF16)
    ck_c = cache_na_k.reshape(Bd, n_odd, P, H_C * HEAD_DIM).astype(BF16)
    cv_c = cache_na_v.reshape(Bd, n_odd, P, H_C * HEAD_DIM).astype(BF16)

    def run_group(x, is_ctx):
        batch, seq = (B, L) if is_ctx else (Bd, T)
        m = batch * seq
        grp0 = 0 if is_ctx else 1
        rows_per_grp = m if is_ctx else seq
        tm_mm = min(1024, rows_per_grp)
        tm_op = min(512, rows_per_grp)
        proj_dtype = F32 if is_ctx else BF16
        sides = dict(ka=[], va=[], st=[], kc=[], vc=[])
        h = _prenorm(x, norm_g[0], mod[0], grp0, rows_per_grp, tm_op)
        out = None
        for l in range(depth):
            if l % 2 == 0:
                e = l // 2
                proj = _in_proj(h, w_even[e], proj_dtype, tm_mm, 1152)
                if is_ctx:
                    ya = _ctx_attention(proj, attn_sink[e], batch, seq, EV_QA, EV_KA, EV_VA, EV_GPA,
                                        H_A, KV_A)
                    o_f, o_b, s_fin = _gla(proj, wg_full[e], bg_full[e], None, e, batch, seq)
                    sides["ka"].append(proj[:, EV_KA:EV_KA + KV_A * HEAD_DIM])
                    sides["va"].append(proj[:, EV_VA:EV_VA + KV_A * HEAD_DIM])
                    sides["st"].append(s_fin)
                else:
                    ya = _window_attention(proj, attn_sink[e], ck_a, cv_a, e, cos_tab, sin_tab, batch, seq)
                    o_f, o_b, _ = _gla(proj, wg_full[e], bg_full[e], state_gla, e, batch, seq)
                yb = _gla_combine(o_f, o_b, gla_norm_g[e], proj, tm_op)
            else:
                o = l // 2
                proj = _in_proj(h, w_odd[o], proj_dtype, tm_mm, 512)
                if is_ctx:
                    ya = _ctx_attention(proj, None, batch, seq, OD_QC, OD_KC, OD_VC, OD_GPC, H_C, H_C)
                    sides["kc"].append(proj[:, OD_KC:OD_KC + H_C * HEAD_DIM])
                    sides["vc"].append(proj[:, OD_VC:OD_VC + H_C * HEAD_DIM])
                else:
                    bias_exp = _na_bias_expand(u_tabs[o])
                    ya = _neighbourhood_attention(proj, bias_exp, ck_c, cv_c, o, batch, seq)
                yb = _gmlp(proj, gmlp_norm_g[o], ws_bf[o], b_t[o], tm_op)
            final = l == depth - 1
            g_next = final_norm_g if final else norm_g[l + 1]
            mod_next = mod[l] if final else mod[l + 1]
            res = _out_proj(ya, yb, w_o[l], x, mod[l], g_next, mod_next, grp0, rows_per_grp, tm_op, final)
            if final:
                out = res[0]
            else:
                x, h = res
        return out, sides

    y_prompt, sides = run_group(x_prompt.reshape(Mc, D), True)
    y_sample, _ = run_group(x_sample.reshape(Ml, D), False)

    new_attn_k = jnp.stack([a.reshape(B, L, KV_A, HEAD_DIM) for a in sides["ka"]], axis=1)
    new_attn_v = jnp.stack([a.reshape(B, L, KV_A, HEAD_DIM) for a in sides["va"]], axis=1)
    new_gla_state = jnp.stack(sides["st"], axis=1)
    new_na_k = jnp.stack([a.reshape(B, L, H_C, HEAD_DIM) for a in sides["kc"]], axis=1)
    new_na_v = jnp.stack([a.reshape(B, L, H_C, HEAD_DIM) for a in sides["vc"]], axis=1)
    return (y_prompt.reshape(B, L, D), y_sample.reshape(Bd, T, D),
            new_attn_k, new_attn_v, new_gla_state, new_na_k, new_na_v)
```

```python
import functools

import numpy as np
import jax
import jax.numpy as jnp
from jax import lax
from jax.experimental import pallas as pl
from jax.experimental.pallas import tpu as pltpu

F32 = jnp.float32
BF16 = jnp.bfloat16

D_MODEL = 2048
HEAD_DIM = 128
GRID_W = 64
HALF_W = D_MODEL // 2
H_A = 8
KV_A = 2
WINDOW = 128
BAND_BLK = 128
H_B = 4
DK_B = 128
DV_B = 256
GLA_RANK = 16
GLA_NORMALIZER = 16.0
GLA_CHUNK = 64
GLA_GROUP = 4
H_C = 8
NA_KH = 8
NA_KW = 16
NA_QR = 4
NA_WR = 12
G_D = 4
D_CHUNK = 128
ROPE_BASE = 10000.0
NEG = -1e30
LOG2E = 1.4426950408889634
Q_FOLD = HEAD_DIM ** -0.5 * LOG2E

EV_QA, EV_KA, EV_VA, EV_QB, EV_KB, EV_VB = 0, 1024, 1280, 1536, 2048, 2560
EV_GP = 3584
EV_GPA, EV_GPB = EV_GP, EV_GP + 1024
EV_RB_SRC = 3584
N_EVEN_COLS = 5632
OUT_SUB = 256
WIN_QB = 2
SM_VREGS = 32
OD_QC, OD_KC, OD_VC, OD_U, OD_V, OD_GPC, OD_GPD = 0, 1024, 2048, 3072, 4096, 5120, 6144
N_ODD_COLS = 7168

VMEM_LIMIT = 52 * 1024 * 1024


def _cparams(*sem):
    return pltpu.CompilerParams(dimension_semantics=sem, vmem_limit_bytes=VMEM_LIMIT)


def _silu(x):
    return x / (1.0 + jnp.exp(-x))


def _dot(a, b):
    return jnp.dot(a, b, preferred_element_type=F32)


def _dot_nt(a, b):
    return lax.dot_general(a, b, (((1,), (1,)), ((), ())), preferred_element_type=F32)


def _rms(x, g, eps=1e-6):
    return x * lax.rsqrt(jnp.mean(x * x, axis=-1, keepdims=True) + eps) * g


def _sm_rows(n_cols):
    return max(16, min(128, (SM_VREGS * 1024 // n_cols) // 16 * 16))


def _softmax_rows(parts, add_fns, extra_fn, chunked):
    m_rows = parts[0].shape[0]
    nr = _sm_rows(sum(s.shape[1] for s in parts)) if chunked else m_rows
    dens, ps = [], []
    for r0 in range(0, m_rows, nr):
        rows = slice(r0, r0 + nr)
        ss = []
        for s, fn in zip(parts, add_fns):
            x = s[rows, :]
            if fn is not None:
                x = x + fn(r0, nr)
            ss.append(x)
        m = functools.reduce(jnp.maximum, [jnp.max(x, axis=-1, keepdims=True) for x in ss])
        if extra_fn is not None:
            m = jnp.maximum(m, extra_fn(r0))
        den = None
        es = []
        for x in ss:
            e = jnp.exp2(x - m)
            part_sum = jnp.sum(e, axis=-1, keepdims=True)
            den = part_sum if den is None else den + part_sum
            es.append(e.astype(BF16))
        if extra_fn is not None:
            den = den + jnp.exp2(extra_fn(r0) - m)
        dens.append(den)
        ps.append(es[0] if len(es) == 1 else jnp.concatenate(es, axis=1))
    return jnp.concatenate(ps, axis=0), jnp.concatenate(dens, axis=0)


def _mod_kernel(cond_ref, w_ref, b_ref, o_ref):
    a = _silu(cond_ref[...]).astype(BF16)
    o_ref[0] = _dot(a, w_ref[0].astype(BF16)) + b_ref[0]


def _modulation(cond, ada_w, ada_b):
    depth, d, n = ada_w.shape
    rows = cond.shape[0]
    tn = 1024
    return pl.pallas_call(
        _mod_kernel,
        grid=(depth, n // tn),
        in_specs=[
            pl.BlockSpec((rows, d), lambda l, j: (0, 0)),
            pl.BlockSpec((1, d, tn), lambda l, j: (l, 0, j)),
            pl.BlockSpec((1, 1, tn), lambda l, j: (l, 0, j)),
        ],
        out_specs=pl.BlockSpec((1, rows, tn), lambda l, j: (l, 0, j)),
        out_shape=jax.ShapeDtypeStruct((depth, rows, n), F32),
        compiler_params=_cparams("parallel", "parallel"),
        name="modulation",
    )(cond, ada_w, ada_b.reshape(depth, 1, n))


def _prenorm_kernel(x_ref, g_ref, mod_ref, h_ref):
    d = x_ref.shape[-1]
    y = _rms(x_ref[...], g_ref[...])
    shift = mod_ref[0, :, 0:d]
    scale = mod_ref[0, :, d:2 * d]
    h_ref[...] = (y * (1.0 + scale) + shift).astype(BF16)


def _prenorm(x, g, mod, grp0, rows_per_grp, tm):
    m, d = x.shape
    tpg = rows_per_grp // tm
    return pl.pallas_call(
        _prenorm_kernel,
        grid=(m // tm,),
        in_specs=[
            pl.BlockSpec((tm, d), lambda i: (i, 0)),
            pl.BlockSpec((1, d), lambda i: (0, 0)),
            pl.BlockSpec((1, 1, 3 * d), lambda i: (grp0 + i // tpg, 0, 0)),
        ],
        out_specs=pl.BlockSpec((tm, d), lambda i: (i, 0)),
        out_shape=jax.ShapeDtypeStruct((m, d), BF16),
        compiler_params=_cparams("parallel"),
        name="prenorm",
    )(x, g.reshape(1, d), mod)


def _cast_kernel(w_ref, o_ref, *, q_tiles):
    scale = jnp.where(pl.program_id(0) < q_tiles, Q_FOLD, 1.0)
    o_ref[...] = (w_ref[...] * scale).astype(BF16)


def _cast_cols(w3, layer, first_tile, n_tiles, tw, q_tiles=0):
    k = w3.shape[1]
    return pl.pallas_call(
        functools.partial(_cast_kernel, q_tiles=q_tiles),
        grid=(n_tiles,),
        in_specs=[pl.BlockSpec((None, k, tw), lambda j: (layer, 0, first_tile + j))],
        out_specs=pl.BlockSpec((k, tw), lambda j: (0, j)),
        out_shape=jax.ShapeDtypeStruct((k, n_tiles * tw), BF16),
        compiler_params=_cparams("parallel"),
        name="weight_cast",
    )(w3)


def _cast_even_kernel(a_ref, b_ref, o_ref, *, q_tiles, plain_tiles):
    j = pl.program_id(0)
    tw = a_ref.shape[-1]

    @pl.when(j < plain_tiles)
    def _():
        scale = jnp.where(j < q_tiles, Q_FOLD, 1.0)
        o_ref[...] = (a_ref[...] * scale).astype(BF16)

    @pl.when(j >= plain_tiles)
    def _():
        x = jnp.concatenate([a_ref[...], b_ref[...]], axis=1)
        shifted = pltpu.roll(x, x.shape[1] - 2 * GLA_RANK, 1)
        o_ref[...] = shifted[:, 0:tw].astype(BF16)


def _cast_even(w3, layer):
    k = w3.shape[1]
    tw = 512
    n_tiles = N_EVEN_COLS // tw
    plain_tiles = EV_GP // tw
    sub = tw // 128
    return pl.pallas_call(
        functools.partial(_cast_even_kernel, q_tiles=H_A * HEAD_DIM // tw, plain_tiles=plain_tiles),
        grid=(n_tiles,),
        in_specs=[pl.BlockSpec((None, k, tw), lambda j: (layer, 0, j)),
                  pl.BlockSpec((None, k, 128), lambda j: (layer, 0, sub * (j + 1)))],
        out_specs=pl.BlockSpec((k, tw), lambda j: (0, j)),
        out_shape=jax.ShapeDtypeStruct((k, N_EVEN_COLS), BF16),
        compiler_params=_cparams("parallel"),
        name="weight_cast_even",
    )(w3, w3)


def _matmul_kernel(h_ref, w_ref, o_ref):
    o_ref[...] = _dot(h_ref[...], w_ref[...]).astype(o_ref.dtype)


def _in_proj(h, w, out_dtype, tm, tn):
    m, k = h.shape
    n = w.shape[1]
    return pl.pallas_call(
        _matmul_kernel,
        grid=(m // tm, n // tn),
        in_specs=[
            pl.BlockSpec((tm, k), lambda i, j: (i, 0)),
            pl.BlockSpec((k, tn), lambda i, j: (0, j)),
        ],
        out_specs=pl.BlockSpec((tm, tn), lambda i, j: (i, j)),
        out_shape=jax.ShapeDtypeStruct((m, n), out_dtype),
        compiler_params=_cparams("parallel", "parallel"),
        name="in_proj",
    )(h, w)


def _outproj_kernel(ya_ref, yb_ref, w_ref, x_ref, mod_ref, g_ref, modn_ref, *out_refs, final):
    d = x_ref.shape[-1]
    half = ya_ref.shape[-1]
    tm = x_ref.shape[0]
    gate = mod_ref[0, :, 2 * d:3 * d]
    for r0 in range(0, tm, OUT_SUB):
        rows = slice(r0, r0 + OUT_SUB)
        acc = _dot(ya_ref[rows, :], w_ref[0:half, :]) + _dot(yb_ref[rows, :], w_ref[half:2 * half, :])
        xn = x_ref[rows, :] + gate * acc
        if final:
            out_refs[0][rows, :] = _rms(xn, g_ref[...])
        else:
            out_refs[0][rows, :] = xn
            shift = modn_ref[0, :, 0:d]
            scale = modn_ref[0, :, d:2 * d]
            out_refs[1][rows, :] = (_rms(xn, g_ref[...]) * (1.0 + scale) + shift).astype(BF16)


def _out_proj(ya, yb, w, x, mod, g_next, mod_next, grp0, rows_per_grp, tm, final):
    m, d = x.shape
    half = ya.shape[1]
    tpg = rows_per_grp // tm
    modspec = pl.BlockSpec((1, 1, 3 * d), lambda i: (grp0 + i // tpg, 0, 0))
    rowspec = pl.BlockSpec((tm, d), lambda i: (i, 0))
    halfspec = pl.BlockSpec((tm, half), lambda i: (i, 0))
    if final:
        out_shape = [jax.ShapeDtypeStruct((m, d), F32)]
        out_specs = [rowspec]
    else:
        out_shape = [jax.ShapeDtypeStruct((m, d), F32), jax.ShapeDtypeStruct((m, d), BF16)]
        out_specs = [rowspec, rowspec]
    return pl.pallas_call(
        functools.partial(_outproj_kernel, final=final),
        grid=(m // tm,),
        in_specs=[
            halfspec, halfspec,
            pl.BlockSpec((2 * half, d), lambda i: (0, 0)),
            rowspec, modspec,
            pl.BlockSpec((1, d), lambda i: (0, 0)),
            modspec,
        ],
        out_specs=out_specs,
        out_shape=out_shape,
        compiler_params=_cparams("parallel"),
        name="out_proj",
    )(ya, yb, w, x, mod, g_next.reshape(1, d), mod_next)


def _ctx_attn_kernel(*refs, n_heads, n_kv, has_sink):
    if has_sink:
        sink_ref, q_ref, k_ref, v_ref, gp0_ref, gp1_ref, o_ref = refs
    else:
        q_ref, k_ref, v_ref, gp0_ref, gp1_ref, o_ref = refs
    L = q_ref.shape[0]
    grp = n_heads // n_kv
    outs = [None] * n_heads
    for kv in range(n_kv):
        heads = list(range(kv * grp, (kv + 1) * grp))
        qg = jnp.concatenate(
            [q_ref[:, h * HEAD_DIM:(h + 1) * HEAD_DIM].astype(BF16) for h in heads], axis=0)
        k = k_ref[:, kv * HEAD_DIM:(kv + 1) * HEAD_DIM].astype(BF16)
        v = v_ref[:, kv * HEAD_DIM:(kv + 1) * HEAD_DIM].astype(BF16)
        s = _dot_nt(qg, k)
        sink_fn = None
        if has_sink:
            sk = jnp.concatenate([jnp.full((L, 1), sink_ref[h] * LOG2E, F32) for h in heads], axis=0)
            sink_fn = lambda r0: sk
        p, den = _softmax_rows([s], [None], sink_fn, chunked=False)
        o = _dot(p, v) / den
        for i, h in enumerate(heads):
            outs[h] = o[i * L:(i + 1) * L, :]
    _store_gated(o_ref, outs, gp0_ref, gp1_ref)


def _store_gated(o_ref, head_outs, gp0_ref, gp1_ref):
    hw = gp0_ref.shape[-1]
    per = hw // HEAD_DIM
    for i, gp_ref in enumerate((gp0_ref, gp1_ref)):
        y = jnp.concatenate(head_outs[i * per:(i + 1) * per], axis=1)
        o_ref[:, i * hw:(i + 1) * hw] = (y * _silu(gp_ref[...].astype(F32))).astype(BF16)


def _ctx_attention(proj, sink, batch, L, q_col, k_col, v_col, gp_col, n_heads, n_kv):
    qw = n_heads * HEAD_DIM
    kw = n_kv * HEAD_DIM
    gw = qw // 2
    has_sink = sink is not None
    in_specs = [
        pl.BlockSpec((L, qw), lambda b: (b, q_col // qw)),
        pl.BlockSpec((L, kw), lambda b: (b, k_col // kw)),
        pl.BlockSpec((L, kw), lambda b: (b, v_col // kw)),
        pl.BlockSpec((L, gw), lambda b: (b, gp_col // gw)),
        pl.BlockSpec((L, gw), lambda b: (b, gp_col // gw + 1)),
    ]
    args = [proj, proj, proj, proj, proj]
    if has_sink:
        in_specs = [pl.BlockSpec(memory_space=pltpu.SMEM)] + in_specs
        args = [sink] + args
    return pl.pallas_call(
        functools.partial(_ctx_attn_kernel, n_heads=n_heads, n_kv=n_kv, has_sink=has_sink),
        grid=(batch,),
        in_specs=in_specs,
        out_specs=pl.BlockSpec((L, qw), lambda b: (b, 0)),
        out_shape=jax.ShapeDtypeStruct((batch * L, qw), BF16),
        compiler_params=_cparams("parallel"),
        name="ctx_attention",
    )(*args)


def _rope(x, cos, sin_signed):
    lane = lax.broadcasted_iota(jnp.int32, x.shape, 1)
    first = (lane % 64) < 32
    swapped = jnp.where(first, pltpu.roll(x, 96, 1), pltpu.roll(x, 32, 1))
    return x * cos + swapped * sin_signed


def _win_attn_kernel(sink_ref, q_ref, kp_ref, kc_ref, kn_ref, vp_ref, vc_ref, vn_ref,
                     cq_ref, sq_ref, cp_ref, sp_ref, cn_ref, sn_ref, mask_ref,
                     ck_ref, cv_ref, gp0_ref, gp1_ref, o_ref, *, nsteps):
    p = pl.program_id(1)
    blk = BAND_BLK
    grp = H_A // KV_A
    cq, sq = cq_ref[...], sq_ref[...]
    lane3 = lax.broadcasted_iota(jnp.int32, (1, 3 * blk), 1)
    edges = []
    for t in range(WIN_QB):
        edge = jnp.zeros((1, 3 * blk), F32)
        if t == 0:
            edge = edge + jnp.where(lane3 < blk, jnp.where(p > 0, 0.0, NEG), 0.0)
        if t == WIN_QB - 1:
            edge = edge + jnp.where(lane3 >= 2 * blk, jnp.where(p < nsteps - 1, 0.0, NEG), 0.0)
        edges.append(edge)
    outs = [[None] * WIN_QB for _ in range(H_A)]
    for kv in range(KV_A):
        sl = slice(kv * HEAD_DIM, (kv + 1) * HEAD_DIM)
        k_all = jnp.concatenate([
            _rope(kp_ref[:, sl].astype(F32), cp_ref[...], sp_ref[...]),
            _rope(kc_ref[:, sl].astype(F32), cq, sq),
            _rope(kn_ref[:, sl].astype(F32), cn_ref[...], sn_ref[...]),
        ], axis=0).astype(BF16)
        v_all = jnp.concatenate([vp_ref[:, sl], vc_ref[:, sl], vn_ref[:, sl]], axis=0).astype(BF16)
        heads = list(range(kv * grp, (kv + 1) * grp))
        q_rot = [_rope(q_ref[:, h * HEAD_DIM:(h + 1) * HEAD_DIM].astype(F32), cq, sq).astype(BF16)
                 for h in heads]
        ck = ck_ref[:, sl]
        cv = cv_ref[:, sl]
        for t in range(WIN_QB):
            rows = slice(t * blk, (t + 1) * blk)
            band = slice(t * blk, (t + 3) * blk)
            qg = jnp.concatenate([q[rows, :] for q in q_rot], axis=0)
            s_band = _dot_nt(qg, k_all[band, :])
            s_ctx = _dot_nt(qg, ck)
            p, den = _softmax_rows(
                [s_band, s_ctx],
                [lambda r0, nr, t=t: mask_ref[r0 % blk:r0 % blk + nr, :] + edges[t], None],
                lambda r0: sink_ref[heads[r0 // blk]] * LOG2E, chunked=True)
            o = _dot(p, jnp.concatenate([v_all[band, :], cv], axis=0)) / den
            for i, h in enumerate(heads):
                outs[h][t] = o[i * blk:(i + 1) * blk, :]
    _store_gated(o_ref, [jnp.concatenate(o, axis=0) for o in outs], gp0_ref, gp1_ref)


def _window_attention(proj, sink, cache_k, cache_v, layer, cos_tab, sin_tab, batch, T):
    blk = BAND_BLK
    nb = T // blk
    qb = WIN_QB
    assert nb % qb == 0
    nsteps = nb // qb
    rq = qb * blk
    qw = H_A * HEAD_DIM
    kw = KV_A * HEAD_DIM
    gw = qw // 2
    P = cache_k.shape[2]

    def prev(p):
        return jnp.maximum(qb * p - 1, 0)

    def nxt(p):
        return jnp.minimum(qb * p + qb, nb - 1)

    kcol, vcol = EV_KA // kw, EV_VA // kw

    def edge(col, f):
        return pl.BlockSpec((blk, kw), lambda b, p: (b * nb + f(p), col))

    def mid(col):
        return pl.BlockSpec((rq, kw), lambda b, p: (b * nsteps + p, col))

    tab_edge = lambda f: pl.BlockSpec((blk, HEAD_DIM), lambda b, p: (f(p), 0))
    tab_mid = pl.BlockSpec((rq, HEAD_DIM), lambda b, p: (p, 0))
    in_specs = [
        pl.BlockSpec(memory_space=pltpu.SMEM),
        pl.BlockSpec((rq, qw), lambda b, p: (b * nsteps + p, EV_QA // qw)),
        edge(kcol, prev), mid(kcol), edge(kcol, nxt),
        edge(vcol, prev), mid(vcol), edge(vcol, nxt),
        tab_mid, tab_mid, tab_edge(prev), tab_edge(prev), tab_edge(nxt), tab_edge(nxt),
        pl.BlockSpec((blk, 3 * blk), lambda b, p: (0, 0)),
        pl.BlockSpec((None, None, P, kw), lambda b, p: (b, layer, 0, 0)),
        pl.BlockSpec((None, None, P, kw), lambda b, p: (b, layer, 0, 0)),
        pl.BlockSpec((rq, gw), lambda b, p: (b * nsteps + p, EV_GPA // gw)),
        pl.BlockSpec((rq, gw), lambda b, p: (b * nsteps + p, EV_GPA // gw + 1)),
    ]
    return pl.pallas_call(
        functools.partial(_win_attn_kernel, nsteps=nsteps),
        grid=(batch, nsteps),
        in_specs=in_specs,
        out_specs=pl.BlockSpec((rq, qw), lambda b, p: (b * nsteps + p, 0)),
        out_shape=jax.ShapeDtypeStruct((batch * T, qw), BF16),
        compiler_params=_cparams("parallel", "parallel"),
        name="window_attention",
    )(sink, proj, proj, proj, proj, proj, proj, proj,
      cos_tab, sin_tab, cos_tab, sin_tab, cos_tab, sin_tab, _band_mask(),
      cache_k, cache_v, proj, proj)


def _band_mask():
    iq = np.arange(BAND_BLK)[:, None]
    jk = np.arange(3 * BAND_BLK)[None, :]
    rel = iq - jk + BAND_BLK
    return jnp.asarray(np.where(np.abs(rel) <= WINDOW, 0.0, NEG), F32)


def _log_sigmoid(x):
    return jnp.minimum(x, 0.0) - jnp.log1p(jnp.exp(-jnp.abs(x)))


def _cumsum_rows(tri, g):
    g_hi = g.astype(BF16)
    r1 = g - g_hi.astype(F32)
    g_mid = r1.astype(BF16)
    g_lo = (r1 - g_mid.astype(F32)).astype(BF16)
    return _dot(tri, g_hi) + _dot(tri, g_mid) + _dot(tri, g_lo)


def _gla_kernel(*refs, nsteps, has_s0):
    fwd_refs, bwd_refs = refs[0:5], refs[5:10]
    wg_ref, bg_ref = refs[10:12]
    if has_s0:
        s0_ref, of_ref, ob_ref, sfin_ref, s_scr = refs[12:]
    else:
        of_ref, ob_ref, sfin_ref, s_scr = refs[12:]
    qf = fwd_refs[0]
    step = pl.program_id(1)
    C = GLA_CHUNK
    G = qf.shape[0] // C
    R = G * C

    @pl.when(step == 0)
    def _():
        if has_s0:
            s_scr[...] = s0_ref[...]
        else:
            s_scr[...] = jnp.zeros(s_scr.shape, F32)

    ri = lax.broadcasted_iota(jnp.int32, (R, R), 0)
    ci = lax.broadcasted_iota(jnp.int32, (R, R), 1)
    same_chunk = (ri // C) == (ci // C)
    hw = H_B * DK_B
    hpv = H_B // 2
    for d, ((q_ref, k_ref, v0_ref, v1_ref, r_ref), o_ref) in enumerate(
            ((fwd_refs, of_ref), (bwd_refs, ob_ref))):
        keep = same_chunk & ((ci <= ri) if d == 0 else (ci >= ri))
        tri = jnp.where(keep, 1.0, 0.0).astype(BF16)
        logit = _dot(r_ref[...].astype(BF16), wg_ref[:, d * hw:(d + 1) * hw]) + bg_ref[:, d * hw:(d + 1) * hw]
        g = _log_sigmoid(logit) / GLA_NORMALIZER
        bc = _cumsum_rows(tri, g)
        edge = C - 1 if d == 0 else 0
        totals = [bc[j * C + edge:j * C + edge + 1, :] for j in range(G)]
        tot_full = jnp.concatenate([jnp.broadcast_to(t, (C, hw)) for t in totals], axis=0)
        e_pos = jnp.exp(bc)
        e_neg = jnp.exp(-bc)
        e_rem = jnp.exp(tot_full - bc)
        e_tot = [jnp.exp(t) for t in totals]
        order = range(G) if d == 0 else range(G - 1, -1, -1)
        for h in range(H_B):
            sl = slice(h * DK_B, (h + 1) * DK_B)
            vl = slice(h * DV_B, (h + 1) * DV_B)
            qh = q_ref[:, sl].astype(F32) * (DK_B ** -0.5)
            kh = k_ref[:, sl].astype(F32)
            v_ref = v0_ref if h < hpv else v1_ref
            vh = v_ref[:, (h % hpv) * DV_B:(h % hpv + 1) * DV_B].astype(BF16)
            qi = (qh * e_pos[:, sl]).astype(BF16)
            ki = (kh * e_neg[:, sl]).astype(BF16)
            ks = kh * e_rem[:, sl]
            att = jnp.where(keep, _dot_nt(qi, ki), 0.0).astype(BF16)
            o_intra = _dot(att, vh)
            S = s_scr[d, h]
            for j in order:
                rows = slice(j * C, (j + 1) * C)
                o_ref[rows, vl] = (o_intra[rows, :] + _dot(qi[rows, :], S.astype(BF16))).astype(o_ref.dtype)
                stacked = jnp.concatenate(
                    [ks[rows, :], jnp.broadcast_to(e_tot[j][:, sl], (C, DK_B))], axis=0)
                tr = stacked.T
                S = tr[:, C:C + 1] * S + _dot(tr[:, 0:C].astype(BF16), vh[rows, :])
            s_scr[d, h] = S

    @pl.when(step == nsteps - 1)
    def _():
        sfin_ref[...] = s_scr[...]


def _gla(proj, r_proj, wg_full, bg_full, s0, layer, batch, T):
    C = GLA_GROUP * GLA_CHUNK
    nc = T // C
    m = batch * T
    qw, vw, rw = H_B * DK_B, H_B * DV_B, 128
    vh = vw // 2

    def fwd(b, c):
        return b * nc + c

    def bwd(b, c):
        return b * nc + (nc - 1 - c)

    def specs(f):
        return [
            pl.BlockSpec((C, qw), lambda b, c: (f(b, c), EV_QB // qw)),
            pl.BlockSpec((C, qw), lambda b, c: (f(b, c), EV_KB // qw)),
            pl.BlockSpec((C, vh), lambda b, c: (f(b, c), EV_VB // vh)),
            pl.BlockSpec((C, vh), lambda b, c: (f(b, c), EV_VB // vh + 1)),
            pl.BlockSpec((C, rw), lambda b, c: (f(b, c), 0)),
        ]

    in_specs = specs(fwd) + specs(bwd) + [
        pl.BlockSpec((rw, 2 * qw), lambda b, c: (0, 0)),
        pl.BlockSpec((1, 2 * qw), lambda b, c: (0, 0)),
    ]
    args = [proj, proj, proj, proj, r_proj] * 2 + [wg_full, bg_full]
    has_s0 = s0 is not None
    if has_s0:
        in_specs.append(pl.BlockSpec((None, None, 2, H_B, DK_B, DV_B),
                                     lambda b, c: (b, layer, 0, 0, 0, 0)))
        args.append(s0)
    return pl.pallas_call(
        functools.partial(_gla_kernel, nsteps=nc, has_s0=has_s0),
        grid=(batch, nc),
        in_specs=in_specs,
        out_specs=[
            pl.BlockSpec((C, vw), lambda b, c: (fwd(b, c), 0)),
            pl.BlockSpec((C, vw), lambda b, c: (bwd(b, c), 0)),
            pl.BlockSpec((None, 2, H_B, DK_B, DV_B), lambda b, c: (b, 0, 0, 0, 0)),
        ],
        out_shape=[
            jax.ShapeDtypeStruct((m, vw), BF16),
            jax.ShapeDtypeStruct((m, vw), BF16),
            jax.ShapeDtypeStruct((batch, 2, H_B, DK_B, DV_B), F32),
        ],
        scratch_shapes=[pltpu.VMEM((2, H_B, DK_B, DV_B), F32)],
        compiler_params=_cparams("parallel", "arbitrary"),
        name="gla_scan",
    )(*args)


def _gla_combine_kernel(of_ref, ob_ref, g_ref, gp0_ref, gp1_ref, o_ref):
    g = g_ref[...]
    hpv = H_B // 2
    for h in range(H_B):
        vl = slice(h * DV_B, (h + 1) * DV_B)
        gp_ref = gp0_ref if h < hpv else gp1_ref
        gp = gp_ref[:, (h % hpv) * DV_B:(h % hpv + 1) * DV_B].astype(F32)
        o = _rms(of_ref[:, vl].astype(F32) + ob_ref[:, vl].astype(F32), g)
        o_ref[:, vl] = (o * _silu(gp)).astype(BF16)


def _gla_combine(o_f, o_b, gnorm, proj, tm):
    m, vw = o_f.shape
    gw = vw // 2
    spec = pl.BlockSpec((tm, vw), lambda i: (i, 0))
    return pl.pallas_call(
        _gla_combine_kernel,
        grid=(m // tm,),
        in_specs=[spec, spec,
                  pl.BlockSpec((1, DV_B), lambda i: (0, 0)),
                  pl.BlockSpec((tm, gw), lambda i: (i, EV_GPB // gw)),
                  pl.BlockSpec((tm, gw), lambda i: (i, EV_GPB // gw + 1))],
        out_specs=spec,
        out_shape=jax.ShapeDtypeStruct((m, vw), BF16),
        compiler_params=_cparams("parallel"),
        name="gla_combine",
    )(o_f, o_b, gnorm.reshape(1, DV_B), proj, proj)


def _na_bias_kernel(u_ref, o_ref):
    place = pl.program_id(0)
    W = GRID_W
    lane = lax.broadcasted_iota(jnp.int32, (W, 128), 1)
    qcol = lax.broadcasted_iota(jnp.int32, (W, 128), 0)
    kcol = lane % W
    cs = jnp.clip(qcol - NA_KW // 2, 0, W - NA_KW)
    col_ok = (kcol >= cs) & (kcol < cs + NA_KW)
    for t in range(NA_QR):
        first = jnp.where(place == 0, 0, jnp.where(place == 1, t, NA_WR - NA_KH))

        def piece(i, base_shift):
            dr = i - t + (NA_KH - 1) - (NA_WR - NA_KH) * place
            in_rows = (i >= first) & (i < first + NA_KH)
            u = jnp.broadcast_to(u_ref[0, pl.ds(jnp.clip(dr, 0, 2 * NA_KH - 2), 1), :], (W, 128))
            rolled = pltpu.roll(u, base_shift, 1, stride=1, stride_axis=0)
            return rolled + jnp.where(in_rows, 0.0, NEG)

        tiles = []
        for i in range(0, NA_WR, 2):
            tile = jnp.where(lane < W, piece(i, 65), piece(i + 1, 1))
            tiles.append(jnp.where(col_ok, tile, NEG) * LOG2E)
        o_ref[0, 0, t * W:(t + 1) * W, :] = jnp.concatenate(tiles, axis=1)


def _na_bias_expand(u_tab):
    W = GRID_W
    n_tab = u_tab.shape[1]
    return pl.pallas_call(
        _na_bias_kernel,
        grid=(3, H_C),
        in_specs=[pl.BlockSpec((1, n_tab, 128), lambda p, h: (h, 0, 0))],
        out_specs=pl.BlockSpec((1, 1, NA_QR * W, NA_WR * W), lambda p, h: (p, h, 0, 0)),
        out_shape=jax.ShapeDtypeStruct((3, H_C, NA_QR * W, NA_WR * W), F32),
        compiler_params=_cparams("parallel", "parallel"),
        name="na_bias_expand",
    )(u_tab)


def _na_kernel(*refs):
    nblk = NA_WR // NA_QR
    q_ref = refs[0]
    k_refs = refs[1:1 + nblk]
    v_refs = refs[1 + nblk:1 + 2 * nblk]
    bias_ref, ck_ref, cv_ref, gp_ref, o_ref = refs[1 + 2 * nblk:]
    outs = []
    for h in range(H_C):
        sl = slice(h * HEAD_DIM, (h + 1) * HEAD_DIM)
        q = q_ref[:, sl]
        kw = jnp.concatenate([r[:, sl] for r in k_refs], axis=0)
        v_cat = jnp.concatenate([r[:, sl] for r in v_refs] + [cv_ref[:, sl]], axis=0)
        s_nb = _dot_nt(q, kw)
        s_ctx = _dot_nt(q, ck_ref[:, sl])
        p, den = _softmax_rows([s_nb, s_ctx],
                               [lambda r0, nr, h=h: bias_ref[0, h, r0:r0 + nr, :], None], None,
                               chunked=False)
        outs.append(_dot(p, v_cat) / den)
    y = jnp.concatenate(outs, axis=1)
    o_ref[...] = (y * _silu(gp_ref[...].astype(F32))).astype(BF16)


def _neighbourhood_attention(proj, bias_exp, cache_k, cache_v, layer, batch, T):
    W = GRID_W
    rows = T // W
    assert rows % NA_QR == 0 and rows >= NA_WR and NA_WR % NA_QR == 0
    steps = rows // NA_QR
    nblk = NA_WR // NA_QR
    rq = NA_QR * W
    qw = H_C * HEAD_DIM
    P = cache_k.shape[2]

    def win0(p):
        return jnp.clip(p - 1, 0, steps - nblk)

    def place(p):
        return jnp.where(p == 0, 0, jnp.where(p == steps - 1, 2, 1))

    def win_spec(col, i):
        return pl.BlockSpec((rq, qw), lambda b, p: (b * steps + win0(p) + i, col // qw))

    in_specs = ([pl.BlockSpec((rq, qw), lambda b, p: (b * steps + p, OD_QC // qw))]
                + [win_spec(OD_KC, i) for i in range(nblk)]
                + [win_spec(OD_VC, i) for i in range(nblk)]
                + [pl.BlockSpec((1, H_C, rq, NA_WR * W), lambda b, p: (place(p), 0, 0, 0)),
                   pl.BlockSpec((None, None, P, qw), lambda b, p: (b, layer, 0, 0)),
                   pl.BlockSpec((None, None, P, qw), lambda b, p: (b, layer, 0, 0)),
                   pl.BlockSpec((rq, qw), lambda b, p: (b * steps + p, OD_GPC // qw))])
    return pl.pallas_call(
        _na_kernel,
        grid=(batch, steps),
        in_specs=in_specs,
        out_specs=pl.BlockSpec((rq, qw), lambda b, p: (b * steps + p, 0)),
        out_shape=jax.ShapeDtypeStruct((batch * T, qw), BF16),
        compiler_params=_cparams("parallel", "parallel"),
        name="neighbourhood_attention",
    )(proj, *([proj] * (2 * nblk)), bias_exp, cache_k, cache_v, proj)


def _gmlp_kernel(u_ref, v_ref, gp_ref, g_ref, ws_ref, bt_ref, o_ref, *, chunks):
    gw = u_ref.shape[-1] // G_D
    for ch in range(chunks):
        rs_ = slice(ch * D_CHUNK, (ch + 1) * D_CHUNK)
        v = v_ref[rs_, :].astype(F32)
        mu = jnp.mean(v, axis=-1, keepdims=True)
        var = jnp.mean(jnp.square(v - mu), axis=-1, keepdims=True)
        vn = ((v - mu) * lax.rsqrt(var + 1e-5) * g_ref[...]).astype(BF16)
        for g in range(G_D):
            cl = slice(g * gw, (g + 1) * gw)
            sp = _dot(ws_ref[g], vn[:, cl]) + bt_ref[:, g:g + 1]
            y = u_ref[rs_, cl].astype(F32) * sp
            o_ref[rs_, cl] = (y * _silu(gp_ref[rs_, cl].astype(F32))).astype(BF16)


def _gmlp(proj, gnorm, ws_bf, b_t, tm):
    m = proj.shape[0]
    uw = HALF_W
    return pl.pallas_call(
        functools.partial(_gmlp_kernel, chunks=tm // D_CHUNK),
        grid=(m // tm,),
        in_specs=[
            pl.BlockSpec((tm, uw), lambda i: (i, OD_U // uw)),
            pl.BlockSpec((tm, uw), lambda i: (i, OD_V // uw)),
            pl.BlockSpec((tm, uw), lambda i: (i, OD_GPD // uw)),
            pl.BlockSpec((1, uw), lambda i: (0, 0)),
            pl.BlockSpec(ws_bf.shape, lambda i: (0, 0, 0)),
            pl.BlockSpec(b_t.shape, lambda i: (0, 0)),
        ],
        out_specs=pl.BlockSpec((tm, uw), lambda i: (i, 0)),
        out_shape=jax.ShapeDtypeStruct((m, uw), BF16),
        compiler_params=_cparams("parallel"),
        name="gmlp",
    )(proj, proj, proj, gnorm.reshape(1, uw), ws_bf, b_t)


def _rope_tables(T):
    t = np.arange(T)
    nf = HEAD_DIM // 4
    inv = ROPE_BASE ** (-jnp.arange(nf, dtype=F32) / nf)
    pos = jnp.stack([t // GRID_W, t % GRID_W], axis=-1).astype(F32)
    ang = pos[:, :, None] * inv
    cos, sin = jnp.cos(ang), jnp.sin(ang)
    cos_tab = jnp.concatenate([cos, cos], axis=-1).reshape(T, HEAD_DIM)
    sin_tab = jnp.concatenate([-sin, sin], axis=-1).reshape(T, HEAD_DIM)
    return cos_tab, sin_tab


def kernel(x_prompt, x_sample, cache_attn_k, cache_attn_v, state_gla, cache_na_k, cache_na_v,
           c, c_ctx, ada_w, ada_b, norm_g, w_in_even, w_in_odd, w_out, attn_sink,
           gla_wg, gla_bg, gla_norm_g, na_bias, gmlp_norm_g, gmlp_ws, gmlp_b, final_norm_g):
    B, L, D = x_prompt.shape
    Bd, T, _ = x_sample.shape
    depth = ada_w.shape[0]
    n_even = w_in_even.shape[0]
    n_odd = w_in_odd.shape[0]
    P = cache_attn_k.shape[2]
    Mc, Ml = B * L, Bd * T

    n_rows = 8 * ((1 + Bd + 7) // 8)
    cond = jnp.concatenate([c_ctx[None, :], c, jnp.zeros((n_rows - 1 - Bd, D), F32)], axis=0)
    mod = _modulation(cond, ada_w, ada_b)
    mod = mod.reshape(depth, n_rows, 1, 3 * D)

    tw = 512
    w_even = [_cast_even(w_in_even, e) for e in range(n_even)]
    w_rank = [_cast_cols(w_in_even, e, EV_RB_SRC // 128, 1, 128) for e in range(n_even)]
    w_odd = [_cast_cols(w_in_odd, o, 0, N_ODD_COLS // tw, tw, q_tiles=H_C * HEAD_DIM // tw)
             for o in range(n_odd)]
    w_o = [_cast_cols(w_out, l, 0, D // tw, tw) for l in range(depth)]
    hw = H_B * DK_B
    wg_full, bg_full = [], []
    for e in range(n_even):
        wg = jnp.zeros((128, 2 * hw), F32)
        wg = wg.at[0:GLA_RANK, 0:hw].set(gla_wg[e, 0])
        wg = wg.at[GLA_RANK:2 * GLA_RANK, hw:2 * hw].set(gla_wg[e, 1])
        wg_full.append(wg.astype(BF16))
        bg_full.append(gla_bg[e].reshape(1, 2 * hw))
    pad_l = 63 - (NA_KW - 1)
    u_tabs = [jnp.pad(na_bias[o], ((0, 0), (0, 0), (pad_l, 128 - pad_l - (2 * NA_KW - 1))), mode="edge")
              for o in range(n_odd)]
    ws_bf = [gmlp_ws[o].astype(BF16) for o in range(n_odd)]
    b_t = [gmlp_b[o].T for o in range(n_odd)]
    cos_tab, sin_tab = _rope_tables(T)
    ck_a = cache_attn_k.reshape(Bd, n_even, P, KV_A * HEAD_DIM).astype(BF16)
    cv_a = cache_attn_v.reshape(Bd, n_even, P, KV_A * HEAD_DIM).astype(BF16)
    ck_c = cache_na_k.reshape(Bd, n_odd, P, H_C * HEAD_DIM).astype(BF16)
    cv_c = cache_na_v.reshape(Bd, n_odd, P, H_C * HEAD_DIM).astype(BF16)

    def run_group(x, is_ctx):
        batch, seq = (B, L) if is_ctx else (Bd, T)
        m = batch * seq
        grp0 = 0 if is_ctx else 1
        rows_per_grp = m if is_ctx else seq
        tm_mm = min(2048, rows_per_grp)
        tm_op = min(512, rows_per_grp)
        proj_dtype = F32 if is_ctx else BF16
        sides = dict(ka=[], va=[], st=[], kc=[], vc=[])
        h = _prenorm(x, norm_g[0], mod[0], grp0, rows_per_grp, tm_op)
        out = None
        for l in range(depth):
            if l % 2 == 0:
                e = l // 2
                proj = _in_proj(h, w_even[e], proj_dtype, tm_mm, 512)
                r_proj = _in_proj(h, w_rank[e], BF16, tm_mm, 128)
                if is_ctx:
                    ya = _ctx_attention(proj, attn_sink[e], batch, seq, EV_QA, EV_KA, EV_VA, EV_GPA,
                                        H_A, KV_A)
                    o_f, o_b, s_fin = _gla(proj, r_proj, wg_full[e], bg_full[e], None, e, batch, seq)
                    sides["ka"].append(proj[:, EV_KA:EV_KA + KV_A * HEAD_DIM])
                    sides["va"].append(proj[:, EV_VA:EV_VA + KV_A * HEAD_DIM])
                    sides["st"].append(s_fin)
                else:
                    ya = _window_attention(proj, attn_sink[e], ck_a, cv_a, e, cos_tab, sin_tab, batch, seq)
                    o_f, o_b, _ = _gla(proj, r_proj, wg_full[e], bg_full[e], state_gla, e, batch, seq)
                yb = _gla_combine(o_f, o_b, gla_norm_g[e], proj, tm_op)
            else:
                o = l // 2
                proj = _in_proj(h, w_odd[o], proj_dtype, tm_mm, 1024)
                if is_ctx:
                    ya = _ctx_attention(proj, None, batch, seq, OD_QC, OD_KC, OD_VC, OD_GPC, H_C, H_C)
                    sides["kc"].append(proj[:, OD_KC:OD_KC + H_C * HEAD_DIM])
                    sides["vc"].append(proj[:, OD_VC:OD_VC + H_C * HEAD_DIM])
                else:
                    bias_exp = _na_bias_expand(u_tabs[o])
                    ya = _neighbourhood_attention(proj, bias_exp, ck_c, cv_c, o, batch, seq)
                yb = _gmlp(proj, gmlp_norm_g[o], ws_bf[o], b_t[o], tm_op)
            final = l == depth - 1
            g_next = final_norm_g if final else norm_g[l + 1]
            mod_next = mod[l] if final else mod[l + 1]
            res = _out_proj(ya, yb, w_o[l], x, mod[l], g_next, mod_next, grp0, rows_per_grp, tm_op, final)
            if final:
                out = res[0]
            else:
                x, h = res
        return out, sides

    y_prompt, sides = run_group(x_prompt.reshape(Mc, D), True)
    y_sample, _ = run_group(x_sample.reshape(Ml, D), False)

    new_attn_k = jnp.stack([a.reshape(B, L, KV_A, HEAD_DIM) for a in sides["ka"]], axis=1)
    new_attn_v = jnp.stack([a.reshape(B, L, KV_A, HEAD_DIM) for a in sides["va"]], axis=1)
    new_gla_state = jnp.stack(sides["st"], axis=1)
    new_na_k = jnp.stack([a.reshape(B, L, H_C, HEAD_DIM) for a in sides["kc"]], axis=1)
    new_na_v = jnp.stack([a.reshape(B, L, H_C, HEAD_DIM) for a in sides["vc"]], axis=1)
    return (y_prompt.reshape(B, L, D), y_sample.reshape(Bd, T, D),
            new_attn_k, new_attn_v, new_gla_state, new_na_k, new_na_v)
```

```python
import functools

import numpy as np
import jax
import jax.numpy as jnp
from jax import lax
from jax.experimental import pallas as pl
from jax.experimental.pallas import tpu as pltpu

F32 = jnp.float32
BF16 = jnp.bfloat16

D_MODEL = 2048
HEAD_DIM = 128
GRID_W = 64
HALF_W = D_MODEL // 2
H_A = 8
KV_A = 2
WINDOW = 128
BAND_BLK = 128
H_B = 4
DK_B = 128
DV_B = 256
GLA_RANK = 16
GLA_NORMALIZER = 16.0
GLA_CHUNK = 64
GLA_GROUP = 4
H_C = 8
NA_KH = 8
NA_KW = 16
NA_QR = 4
NA_WR = 12
G_D = 4
D_CHUNK = 128
ROPE_BASE = 10000.0
NEG = -1e30
LOG2E = 1.4426950408889634
Q_FOLD = HEAD_DIM ** -0.5 * LOG2E

EV_QA, EV_KA, EV_VA, EV_QB, EV_KB, EV_VB = 0, 1024, 1280, 1536, 2048, 2560
EV_GP = 3584
EV_GPA, EV_GPB = EV_GP, EV_GP + 1024
EV_RB_SRC = 3584
N_EVEN_COLS = 5632
OUT_SUB = 256
WIN_QB = 2
SM_VREGS = 32
OD_QC, OD_KC, OD_VC, OD_U, OD_V, OD_GPC, OD_GPD = 0, 1024, 2048, 3072, 4096, 5120, 6144
N_ODD_COLS = 7168

VMEM_LIMIT = 52 * 1024 * 1024


def _cparams(*sem):
    return pltpu.CompilerParams(dimension_semantics=sem, vmem_limit_bytes=VMEM_LIMIT)


def _silu(x):
    return x / (1.0 + jnp.exp(-x))


def _dot(a, b):
    return jnp.dot(a, b, preferred_element_type=F32)


def _dot_nt(a, b):
    return lax.dot_general(a, b, (((1,), (1,)), ((), ())), preferred_element_type=F32)


def _rms(x, g, eps=1e-6):
    return x * lax.rsqrt(jnp.mean(x * x, axis=-1, keepdims=True) + eps) * g


def _sm_rows(n_cols):
    return max(16, min(128, (SM_VREGS * 1024 // n_cols) // 16 * 16))


def _softmax_rows(parts, add_fns, extra_fn, chunked):
    m_rows = parts[0].shape[0]
    nr = _sm_rows(sum(s.shape[1] for s in parts)) if chunked else m_rows
    dens, ps = [], []
    for r0 in range(0, m_rows, nr):
        rows = slice(r0, r0 + nr)
        ss = []
        for s, fn in zip(parts, add_fns):
            x = s[rows, :]
            if fn is not None:
                x = x + fn(r0, nr)
            ss.append(x)
        m = functools.reduce(jnp.maximum, [jnp.max(x, axis=-1, keepdims=True) for x in ss])
        if extra_fn is not None:
            m = jnp.maximum(m, extra_fn(r0))
        den = None
        es = []
        for x in ss:
            e = jnp.exp2(x - m)
            part_sum = jnp.sum(e, axis=-1, keepdims=True)
            den = part_sum if den is None else den + part_sum
            es.append(e.astype(BF16))
        if extra_fn is not None:
            den = den + jnp.exp2(extra_fn(r0) - m)
        dens.append(den)
        ps.append(es[0] if len(es) == 1 else jnp.concatenate(es, axis=1))
    return jnp.concatenate(ps, axis=0), jnp.concatenate(dens, axis=0)


def _mod_kernel(cond_ref, w_ref, b_ref, o_ref):
    a = _silu(cond_ref[...]).astype(BF16)
    o_ref[0] = _dot(a, w_ref[0].astype(BF16)) + b_ref[0]


def _modulation(cond, ada_w, ada_b):
    depth, d, n = ada_w.shape
    rows = cond.shape[0]
    tn = 1024
    return pl.pallas_call(
        _mod_kernel,
        grid=(depth, n // tn),
        in_specs=[
            pl.BlockSpec((rows, d), lambda l, j: (0, 0)),
            pl.BlockSpec((1, d, tn), lambda l, j: (l, 0, j)),
            pl.BlockSpec((1, 1, tn), lambda l, j: (l, 0, j)),
        ],
        out_specs=pl.BlockSpec((1, rows, tn), lambda l, j: (l, 0, j)),
        out_shape=jax.ShapeDtypeStruct((depth, rows, n), F32),
        compiler_params=_cparams("parallel", "parallel"),
        name="modulation",
    )(cond, ada_w, ada_b.reshape(depth, 1, n))


def _prenorm_kernel(x_ref, g_ref, mod_ref, h_ref):
    d = x_ref.shape[-1]
    y = _rms(x_ref[...], g_ref[...])
    shift = mod_ref[0, :, 0:d]
    scale = mod_ref[0, :, d:2 * d]
    h_ref[...] = (y * (1.0 + scale) + shift).astype(BF16)


def _prenorm(x, g, mod, grp0, rows_per_grp, tm):
    m, d = x.shape
    tpg = rows_per_grp // tm
    return pl.pallas_call(
        _prenorm_kernel,
        grid=(m // tm,),
        in_specs=[
            pl.BlockSpec((tm, d), lambda i: (i, 0)),
            pl.BlockSpec((1, d), lambda i: (0, 0)),
            pl.BlockSpec((1, 1, 3 * d), lambda i: (grp0 + i // tpg, 0, 0)),
        ],
        out_specs=pl.BlockSpec((tm, d), lambda i: (i, 0)),
        out_shape=jax.ShapeDtypeStruct((m, d), BF16),
        compiler_params=_cparams("parallel"),
        name="prenorm",
    )(x, g.reshape(1, d), mod)


def _cast_kernel(w_ref, o_ref, *, q_tiles):
    scale = jnp.where(pl.program_id(0) < q_tiles, Q_FOLD, 1.0)
    o_ref[...] = (w_ref[...] * scale).astype(BF16)


def _cast_cols(w3, layer, first_tile, n_tiles, tw, q_tiles=0):
    k = w3.shape[1]
    return pl.pallas_call(
        functools.partial(_cast_kernel, q_tiles=q_tiles),
        grid=(n_tiles,),
        in_specs=[pl.BlockSpec((None, k, tw), lambda j: (layer, 0, first_tile + j))],
        out_specs=pl.BlockSpec((k, tw), lambda j: (0, j)),
        out_shape=jax.ShapeDtypeStruct((k, n_tiles * tw), BF16),
        compiler_params=_cparams("parallel"),
        name="weight_cast",
    )(w3)


def _matmul_kernel(h_ref, w_ref, o_ref):
    o_ref[...] = _dot(h_ref[...], w_ref[...]).astype(o_ref.dtype)


def _in_proj(h, w, out_dtype, tm, tn):
    m, k = h.shape
    n = w.shape[1]
    return pl.pallas_call(
        _matmul_kernel,
        grid=(m // tm, n // tn),
        in_specs=[
            pl.BlockSpec((tm, k), lambda i, j: (i, 0)),
            pl.BlockSpec((k, tn), lambda i, j: (0, j)),
        ],
        out_specs=pl.BlockSpec((tm, tn), lambda i, j: (i, j)),
        out_shape=jax.ShapeDtypeStruct((m, n), out_dtype),
        compiler_params=_cparams("parallel", "parallel"),
        name="in_proj",
    )(h, w)


def _matmul_kv_kernel(h_ref, w_ref, kin_ref, vin_ref, o_ref, k_ref, v_ref, *, k_col, v_col, width):
    del kin_ref, vin_ref
    tn = w_ref.shape[-1]
    j = pl.program_id(1)
    res = _dot(h_ref[...], w_ref[...])
    o_ref[...] = res.astype(o_ref.dtype)
    for col, ref in ((k_col, k_ref), (v_col, v_ref)):
        for jj in range(col // tn, (col + width - 1) // tn + 1):
            lo, hi = max(col, jj * tn), min(col + width, (jj + 1) * tn)

            @pl.when(j == jj)
            def _(lo=lo, hi=hi, jj=jj, col=col, ref=ref):
                piece = res[:, lo - jj * tn:hi - jj * tn]
                ref[:, :, lo - col:hi - col] = piece.reshape(ref.shape[0], ref.shape[1], hi - lo)


def _in_proj_kv(h, w, tm, tn, k_buf, v_buf, layer, k_col, v_col):
    m, k = h.shape
    n = w.shape[1]
    batch, _, seq, width = k_buf.shape
    assert tm % seq == 0
    kvspec = pl.BlockSpec((tm // seq, None, seq, width), lambda i, j: (i, layer, 0, 0))
    anyspec = pl.BlockSpec(memory_space=pl.ANY)
    return pl.pallas_call(
        functools.partial(_matmul_kv_kernel, k_col=k_col, v_col=v_col, width=width),
        grid=(m // tm, n // tn),
        in_specs=[
            pl.BlockSpec((tm, k), lambda i, j: (i, 0)),
            pl.BlockSpec((k, tn), lambda i, j: (0, j)),
            anyspec, anyspec,
        ],
        out_specs=[pl.BlockSpec((tm, tn), lambda i, j: (i, j)), kvspec, kvspec],
        out_shape=[jax.ShapeDtypeStruct((m, n), F32),
                   jax.ShapeDtypeStruct(k_buf.shape, F32), jax.ShapeDtypeStruct(v_buf.shape, F32)],
        input_output_aliases={2: 1, 3: 2},
        compiler_params=_cparams("arbitrary", "arbitrary"),
        name="in_proj_kv",
    )(h, w, k_buf, v_buf)


def _gla_merge(of_ref, ob_ref, gn_ref, gp0_ref, gp1_ref, rows):
    hpv = H_B // 2
    outs = []
    for h in range(H_B):
        vl = slice(h * DV_B, (h + 1) * DV_B)
        gp_ref = gp0_ref if h < hpv else gp1_ref
        gp = gp_ref[rows, (h % hpv) * DV_B:(h % hpv + 1) * DV_B].astype(F32)
        o = _rms(of_ref[rows, vl].astype(F32) + ob_ref[rows, vl].astype(F32), gn_ref[...])
        outs.append((o * _silu(gp)).astype(BF16))
    return jnp.concatenate(outs, axis=1)


def _outproj_kernel(*refs, final, fuse_gla):
    if fuse_gla:
        ya_ref, of_ref, ob_ref, gn_ref, gp0_ref, gp1_ref = refs[:6]
        refs = refs[6:]
    else:
        ya_ref, yb_ref = refs[:2]
        refs = refs[2:]
    w_ref, x_ref, mod_ref, g_ref, modn_ref = refs[:5]
    out_refs = refs[5:]
    d = x_ref.shape[-1]
    half = ya_ref.shape[-1]
    tm = x_ref.shape[0]
    gate = mod_ref[0, :, 2 * d:3 * d]
    for r0 in range(0, tm, OUT_SUB):
        rows = slice(r0, r0 + OUT_SUB)
        if fuse_gla:
            yb = _gla_merge(of_ref, ob_ref, gn_ref, gp0_ref, gp1_ref, rows)
        else:
            yb = yb_ref[rows, :]
        acc = _dot(ya_ref[rows, :], w_ref[0:half, :]) + _dot(yb, w_ref[half:2 * half, :])
        xn = x_ref[rows, :] + gate * acc
        if final:
            out_refs[0][rows, :] = _rms(xn, g_ref[...])
        else:
            out_refs[0][rows, :] = xn
            shift = modn_ref[0, :, 0:d]
            scale = modn_ref[0, :, d:2 * d]
            out_refs[1][rows, :] = (_rms(xn, g_ref[...]) * (1.0 + scale) + shift).astype(BF16)


def _out_proj(ya, second, w, x, mod, g_next, mod_next, grp0, rows_per_grp, tm, final):
    m, d = x.shape
    half = ya.shape[1]
    tpg = rows_per_grp // tm
    modspec = pl.BlockSpec((1, 1, 3 * d), lambda i: (grp0 + i // tpg, 0, 0))
    rowspec = pl.BlockSpec((tm, d), lambda i: (i, 0))
    halfspec = pl.BlockSpec((tm, half), lambda i: (i, 0))
    fuse_gla = isinstance(second, tuple)
    if fuse_gla:
        o_f, o_b, gnorm, proj = second
        gw = half // 2
        mix_specs = [halfspec, halfspec, halfspec,
                     pl.BlockSpec((1, DV_B), lambda i: (0, 0)),
                     pl.BlockSpec((tm, gw), lambda i: (i, EV_GPB // gw)),
                     pl.BlockSpec((tm, gw), lambda i: (i, EV_GPB // gw + 1))]
        mix_args = [ya, o_f, o_b, gnorm.reshape(1, DV_B), proj, proj]
    else:
        mix_specs = [halfspec, halfspec]
        mix_args = [ya, second]
    if final:
        out_shape = [jax.ShapeDtypeStruct((m, d), F32)]
        out_specs = [rowspec]
    else:
        out_shape = [jax.ShapeDtypeStruct((m, d), F32), jax.ShapeDtypeStruct((m, d), BF16)]
        out_specs = [rowspec, rowspec]
    return pl.pallas_call(
        functools.partial(_outproj_kernel, final=final, fuse_gla=fuse_gla),
        grid=(m // tm,),
        in_specs=mix_specs + [
            pl.BlockSpec((2 * half, d), lambda i: (0, 0)),
            rowspec, modspec,
            pl.BlockSpec((1, d), lambda i: (0, 0)),
            modspec,
        ],
        out_specs=out_specs,
        out_shape=out_shape,
        compiler_params=_cparams("parallel"),
        name="out_proj",
    )(*mix_args, w, x, mod, g_next.reshape(1, d), mod_next)


def _ctx_attn_kernel(*refs, n_heads, n_kv, has_sink):
    if has_sink:
        sink_ref, q_ref, k_ref, v_ref, gp0_ref, gp1_ref, o_ref = refs
    else:
        q_ref, k_ref, v_ref, gp0_ref, gp1_ref, o_ref = refs
    L = q_ref.shape[0]
    grp = n_heads // n_kv
    outs = [None] * n_heads
    for kv in range(n_kv):
        heads = list(range(kv * grp, (kv + 1) * grp))
        qg = jnp.concatenate(
            [q_ref[:, h * HEAD_DIM:(h + 1) * HEAD_DIM].astype(BF16) for h in heads], axis=0)
        k = k_ref[:, kv * HEAD_DIM:(kv + 1) * HEAD_DIM].astype(BF16)
        v = v_ref[:, kv * HEAD_DIM:(kv + 1) * HEAD_DIM].astype(BF16)
        s = _dot_nt(qg, k)
        sink_fn = None
        if has_sink:
            sk = jnp.concatenate([jnp.full((L, 1), sink_ref[h] * LOG2E, F32) for h in heads], axis=0)
            sink_fn = lambda r0: sk
        p, den = _softmax_rows([s], [None], sink_fn, chunked=False)
        o = _dot(p, v) / den
        for i, h in enumerate(heads):
            outs[h] = o[i * L:(i + 1) * L, :]
    _store_gated(o_ref, outs, gp0_ref, gp1_ref)


def _store_gated(o_ref, head_outs, gp0_ref, gp1_ref):
    hw = gp0_ref.shape[-1]
    per = hw // HEAD_DIM
    for i, gp_ref in enumerate((gp0_ref, gp1_ref)):
        y = jnp.concatenate(head_outs[i * per:(i + 1) * per], axis=1)
        o_ref[:, i * hw:(i + 1) * hw] = (y * _silu(gp_ref[...].astype(F32))).astype(BF16)


def _ctx_attention(proj, sink, batch, L, q_col, k_col, v_col, gp_col, n_heads, n_kv):
    qw = n_heads * HEAD_DIM
    kw = n_kv * HEAD_DIM
    gw = qw // 2
    has_sink = sink is not None
    in_specs = [
        pl.BlockSpec((L, qw), lambda b: (b, q_col // qw)),
        pl.BlockSpec((L, kw), lambda b: (b, k_col // kw)),
        pl.BlockSpec((L, kw), lambda b: (b, v_col // kw)),
        pl.BlockSpec((L, gw), lambda b: (b, gp_col // gw)),
        pl.BlockSpec((L, gw), lambda b: (b, gp_col // gw + 1)),
    ]
    args = [proj, proj, proj, proj, proj]
    if has_sink:
        in_specs = [pl.BlockSpec(memory_space=pltpu.SMEM)] + in_specs
        args = [sink] + args
    return pl.pallas_call(
        functools.partial(_ctx_attn_kernel, n_heads=n_heads, n_kv=n_kv, has_sink=has_sink),
        grid=(batch,),
        in_specs=in_specs,
        out_specs=pl.BlockSpec((L, qw), lambda b: (b, 0)),
        out_shape=jax.ShapeDtypeStruct((batch * L, qw), BF16),
        compiler_params=_cparams("parallel"),
        name="ctx_attention",
    )(*args)


def _rope(x, cos, sin_signed):
    lane = lax.broadcasted_iota(jnp.int32, x.shape, 1)
    first = (lane % 64) < 32
    swapped = jnp.where(first, pltpu.roll(x, 96, 1), pltpu.roll(x, 32, 1))
    return x * cos + swapped * sin_signed


def _win_attn_kernel(sink_ref, q_ref, kp_ref, kc_ref, kn_ref, vp_ref, vc_ref, vn_ref,
                     cq_ref, sq_ref, cp_ref, sp_ref, cn_ref, sn_ref, mask_ref,
                     ck_ref, cv_ref, gp0_ref, gp1_ref, o_ref, *, nsteps):
    p = pl.program_id(1)
    blk = BAND_BLK
    grp = H_A // KV_A
    cq, sq = cq_ref[...], sq_ref[...]
    lane3 = lax.broadcasted_iota(jnp.int32, (1, 3 * blk), 1)
    edges = []
    for t in range(WIN_QB):
        edge = jnp.zeros((1, 3 * blk), F32)
        if t == 0:
            edge = edge + jnp.where(lane3 < blk, jnp.where(p > 0, 0.0, NEG), 0.0)
        if t == WIN_QB - 1:
            edge = edge + jnp.where(lane3 >= 2 * blk, jnp.where(p < nsteps - 1, 0.0, NEG), 0.0)
        edges.append(edge)
    outs = [[None] * WIN_QB for _ in range(H_A)]
    for kv in range(KV_A):
        sl = slice(kv * HEAD_DIM, (kv + 1) * HEAD_DIM)
        k_all = jnp.concatenate([
            _rope(kp_ref[:, sl].astype(F32), cp_ref[...], sp_ref[...]),
            _rope(kc_ref[:, sl].astype(F32), cq, sq),
            _rope(kn_ref[:, sl].astype(F32), cn_ref[...], sn_ref[...]),
        ], axis=0).astype(BF16)
        v_all = jnp.concatenate([vp_ref[:, sl], vc_ref[:, sl], vn_ref[:, sl]], axis=0).astype(BF16)
        heads = list(range(kv * grp, (kv + 1) * grp))
        q_rot = [_rope(q_ref[:, h * HEAD_DIM:(h + 1) * HEAD_DIM].astype(F32), cq, sq).astype(BF16)
                 for h in heads]
        ck = ck_ref[:, sl]
        cv = cv_ref[:, sl]
        for t in range(WIN_QB):
            rows = slice(t * blk, (t + 1) * blk)
            band = slice(t * blk, (t + 3) * blk)
            qg = jnp.concatenate([q[rows, :] for q in q_rot], axis=0)
            s_band = _dot_nt(qg, k_all[band, :])
            s_ctx = _dot_nt(qg, ck)
            p, den = _softmax_rows(
                [s_band, s_ctx],
                [lambda r0, nr, t=t: mask_ref[r0 % blk:r0 % blk + nr, :] + edges[t], None],
                lambda r0: sink_ref[heads[r0 // blk]] * LOG2E, chunked=True)
            o = _dot(p, jnp.concatenate([v_all[band, :], cv], axis=0)) / den
            for i, h in enumerate(heads):
                outs[h][t] = o[i * blk:(i + 1) * blk, :]
    _store_gated(o_ref, [jnp.concatenate(o, axis=0) for o in outs], gp0_ref, gp1_ref)


def _window_attention(proj, sink, cache_k, cache_v, layer, cos_tab, sin_tab, batch, T):
    blk = BAND_BLK
    nb = T // blk
    qb = WIN_QB
    assert nb % qb == 0
    nsteps = nb // qb
    rq = qb * blk
    qw = H_A * HEAD_DIM
    kw = KV_A * HEAD_DIM
    gw = qw // 2
    P = cache_k.shape[2]

    def prev(p):
        return jnp.maximum(qb * p - 1, 0)

    def nxt(p):
        return jnp.minimum(qb * p + qb, nb - 1)

    kcol, vcol = EV_KA // kw, EV_VA // kw

    def edge(col, f):
        return pl.BlockSpec((blk, kw), lambda b, p: (b * nb + f(p), col))

    def mid(col):
        return pl.BlockSpec((rq, kw), lambda b, p: (b * nsteps + p, col))

    tab_edge = lambda f: pl.BlockSpec((blk, HEAD_DIM), lambda b, p: (f(p), 0))
    tab_mid = pl.BlockSpec((rq, HEAD_DIM), lambda b, p: (p, 0))
    in_specs = [
        pl.BlockSpec(memory_space=pltpu.SMEM),
        pl.BlockSpec((rq, qw), lambda b, p: (b * nsteps + p, EV_QA // qw)),
        edge(kcol, prev), mid(kcol), edge(kcol, nxt),
        edge(vcol, prev), mid(vcol), edge(vcol, nxt),
        tab_mid, tab_mid, tab_edge(prev), tab_edge(prev), tab_edge(nxt), tab_edge(nxt),
        pl.BlockSpec((blk, 3 * blk), lambda b, p: (0, 0)),
        pl.BlockSpec((None, None, P, kw), lambda b, p: (b, layer, 0, 0)),
        pl.BlockSpec((None, None, P, kw), lambda b, p: (b, layer, 0, 0)),
        pl.BlockSpec((rq, gw), lambda b, p: (b * nsteps + p, EV_GPA // gw)),
        pl.BlockSpec((rq, gw), lambda b, p: (b * nsteps + p, EV_GPA // gw + 1)),
    ]
    return pl.pallas_call(
        functools.partial(_win_attn_kernel, nsteps=nsteps),
        grid=(batch, nsteps),
        in_specs=in_specs,
        out_specs=pl.BlockSpec((rq, qw), lambda b, p: (b * nsteps + p, 0)),
        out_shape=jax.ShapeDtypeStruct((batch * T, qw), BF16),
        compiler_params=_cparams("parallel", "parallel"),
        name="window_attention",
    )(sink, proj, proj, proj, proj, proj, proj, proj,
      cos_tab, sin_tab, cos_tab, sin_tab, cos_tab, sin_tab, _band_mask(),
      cache_k, cache_v, proj, proj)


def _band_mask():
    iq = np.arange(BAND_BLK)[:, None]
    jk = np.arange(3 * BAND_BLK)[None, :]
    rel = iq - jk + BAND_BLK
    return jnp.asarray(np.where(np.abs(rel) <= WINDOW, 0.0, NEG), F32)


def _log_sigmoid(x):
    return jnp.minimum(x, 0.0) - jnp.log1p(jnp.exp(-jnp.abs(x)))


def _cumsum_rows(tri, g):
    g_hi = g.astype(BF16)
    r1 = g - g_hi.astype(F32)
    g_mid = r1.astype(BF16)
    g_lo = (r1 - g_mid.astype(F32)).astype(BF16)
    return _dot(tri, g_hi) + _dot(tri, g_mid) + _dot(tri, g_lo)


def _gla_kernel(*refs, nsteps, has_s0):
    fwd_refs, bwd_refs = refs[0:5], refs[5:10]
    wg_ref, bg_ref = refs[10:12]
    if has_s0:
        s0_ref, of_ref, ob_ref, s_scr = refs[12:]
        sfin_ref = None
    else:
        of_ref, ob_ref, sfin_ref, s_scr = refs[13:]
    qf = fwd_refs[0]
    step = pl.program_id(1)
    C = GLA_CHUNK
    G = qf.shape[0] // C
    R = G * C

    @pl.when(step == 0)
    def _():
        if has_s0:
            s_scr[...] = s0_ref[...]
        else:
            s_scr[...] = jnp.zeros(s_scr.shape, F32)

    ri = lax.broadcasted_iota(jnp.int32, (R, R), 0)
    ci = lax.broadcasted_iota(jnp.int32, (R, R), 1)
    same_chunk = (ri // C) == (ci // C)
    hw = H_B * DK_B
    hpv = H_B // 2
    for d, ((q_ref, k_ref, v0_ref, v1_ref, r_ref), o_ref) in enumerate(
            ((fwd_refs, of_ref), (bwd_refs, ob_ref))):
        keep = same_chunk & ((ci <= ri) if d == 0 else (ci >= ri))
        tri = jnp.where(keep, 1.0, 0.0).astype(BF16)
        logit = _dot(r_ref[...].astype(BF16), wg_ref[:, d * hw:(d + 1) * hw]) + bg_ref[:, d * hw:(d + 1) * hw]
        g = _log_sigmoid(logit) / GLA_NORMALIZER
        bc = _cumsum_rows(tri, g)
        edge = C - 1 if d == 0 else 0
        totals = [bc[j * C + edge:j * C + edge + 1, :] for j in range(G)]
        tot_full = jnp.concatenate([jnp.broadcast_to(t, (C, hw)) for t in totals], axis=0)
        e_pos = jnp.exp(bc)
        e_neg = jnp.exp(-bc)
        e_rem = jnp.exp(tot_full - bc)
        e_tot = [jnp.exp(t) for t in totals]
        order = range(G) if d == 0 else range(G - 1, -1, -1)
        for h in range(H_B):
            sl = slice(h * DK_B, (h + 1) * DK_B)
            vl = slice(h * DV_B, (h + 1) * DV_B)
            qh = q_ref[:, sl].astype(F32) * (DK_B ** -0.5)
            kh = k_ref[:, sl].astype(F32)
            v_ref = v0_ref if h < hpv else v1_ref
            vh = v_ref[:, (h % hpv) * DV_B:(h % hpv + 1) * DV_B].astype(BF16)
            qi = (qh * e_pos[:, sl]).astype(BF16)
            ki = (kh * e_neg[:, sl]).astype(BF16)
            ks = kh * e_rem[:, sl]
            att = jnp.where(keep, _dot_nt(qi, ki), 0.0).astype(BF16)
            o_intra = _dot(att, vh)
            S = s_scr[d, h]
            for j in order:
                rows = slice(j * C, (j + 1) * C)
                o_ref[rows, vl] = (o_intra[rows, :] + _dot(qi[rows, :], S.astype(BF16))).astype(o_ref.dtype)
                stacked = jnp.concatenate(
                    [ks[rows, :], jnp.broadcast_to(e_tot[j][:, sl], (C, DK_B))], axis=0)
                tr = stacked.T
                S = tr[:, C:C + 1] * S + _dot(tr[:, 0:C].astype(BF16), vh[rows, :])
            s_scr[d, h] = S

    if sfin_ref is not None:
        @pl.when(step == nsteps - 1)
        def _():
            sfin_ref[...] = s_scr[...]


def _gla(proj, r_proj, wg_full, bg_full, s0, state_buf, layer, batch, T):
    C = GLA_GROUP * GLA_CHUNK
    nc = T // C
    m = batch * T
    qw, vw, rw = H_B * DK_B, H_B * DV_B, 128
    vh = vw // 2

    def fwd(b, c):
        return b * nc + c

    def bwd(b, c):
        return b * nc + (nc - 1 - c)

    def specs(f):
        return [
            pl.BlockSpec((C, qw), lambda b, c: (f(b, c), EV_QB // qw)),
            pl.BlockSpec((C, qw), lambda b, c: (f(b, c), EV_KB // qw)),
            pl.BlockSpec((C, vh), lambda b, c: (f(b, c), EV_VB // vh)),
            pl.BlockSpec((C, vh), lambda b, c: (f(b, c), EV_VB // vh + 1)),
            pl.BlockSpec((C, rw), lambda b, c: (f(b, c), 0)),
        ]

    in_specs = specs(fwd) + specs(bwd) + [
        pl.BlockSpec((rw, 2 * qw), lambda b, c: (0, 0)),
        pl.BlockSpec((1, 2 * qw), lambda b, c: (0, 0)),
    ]
    args = [proj, proj, proj, proj, r_proj] * 2 + [wg_full, bg_full]
    has_s0 = s0 is not None
    state_spec = pl.BlockSpec((None, None, 2, H_B, DK_B, DV_B), lambda b, c: (b, layer, 0, 0, 0, 0))
    out_specs = [pl.BlockSpec((C, vw), lambda b, c: (fwd(b, c), 0)),
                 pl.BlockSpec((C, vw), lambda b, c: (bwd(b, c), 0))]
    out_shape = [jax.ShapeDtypeStruct((m, vw), BF16), jax.ShapeDtypeStruct((m, vw), BF16)]
    aliases = {}
    if has_s0:
        in_specs.append(state_spec)
        args.append(s0)
    else:
        in_specs.append(pl.BlockSpec(memory_space=pl.ANY))
        args.append(state_buf)
        out_specs.append(state_spec)
        out_shape.append(jax.ShapeDtypeStruct(state_buf.shape, F32))
        aliases = {len(args) - 1: 2}
    return pl.pallas_call(
        functools.partial(_gla_kernel, nsteps=nc, has_s0=has_s0),
        grid=(batch, nc),
        in_specs=in_specs,
        out_specs=out_specs,
        out_shape=out_shape,
        input_output_aliases=aliases,
        scratch_shapes=[pltpu.VMEM((2, H_B, DK_B, DV_B), F32)],
        compiler_params=_cparams("arbitrary", "arbitrary"),
        name="gla_scan",
    )(*args)


def _na_bias_kernel(u_ref, o_ref):
    place = pl.program_id(0)
    W = GRID_W
    lane = lax.broadcasted_iota(jnp.int32, (W, 128), 1)
    qcol = lax.broadcasted_iota(jnp.int32, (W, 128), 0)
    kcol = lane % W
    cs = jnp.clip(qcol - NA_KW // 2, 0, W - NA_KW)
    col_ok = (kcol >= cs) & (kcol < cs + NA_KW)
    for t in range(NA_QR):
        first = jnp.where(place == 0, 0, jnp.where(place == 1, t, NA_WR - NA_KH))

        def piece(i, base_shift):
            dr = i - t + (NA_KH - 1) - (NA_WR - NA_KH) * place
            in_rows = (i >= first) & (i < first + NA_KH)
            u = jnp.broadcast_to(u_ref[0, pl.ds(jnp.clip(dr, 0, 2 * NA_KH - 2), 1), :], (W, 128))
            rolled = pltpu.roll(u, base_shift, 1, stride=1, stride_axis=0)
            return rolled + jnp.where(in_rows, 0.0, NEG)

        tiles = []
        for i in range(0, NA_WR, 2):
            tile = jnp.where(lane < W, piece(i, 65), piece(i + 1, 1))
            tiles.append(jnp.where(col_ok, tile, NEG) * LOG2E)
        o_ref[0, 0, t * W:(t + 1) * W, :] = jnp.concatenate(tiles, axis=1)


def _na_bias_expand(u_tab):
    W = GRID_W
    n_tab = u_tab.shape[1]
    return pl.pallas_call(
        _na_bias_kernel,
        grid=(3, H_C),
        in_specs=[pl.BlockSpec((1, n_tab, 128), lambda p, h: (h, 0, 0))],
        out_specs=pl.BlockSpec((1, 1, NA_QR * W, NA_WR * W), lambda p, h: (p, h, 0, 0)),
        out_shape=jax.ShapeDtypeStruct((3, H_C, NA_QR * W, NA_WR * W), F32),
        compiler_params=_cparams("parallel", "parallel"),
        name="na_bias_expand",
    )(u_tab)


def _na_kernel(*refs):
    nblk = NA_WR // NA_QR
    q_ref = refs[0]
    k_refs = refs[1:1 + nblk]
    v_refs = refs[1 + nblk:1 + 2 * nblk]
    bias_ref, ck_ref, cv_ref, gp_ref, o_ref = refs[1 + 2 * nblk:]
    outs = []
    for h in range(H_C):
        sl = slice(h * HEAD_DIM, (h + 1) * HEAD_DIM)
        q = q_ref[:, sl]
        kw = jnp.concatenate([r[:, sl] for r in k_refs], axis=0)
        v_cat = jnp.concatenate([r[:, sl] for r in v_refs] + [cv_ref[:, sl]], axis=0)
        s_nb = _dot_nt(q, kw)
        s_ctx = _dot_nt(q, ck_ref[:, sl])
        p, den = _softmax_rows([s_nb, s_ctx],
                               [lambda r0, nr, h=h: bias_ref[0, h, r0:r0 + nr, :], None], None,
                               chunked=False)
        outs.append(_dot(p, v_cat) / den)
    y = jnp.concatenate(outs, axis=1)
    o_ref[...] = (y * _silu(gp_ref[...].astype(F32))).astype(BF16)


def _neighbourhood_attention(proj, bias_exp, cache_k, cache_v, layer, batch, T):
    W = GRID_W
    rows = T // W
    assert rows % NA_QR == 0 and rows >= NA_WR and NA_WR % NA_QR == 0
    steps = rows // NA_QR
    nblk = NA_WR // NA_QR
    rq = NA_QR * W
    qw = H_C * HEAD_DIM
    P = cache_k.shape[2]

    def win0(p):
        return jnp.clip(p - 1, 0, steps - nblk)

    def place(p):
        return jnp.where(p == 0, 0, jnp.where(p == steps - 1, 2, 1))

    def win_spec(col, i):
        return pl.BlockSpec((rq, qw), lambda b, p: (b * steps + win0(p) + i, col // qw))

    in_specs = ([pl.BlockSpec((rq, qw), lambda b, p: (b * steps + p, OD_QC // qw))]
                + [win_spec(OD_KC, i) for i in range(nblk)]
                + [win_spec(OD_VC, i) for i in range(nblk)]
                + [pl.BlockSpec((1, H_C, rq, NA_WR * W), lambda b, p: (place(p), 0, 0, 0)),
                   pl.BlockSpec((None, None, P, qw), lambda b, p: (b, layer, 0, 0)),
                   pl.BlockSpec((None, None, P, qw), lambda b, p: (b, layer, 0, 0)),
                   pl.BlockSpec((rq, qw), lambda b, p: (b * steps + p, OD_GPC // qw))])
    return pl.pallas_call(
        _na_kernel,
        grid=(batch, steps),
        in_specs=in_specs,
        out_specs=pl.BlockSpec((rq, qw), lambda b, p: (b * steps + p, 0)),
        out_shape=jax.ShapeDtypeStruct((batch * T, qw), BF16),
        compiler_params=_cparams("parallel", "parallel"),
        name="neighbourhood_attention",
    )(proj, *([proj] * (2 * nblk)), bias_exp, cache_k, cache_v, proj)


def _gmlp_kernel(u_ref, v_ref, gp_ref, g_ref, ws_ref, bt_ref, o_ref, *, chunks):
    gw = u_ref.shape[-1] // G_D
    for ch in range(chunks):
        rs_ = slice(ch * D_CHUNK, (ch + 1) * D_CHUNK)
        v = v_ref[rs_, :].astype(F32)
        mu = jnp.mean(v, axis=-1, keepdims=True)
        var = jnp.mean(jnp.square(v - mu), axis=-1, keepdims=True)
        vn = ((v - mu) * lax.rsqrt(var + 1e-5) * g_ref[...]).astype(BF16)
        for g in range(G_D):
            cl = slice(g * gw, (g + 1) * gw)
            sp = _dot(ws_ref[g], vn[:, cl]) + bt_ref[:, g:g + 1]
            y = u_ref[rs_, cl].astype(F32) * sp
            o_ref[rs_, cl] = (y * _silu(gp_ref[rs_, cl].astype(F32))).astype(BF16)


def _gmlp(proj, gnorm, ws_bf, b_t, tm):
    m = proj.shape[0]
    uw = HALF_W
    return pl.pallas_call(
        functools.partial(_gmlp_kernel, chunks=tm // D_CHUNK),
        grid=(m // tm,),
        in_specs=[
            pl.BlockSpec((tm, uw), lambda i: (i, OD_U // uw)),
            pl.BlockSpec((tm, uw), lambda i: (i, OD_V // uw)),
            pl.BlockSpec((tm, uw), lambda i: (i, OD_GPD // uw)),
            pl.BlockSpec((1, uw), lambda i: (0, 0)),
            pl.BlockSpec(ws_bf.shape, lambda i: (0, 0, 0)),
            pl.BlockSpec(b_t.shape, lambda i: (0, 0)),
        ],
        out_specs=pl.BlockSpec((tm, uw), lambda i: (i, 0)),
        out_shape=jax.ShapeDtypeStruct((m, uw), BF16),
        compiler_params=_cparams("parallel"),
        name="gmlp",
    )(proj, proj, proj, gnorm.reshape(1, uw), ws_bf, b_t)


def _rope_tables(T):
    t = np.arange(T)
    nf = HEAD_DIM // 4
    inv = ROPE_BASE ** (-jnp.arange(nf, dtype=F32) / nf)
    pos = jnp.stack([t // GRID_W, t % GRID_W], axis=-1).astype(F32)
    ang = pos[:, :, None] * inv
    cos, sin = jnp.cos(ang), jnp.sin(ang)
    cos_tab = jnp.concatenate([cos, cos], axis=-1).reshape(T, HEAD_DIM)
    sin_tab = jnp.concatenate([-sin, sin], axis=-1).reshape(T, HEAD_DIM)
    return cos_tab, sin_tab


def kernel(x_prompt, x_sample, cache_attn_k, cache_attn_v, state_gla, cache_na_k, cache_na_v,
           c, c_ctx, ada_w, ada_b, norm_g, w_in_even, w_in_odd, w_out, attn_sink,
           gla_wg, gla_bg, gla_norm_g, na_bias, gmlp_norm_g, gmlp_ws, gmlp_b, final_norm_g):
    B, L, D = x_prompt.shape
    Bd, T, _ = x_sample.shape
    depth = ada_w.shape[0]
    n_even = w_in_even.shape[0]
    n_odd = w_in_odd.shape[0]
    P = cache_attn_k.shape[2]
    Mc, Ml = B * L, Bd * T

    n_rows = 8 * ((1 + Bd + 7) // 8)
    cond = jnp.concatenate([c_ctx[None, :], c, jnp.zeros((n_rows - 1 - Bd, D), F32)], axis=0)
    mod = _modulation(cond, ada_w, ada_b)
    mod = mod.reshape(depth, n_rows, 1, 3 * D)

    tw = 512
    col_scale = jnp.concatenate([jnp.full((H_A * HEAD_DIM,), Q_FOLD, F32),
                                 jnp.ones((N_EVEN_COLS - H_A * HEAD_DIM,), F32)])
    w_even = [(jnp.concatenate([w_in_even[e, :, :EV_GP], w_in_even[e, :, EV_GP + 2 * GLA_RANK:]], axis=-1)
               * col_scale).astype(BF16) for e in range(n_even)]
    w_rank = [w_in_even[e, :, EV_RB_SRC:EV_RB_SRC + 128].astype(BF16) for e in range(n_even)]
    w_odd = [_cast_cols(w_in_odd, o, 0, N_ODD_COLS // tw, tw, q_tiles=H_C * HEAD_DIM // tw)
             for o in range(n_odd)]
    w_o = [_cast_cols(w_out, l, 0, D // tw, tw) for l in range(depth)]
    hw = H_B * DK_B
    wg_full, bg_full = [], []
    for e in range(n_even):
        wg = jnp.zeros((128, 2 * hw), F32)
        wg = wg.at[0:GLA_RANK, 0:hw].set(gla_wg[e, 0])
        wg = wg.at[GLA_RANK:2 * GLA_RANK, hw:2 * hw].set(gla_wg[e, 1])
        wg_full.append(wg.astype(BF16))
        bg_full.append(gla_bg[e].reshape(1, 2 * hw))
    pad_l = 63 - (NA_KW - 1)
    u_tabs = [jnp.pad(na_bias[o], ((0, 0), (0, 0), (pad_l, 128 - pad_l - (2 * NA_KW - 1))), mode="edge")
              for o in range(n_odd)]
    ws_bf = [gmlp_ws[o].astype(BF16) for o in range(n_odd)]
    b_t = [gmlp_b[o].T for o in range(n_odd)]
    cos_tab, sin_tab = _rope_tables(T)
    ck_a = cache_attn_k.reshape(Bd, n_even, P, KV_A * HEAD_DIM).astype(BF16)
    cv_a = cache_attn_v.reshape(Bd, n_even, P, KV_A * HEAD_DIM).astype(BF16)
    ck_c = cache_na_k.reshape(Bd, n_odd, P, H_C * HEAD_DIM).astype(BF16)
    cv_c = cache_na_v.reshape(Bd, n_odd, P, H_C * HEAD_DIM).astype(BF16)

    def run_group(x, is_ctx):
        batch, seq = (B, L) if is_ctx else (Bd, T)
        m = batch * seq
        grp0 = 0 if is_ctx else 1
        rows_per_grp = m if is_ctx else seq
        tm_mm = min(2048, rows_per_grp)
        tm_kv = min(1024, rows_per_grp)
        tm_op = min(512, rows_per_grp)
        side = None
        if is_ctx:
            side = dict(ka=jnp.zeros((B, n_even, L, KV_A * HEAD_DIM), F32),
                        va=jnp.zeros((B, n_even, L, KV_A * HEAD_DIM), F32),
                        st=jnp.zeros((B, n_even, 2, H_B, DK_B, DV_B), F32),
                        kc=jnp.zeros((B, n_odd, L, H_C * HEAD_DIM), F32),
                        vc=jnp.zeros((B, n_odd, L, H_C * HEAD_DIM), F32))
        h = _prenorm(x, norm_g[0], mod[0], grp0, rows_per_grp, tm_op)
        out = None
        for l in range(depth):
            if l % 2 == 0:
                e = l // 2
                r_proj = _in_proj(h, w_rank[e], BF16, tm_mm, 128)
                if is_ctx:
                    proj, side["ka"], side["va"] = _in_proj_kv(
                        h, w_even[e], tm_kv, 512, side["ka"], side["va"], e, EV_KA, EV_VA)
                    ya = _ctx_attention(proj, attn_sink[e], batch, seq, EV_QA, EV_KA, EV_VA, EV_GPA,
                                        H_A, KV_A)
                    o_f, o_b, side["st"] = _gla(proj, r_proj, wg_full[e], bg_full[e], None, side["st"],
                                                e, batch, seq)
                else:
                    proj = _in_proj(h, w_even[e], BF16, tm_mm, 512)
                    ya = _window_attention(proj, attn_sink[e], ck_a, cv_a, e, cos_tab, sin_tab, batch, seq)
                    o_f, o_b = _gla(proj, r_proj, wg_full[e], bg_full[e], state_gla, None, e, batch, seq)
                second = (o_f, o_b, gla_norm_g[e], proj)
            else:
                o = l // 2
                if is_ctx:
                    proj, side["kc"], side["vc"] = _in_proj_kv(
                        h, w_odd[o], tm_kv, 512, side["kc"], side["vc"], o, OD_KC, OD_VC)
                    ya = _ctx_attention(proj, None, batch, seq, OD_QC, OD_KC, OD_VC, OD_GPC, H_C, H_C)
                else:
                    proj = _in_proj(h, w_odd[o], BF16, tm_mm, 1024)
                    bias_exp = _na_bias_expand(u_tabs[o])
                    ya = _neighbourhood_attention(proj, bias_exp, ck_c, cv_c, o, batch, seq)
                second = _gmlp(proj, gmlp_norm_g[o], ws_bf[o], b_t[o], tm_op)
            final = l == depth - 1
            g_next = final_norm_g if final else norm_g[l + 1]
            mod_next = mod[l] if final else mod[l + 1]
            res = _out_proj(ya, second, w_o[l], x, mod[l], g_next, mod_next, grp0, rows_per_grp, tm_op, final)
            if final:
                out = res[0]
            else:
                x, h = res
        return out, side

    y_prompt, side = run_group(x_prompt.reshape(Mc, D), True)
    y_sample, _ = run_group(x_sample.reshape(Ml, D), False)

    return (y_prompt.reshape(B, L, D), y_sample.reshape(Bd, T, D),
            side["ka"].reshape(B, n_even, L, KV_A, HEAD_DIM), side["va"].reshape(B, n_even, L, KV_A, HEAD_DIM),
            side["st"],
            side["kc"].reshape(B, n_odd, L, H_C, HEAD_DIM), side["vc"].reshape(B, n_odd, L, H_C, HEAD_DIM))
```

```python
import functools

import numpy as np
import jax
import jax.numpy as jnp
from jax import lax
from jax.experimental import pallas as pl
from jax.experimental.pallas import tpu as pltpu

F32 = jnp.float32
BF16 = jnp.bfloat16

D_MODEL = 2048
HEAD_DIM = 128
GRID_W = 64
HALF_W = D_MODEL // 2
H_A = 8
KV_A = 2
WINDOW = 128
BAND_BLK = 128
H_B = 4
DK_B = 128
DV_B = 256
GLA_RANK = 16
GLA_NORMALIZER = 16.0
GLA_CHUNK = 64
GLA_GROUP = 4
H_C = 8
NA_KH = 8
NA_KW = 16
NA_QR = 4
NA_WR = 12
G_D = 4
D_CHUNK = 128
ROPE_BASE = 10000.0
NEG = -1e30
LOG2E = 1.4426950408889634
Q_FOLD = HEAD_DIM ** -0.5 * LOG2E

EV_QA, EV_KA, EV_VA, EV_QB, EV_KB, EV_VB = 0, 1024, 1280, 1536, 2048, 2560
EV_GP = 3584
EV_GPA, EV_GPB = EV_GP, EV_GP + 1024
EV_RB_SRC = 3584
N_EVEN_COLS = 5632
OUT_SUB = 256
WIN_QB = 2
SM_VREGS = 32
OD_QC, OD_KC, OD_VC, OD_U, OD_V, OD_GPC, OD_GPD = 0, 1024, 2048, 3072, 4096, 5120, 6144
N_ODD_COLS = 7168

VMEM_LIMIT = 52 * 1024 * 1024


def _cparams(*sem):
    return pltpu.CompilerParams(dimension_semantics=sem, vmem_limit_bytes=VMEM_LIMIT)


def _silu(x):
    return x / (1.0 + jnp.exp(-x))


def _dot(a, b):
    return jnp.dot(a, b, preferred_element_type=F32)


def _dot_nt(a, b):
    return lax.dot_general(a, b, (((1,), (1,)), ((), ())), preferred_element_type=F32)


def _rms(x, g, eps=1e-6):
    return x * lax.rsqrt(jnp.mean(x * x, axis=-1, keepdims=True) + eps) * g


def _sm_rows(n_cols):
    return max(16, min(128, (SM_VREGS * 1024 // n_cols) // 16 * 16))


def _softmax_rows(parts, add_fns, extra_fn, chunked):
    m_rows = parts[0].shape[0]
    nr = _sm_rows(sum(s.shape[1] for s in parts)) if chunked else m_rows
    extras, ps = [], []
    for r0 in range(0, m_rows, nr):
        rows = slice(r0, r0 + nr)
        ss = []
        for s, fn in zip(parts, add_fns):
            x = s[rows, :]
            if fn is not None:
                x = x + fn(r0, nr)
            ss.append(x)
        m = functools.reduce(jnp.maximum, [jnp.max(x, axis=-1, keepdims=True) for x in ss])
        if extra_fn is not None:
            m = jnp.maximum(m, extra_fn(r0))
            extras.append(jnp.exp2(extra_fn(r0) - m))
        es = [jnp.exp2((x - m).astype(BF16)) for x in ss]
        ps.append(es[0] if len(es) == 1 else jnp.concatenate(es, axis=1))
    extra_w = jnp.concatenate(extras, axis=0) if extras else None
    return jnp.concatenate(ps, axis=0), extra_w


def _softmax_small(s, extra):
    m = jnp.max(s, axis=-1, keepdims=True)
    if extra is not None:
        m = jnp.maximum(m, extra)
    e = jnp.exp2(s - m)
    den = jnp.sum(e, axis=-1, keepdims=True)
    if extra is not None:
        den = den + jnp.exp2(extra - m)
    return e.astype(BF16), den


def _pv_normalised(p, v, extra_w):
    o2 = _dot(p, jnp.concatenate([v, jnp.ones_like(v)], axis=1))
    den = o2[:, HEAD_DIM:HEAD_DIM + 1]
    if extra_w is not None:
        den = den + extra_w
    return o2[:, 0:HEAD_DIM] / den


def _mod_kernel(cond_ref, w_ref, b_ref, o_ref):
    a = _silu(cond_ref[...]).astype(BF16)
    o_ref[0] = _dot(a, w_ref[0].astype(BF16)) + b_ref[0]


def _modulation(cond, ada_w, ada_b):
    depth, d, n = ada_w.shape
    rows = cond.shape[0]
    tn = 1024
    return pl.pallas_call(
        _mod_kernel,
        grid=(depth, n // tn),
        in_specs=[
            pl.BlockSpec((rows, d), lambda l, j: (0, 0)),
            pl.BlockSpec((1, d, tn), lambda l, j: (l, 0, j)),
            pl.BlockSpec((1, 1, tn), lambda l, j: (l, 0, j)),
        ],
        out_specs=pl.BlockSpec((1, rows, tn), lambda l, j: (l, 0, j)),
        out_shape=jax.ShapeDtypeStruct((depth, rows, n), F32),
        compiler_params=_cparams("parallel", "parallel"),
        name="modulation",
    )(cond, ada_w, ada_b.reshape(depth, 1, n))


def _prenorm_kernel(x_ref, g_ref, mod_ref, h_ref):
    d = x_ref.shape[-1]
    y = _rms(x_ref[...], g_ref[...])
    shift = mod_ref[0, :, 0:d]
    scale = mod_ref[0, :, d:2 * d]
    h_ref[...] = (y * (1.0 + scale) + shift).astype(BF16)


def _prenorm(x, g, mod, grp0, rows_per_grp, tm):
    m, d = x.shape
    tpg = rows_per_grp // tm
    return pl.pallas_call(
        _prenorm_kernel,
        grid=(m // tm,),
        in_specs=[
            pl.BlockSpec((tm, d), lambda i: (i, 0)),
            pl.BlockSpec((1, d), lambda i: (0, 0)),
            pl.BlockSpec((1, 1, 3 * d), lambda i: (grp0 + i // tpg, 0, 0)),
        ],
        out_specs=pl.BlockSpec((tm, d), lambda i: (i, 0)),
        out_shape=jax.ShapeDtypeStruct((m, d), BF16),
        compiler_params=_cparams("parallel"),
        name="prenorm",
    )(x, g.reshape(1, d), mod)


def _cast_kernel(w_ref, o_ref, *, q_tiles):
    scale = jnp.where(pl.program_id(0) < q_tiles, Q_FOLD, 1.0)
    o_ref[...] = (w_ref[...] * scale).astype(BF16)


def _cast_cols(w3, layer, first_tile, n_tiles, tw, q_tiles=0):
    k = w3.shape[1]
    return pl.pallas_call(
        functools.partial(_cast_kernel, q_tiles=q_tiles),
        grid=(n_tiles,),
        in_specs=[pl.BlockSpec((None, k, tw), lambda j: (layer, 0, first_tile + j))],
        out_specs=pl.BlockSpec((k, tw), lambda j: (0, j)),
        out_shape=jax.ShapeDtypeStruct((k, n_tiles * tw), BF16),
        compiler_params=_cparams("parallel"),
        name="weight_cast",
    )(w3)


def _cast_t_kernel(a_ref, b_ref, o_ref, *, q_tiles, plain_tiles, skip):
    j = pl.program_id(0)

    @pl.when(j < plain_tiles)
    def _():
        scale = jnp.where(j < q_tiles, Q_FOLD, 1.0)
        o_ref[...] = (a_ref[...] * scale).T.astype(BF16)

    @pl.when(j >= plain_tiles)
    def _():
        x = jnp.concatenate([a_ref[skip:, :], b_ref[...]], axis=0)
        o_ref[...] = x.T.astype(BF16)


def _cast_even_t(w3t, layer):
    n_src, k = w3t.shape[1:]
    tw = 512
    skip = 2 * GLA_RANK
    assert EV_GP % tw == 0 and N_EVEN_COLS % tw == 0 and n_src == N_EVEN_COLS + skip
    return pl.pallas_call(
        functools.partial(_cast_t_kernel, q_tiles=H_A * HEAD_DIM // tw, plain_tiles=EV_GP // tw, skip=skip),
        grid=(N_EVEN_COLS // tw,),
        in_specs=[pl.BlockSpec((None, tw, k), lambda j: (layer, j, 0)),
                  pl.BlockSpec((None, skip, k), lambda j: (layer, (tw // skip) * (j + 1), 0))],
        out_specs=pl.BlockSpec((k, tw), lambda j: (0, j)),
        out_shape=jax.ShapeDtypeStruct((k, N_EVEN_COLS), BF16),
        compiler_params=_cparams("parallel"),
        name="weight_cast_even",
    )(w3t, w3t)


def _cast_rows_t_kernel(a_ref, o_ref):
    o_ref[...] = a_ref[...].T.astype(BF16)


def _cast_rows_t(w3t, layer, row0, n_rows):
    k = w3t.shape[2]
    return pl.pallas_call(
        _cast_rows_t_kernel,
        grid=(1,),
        in_specs=[pl.BlockSpec((None, n_rows, k), lambda j: (layer, row0 // n_rows, 0))],
        out_specs=pl.BlockSpec((k, n_rows), lambda j: (0, 0)),
        out_shape=jax.ShapeDtypeStruct((k, n_rows), BF16),
        compiler_params=_cparams("parallel"),
        name="weight_cast_rows",
    )(w3t)


def _matmul_kernel(h_ref, w_ref, o_ref):
    o_ref[...] = _dot(h_ref[...], w_ref[...]).astype(o_ref.dtype)


def _in_proj(h, w, out_dtype, tm, tn):
    m, k = h.shape
    n = w.shape[1]
    return pl.pallas_call(
        _matmul_kernel,
        grid=(m // tm, n // tn),
        in_specs=[
            pl.BlockSpec((tm, k), lambda i, j: (i, 0)),
            pl.BlockSpec((k, tn), lambda i, j: (0, j)),
        ],
        out_specs=pl.BlockSpec((tm, tn), lambda i, j: (i, j)),
        out_shape=jax.ShapeDtypeStruct((m, n), out_dtype),
        compiler_params=_cparams("parallel", "parallel"),
        name="in_proj",
    )(h, w)


def _matmul_kv_kernel(h_ref, w_ref, kin_ref, vin_ref, o_ref, k_ref, v_ref, *, k_col, v_col, width):
    del kin_ref, vin_ref
    tn = w_ref.shape[-1]
    j = pl.program_id(1)
    res = _dot(h_ref[...], w_ref[...])
    o_ref[...] = res.astype(o_ref.dtype)
    for col, ref in ((k_col, k_ref), (v_col, v_ref)):
        for jj in range(col // tn, (col + width - 1) // tn + 1):
            lo, hi = max(col, jj * tn), min(col + width, (jj + 1) * tn)

            @pl.when(j == jj)
            def _(lo=lo, hi=hi, jj=jj, col=col, ref=ref):
                piece = res[:, lo - jj * tn:hi - jj * tn]
                ref[:, :, lo - col:hi - col] = piece.reshape(ref.shape[0], ref.shape[1], hi - lo)


def _in_proj_kv(h, w, tm, tn, k_buf, v_buf, layer, k_col, v_col):
    m, k = h.shape
    n = w.shape[1]
    batch, _, seq, width = k_buf.shape
    assert tm % seq == 0
    kvspec = pl.BlockSpec((tm // seq, None, seq, width), lambda i, j: (i, layer, 0, 0))
    anyspec = pl.BlockSpec(memory_space=pl.ANY)
    return pl.pallas_call(
        functools.partial(_matmul_kv_kernel, k_col=k_col, v_col=v_col, width=width),
        grid=(m // tm, n // tn),
        in_specs=[
            pl.BlockSpec((tm, k), lambda i, j: (i, 0)),
            pl.BlockSpec((k, tn), lambda i, j: (0, j)),
            anyspec, anyspec,
        ],
        out_specs=[pl.BlockSpec((tm, tn), lambda i, j: (i, j)), kvspec, kvspec],
        out_shape=[jax.ShapeDtypeStruct((m, n), F32),
                   jax.ShapeDtypeStruct(k_buf.shape, F32), jax.ShapeDtypeStruct(v_buf.shape, F32)],
        input_output_aliases={2: 1, 3: 2},
        compiler_params=_cparams("arbitrary", "arbitrary"),
        name="in_proj_kv",
    )(h, w, k_buf, v_buf)


def _gla_merge(of_ref, ob_ref, gn_ref, gp0_ref, gp1_ref, rows):
    hpv = H_B // 2
    outs = []
    for h in range(H_B):
        vl = slice(h * DV_B, (h + 1) * DV_B)
        gp_ref = gp0_ref if h < hpv else gp1_ref
        gp = gp_ref[rows, (h % hpv) * DV_B:(h % hpv + 1) * DV_B].astype(F32)
        o = _rms(of_ref[rows, vl].astype(F32) + ob_ref[rows, vl].astype(F32), gn_ref[...])
        outs.append((o * _silu(gp)).astype(BF16))
    return jnp.concatenate(outs, axis=1)


def _outproj_kernel(*refs, final, fuse_gla):
    if fuse_gla:
        ya_ref, of_ref, ob_ref, gn_ref, gp0_ref, gp1_ref = refs[:6]
        refs = refs[6:]
    else:
        ya_ref, yb_ref = refs[:2]
        refs = refs[2:]
    w_ref, x_ref, mod_ref, g_ref, modn_ref = refs[:5]
    out_refs = refs[5:]
    d = x_ref.shape[-1]
    half = ya_ref.shape[-1]
    tm = x_ref.shape[0]
    gate = mod_ref[0, :, 2 * d:3 * d]
    for r0 in range(0, tm, OUT_SUB):
        rows = slice(r0, r0 + OUT_SUB)
        if fuse_gla:
            yb = _gla_merge(of_ref, ob_ref, gn_ref, gp0_ref, gp1_ref, rows)
        else:
            yb = yb_ref[rows, :]
        acc = _dot(ya_ref[rows, :], w_ref[0:half, :]) + _dot(yb, w_ref[half:2 * half, :])
        xn = x_ref[rows, :] + gate * acc
        if final:
            out_refs[0][rows, :] = _rms(xn, g_ref[...])
        else:
            out_refs[0][rows, :] = xn
            shift = modn_ref[0, :, 0:d]
            scale = modn_ref[0, :, d:2 * d]
            out_refs[1][rows, :] = (_rms(xn, g_ref[...]) * (1.0 + scale) + shift).astype(BF16)


def _out_proj(ya, second, w, x, mod, g_next, mod_next, grp0, rows_per_grp, tm, final):
    m, d = x.shape
    half = ya.shape[1]
    tpg = rows_per_grp // tm
    modspec = pl.BlockSpec((1, 1, 3 * d), lambda i: (grp0 + i // tpg, 0, 0))
    rowspec = pl.BlockSpec((tm, d), lambda i: (i, 0))
    halfspec = pl.BlockSpec((tm, half), lambda i: (i, 0))
    fuse_gla = isinstance(second, tuple)
    if fuse_gla:
        o_f, o_b, gnorm, proj = second
        gw = half // 2
        mix_specs = [halfspec, halfspec, halfspec,
                     pl.BlockSpec((1, DV_B), lambda i: (0, 0)),
                     pl.BlockSpec((tm, gw), lambda i: (i, EV_GPB // gw)),
                     pl.BlockSpec((tm, gw), lambda i: (i, EV_GPB // gw + 1))]
        mix_args = [ya, o_f, o_b, gnorm.reshape(1, DV_B), proj, proj]
    else:
        mix_specs = [halfspec, halfspec]
        mix_args = [ya, second]
    if final:
        out_shape = [jax.ShapeDtypeStruct((m, d), F32)]
        out_specs = [rowspec]
    else:
        out_shape = [jax.ShapeDtypeStruct((m, d), F32), jax.ShapeDtypeStruct((m, d), BF16)]
        out_specs = [rowspec, rowspec]
    return pl.pallas_call(
        functools.partial(_outproj_kernel, final=final, fuse_gla=fuse_gla),
        grid=(m // tm,),
        in_specs=mix_specs + [
            pl.BlockSpec((2 * half, d), lambda i: (0, 0)),
            rowspec, modspec,
            pl.BlockSpec((1, d), lambda i: (0, 0)),
            modspec,
        ],
        out_specs=out_specs,
        out_shape=out_shape,
        compiler_params=_cparams("parallel"),
        name="out_proj",
    )(*mix_args, w, x, mod, g_next.reshape(1, d), mod_next)


def _ctx_attn_kernel(*refs, n_heads, n_kv, has_sink):
    if has_sink:
        sink_ref, q_ref, k_ref, v_ref, gp0_ref, gp1_ref, o_ref = refs
    else:
        q_ref, k_ref, v_ref, gp0_ref, gp1_ref, o_ref = refs
    L = q_ref.shape[0]
    grp = n_heads // n_kv
    outs = [None] * n_heads
    for kv in range(n_kv):
        heads = list(range(kv * grp, (kv + 1) * grp))
        qg = jnp.concatenate(
            [q_ref[:, h * HEAD_DIM:(h + 1) * HEAD_DIM].astype(BF16) for h in heads], axis=0)
        k = k_ref[:, kv * HEAD_DIM:(kv + 1) * HEAD_DIM].astype(BF16)
        v = v_ref[:, kv * HEAD_DIM:(kv + 1) * HEAD_DIM].astype(BF16)
        s = _dot_nt(qg, k)
        sk = None
        if has_sink:
            sk = jnp.concatenate([jnp.full((L, 1), sink_ref[h] * LOG2E, F32) for h in heads], axis=0)
        p, den = _softmax_small(s, sk)
        o = _dot(p, v) / den
        for i, h in enumerate(heads):
            outs[h] = o[i * L:(i + 1) * L, :]
    _store_gated(o_ref, outs, gp0_ref, gp1_ref)


def _store_gated(o_ref, head_outs, gp0_ref, gp1_ref):
    hw = gp0_ref.shape[-1]
    per = hw // HEAD_DIM
    for i, gp_ref in enumerate((gp0_ref, gp1_ref)):
        y = jnp.concatenate(head_outs[i * per:(i + 1) * per], axis=1)
        o_ref[:, i * hw:(i + 1) * hw] = (y * _silu(gp_ref[...].astype(F32))).astype(BF16)


def _ctx_attention(proj, sink, batch, L, q_col, k_col, v_col, gp_col, n_heads, n_kv):
    qw = n_heads * HEAD_DIM
    kw = n_kv * HEAD_DIM
    gw = qw // 2
    has_sink = sink is not None
    in_specs = [
        pl.BlockSpec((L, qw), lambda b: (b, q_col // qw)),
        pl.BlockSpec((L, kw), lambda b: (b, k_col // kw)),
        pl.BlockSpec((L, kw), lambda b: (b, v_col // kw)),
        pl.BlockSpec((L, gw), lambda b: (b, gp_col // gw)),
        pl.BlockSpec((L, gw), lambda b: (b, gp_col // gw + 1)),
    ]
    args = [proj, proj, proj, proj, proj]
    if has_sink:
        in_specs = [pl.BlockSpec(memory_space=pltpu.SMEM)] + in_specs
        args = [sink] + args
    return pl.pallas_call(
        functools.partial(_ctx_attn_kernel, n_heads=n_heads, n_kv=n_kv, has_sink=has_sink),
        grid=(batch,),
        in_specs=in_specs,
        out_specs=pl.BlockSpec((L, qw), lambda b: (b, 0)),
        out_shape=jax.ShapeDtypeStruct((batch * L, qw), BF16),
        compiler_params=_cparams("parallel"),
        name="ctx_attention",
    )(*args)


def _rope(x, cos, sin_signed):
    lane = lax.broadcasted_iota(jnp.int32, x.shape, 1)
    first = (lane % 64) < 32
    swapped = jnp.where(first, pltpu.roll(x, 96, 1), pltpu.roll(x, 32, 1))
    return x * cos + swapped * sin_signed


def _win_attn_kernel(sink_ref, q_ref, kp_ref, kc_ref, kn_ref, vp_ref, vc_ref, vn_ref,
                     cq_ref, sq_ref, cp_ref, sp_ref, cn_ref, sn_ref, mask_ref,
                     ck_ref, cv_ref, gp0_ref, gp1_ref, o_ref, *, nsteps):
    p = pl.program_id(1)
    blk = BAND_BLK
    grp = H_A // KV_A
    cq, sq = cq_ref[...], sq_ref[...]
    lane3 = lax.broadcasted_iota(jnp.int32, (1, 3 * blk), 1)
    edges = []
    for t in range(WIN_QB):
        edge = jnp.zeros((1, 3 * blk), F32)
        if t == 0:
            edge = edge + jnp.where(lane3 < blk, jnp.where(p > 0, 0.0, NEG), 0.0)
        if t == WIN_QB - 1:
            edge = edge + jnp.where(lane3 >= 2 * blk, jnp.where(p < nsteps - 1, 0.0, NEG), 0.0)
        edges.append(edge)
    outs = [[None] * WIN_QB for _ in range(H_A)]
    for kv in range(KV_A):
        sl = slice(kv * HEAD_DIM, (kv + 1) * HEAD_DIM)
        k_all = jnp.concatenate([
            _rope(kp_ref[:, sl].astype(F32), cp_ref[...], sp_ref[...]),
            _rope(kc_ref[:, sl].astype(F32), cq, sq),
            _rope(kn_ref[:, sl].astype(F32), cn_ref[...], sn_ref[...]),
        ], axis=0).astype(BF16)
        v_all = jnp.concatenate([vp_ref[:, sl], vc_ref[:, sl], vn_ref[:, sl]], axis=0).astype(BF16)
        heads = list(range(kv * grp, (kv + 1) * grp))
        q_rot = [_rope(q_ref[:, h * HEAD_DIM:(h + 1) * HEAD_DIM].astype(F32), cq, sq).astype(BF16)
                 for h in heads]
        ck = ck_ref[:, sl]
        cv = cv_ref[:, sl]
        for t in range(WIN_QB):
            rows = slice(t * blk, (t + 1) * blk)
            band = slice(t * blk, (t + 3) * blk)
            qg = jnp.concatenate([q[rows, :] for q in q_rot], axis=0)
            s_band = _dot_nt(qg, k_all[band, :])
            s_ctx = _dot_nt(qg, ck)
            p, sink_w = _softmax_rows(
                [s_band, s_ctx],
                [lambda r0, nr, t=t: mask_ref[r0 % blk:r0 % blk + nr, :] + edges[t], None],
                lambda r0: sink_ref[heads[r0 // blk]] * LOG2E, chunked=True)
            o = _pv_normalised(p, jnp.concatenate([v_all[band, :], cv], axis=0), sink_w)
            for i, h in enumerate(heads):
                outs[h][t] = o[i * blk:(i + 1) * blk, :]
    _store_gated(o_ref, [jnp.concatenate(o, axis=0) for o in outs], gp0_ref, gp1_ref)


def _window_attention(proj, sink, cache_k, cache_v, layer, cos_tab, sin_tab, batch, T):
    blk = BAND_BLK
    nb = T // blk
    qb = WIN_QB
    assert nb % qb == 0
    nsteps = nb // qb
    rq = qb * blk
    qw = H_A * HEAD_DIM
    kw = KV_A * HEAD_DIM
    gw = qw // 2
    P = cache_k.shape[2]

    def prev(p):
        return jnp.maximum(qb * p - 1, 0)

    def nxt(p):
        return jnp.minimum(qb * p + qb, nb - 1)

    kcol, vcol = EV_KA // kw, EV_VA // kw

    def edge(col, f):
        return pl.BlockSpec((blk, kw), lambda b, p: (b * nb + f(p), col))

    def mid(col):
        return pl.BlockSpec((rq, kw), lambda b, p: (b * nsteps + p, col))

    tab_edge = lambda f: pl.BlockSpec((blk, HEAD_DIM), lambda b, p: (f(p), 0))
    tab_mid = pl.BlockSpec((rq, HEAD_DIM), lambda b, p: (p, 0))
    in_specs = [
        pl.BlockSpec(memory_space=pltpu.SMEM),
        pl.BlockSpec((rq, qw), lambda b, p: (b * nsteps + p, EV_QA // qw)),
        edge(kcol, prev), mid(kcol), edge(kcol, nxt),
        edge(vcol, prev), mid(vcol), edge(vcol, nxt),
        tab_mid, tab_mid, tab_edge(prev), tab_edge(prev), tab_edge(nxt), tab_edge(nxt),
        pl.BlockSpec((blk, 3 * blk), lambda b, p: (0, 0)),
        pl.BlockSpec((None, None, P, kw), lambda b, p: (b, layer, 0, 0)),
        pl.BlockSpec((None, None, P, kw), lambda b, p: (b, layer, 0, 0)),
        pl.BlockSpec((rq, gw), lambda b, p: (b * nsteps + p, EV_GPA // gw)),
        pl.BlockSpec((rq, gw), lambda b, p: (b * nsteps + p, EV_GPA // gw + 1)),
    ]
    return pl.pallas_call(
        functools.partial(_win_attn_kernel, nsteps=nsteps),
        grid=(batch, nsteps),
        in_specs=in_specs,
        out_specs=pl.BlockSpec((rq, qw), lambda b, p: (b * nsteps + p, 0)),
        out_shape=jax.ShapeDtypeStruct((batch * T, qw), BF16),
        compiler_params=_cparams("parallel", "parallel"),
        name="window_attention",
    )(sink, proj, proj, proj, proj, proj, proj, proj,
      cos_tab, sin_tab, cos_tab, sin_tab, cos_tab, sin_tab, _band_mask(),
      cache_k, cache_v, proj, proj)


def _band_mask():
    iq = np.arange(BAND_BLK)[:, None]
    jk = np.arange(3 * BAND_BLK)[None, :]
    rel = iq - jk + BAND_BLK
    return jnp.asarray(np.where(np.abs(rel) <= WINDOW, 0.0, NEG), F32)


def _log_sigmoid(x):
    return jnp.minimum(x, 0.0) - jnp.log1p(jnp.exp(-jnp.abs(x)))


def _cumsum_rows(tri, g):
    g_hi = g.astype(BF16)
    r1 = g - g_hi.astype(F32)
    g_mid = r1.astype(BF16)
    g_lo = (r1 - g_mid.astype(F32)).astype(BF16)
    return _dot(tri, g_hi) + _dot(tri, g_mid) + _dot(tri, g_lo)


def _gla_kernel(*refs, nsteps, has_s0):
    fwd_refs, bwd_refs = refs[0:5], refs[5:10]
    wg_ref, bg_ref = refs[10:12]
    if has_s0:
        s0_ref, of_ref, ob_ref, s_scr = refs[12:]
        sfin_ref = None
    else:
        of_ref, ob_ref, sfin_ref, s_scr = refs[13:]
    qf = fwd_refs[0]
    step = pl.program_id(1)
    C = GLA_CHUNK
    G = qf.shape[0] // C
    R = G * C

    @pl.when(step == 0)
    def _():
        if has_s0:
            s_scr[...] = s0_ref[...]
        else:
            s_scr[...] = jnp.zeros(s_scr.shape, F32)

    ri = lax.broadcasted_iota(jnp.int32, (R, R), 0)
    ci = lax.broadcasted_iota(jnp.int32, (R, R), 1)
    same_chunk = (ri // C) == (ci // C)
    hw = H_B * DK_B
    hpv = H_B // 2
    for d, ((q_ref, k_ref, v0_ref, v1_ref, r_ref), o_ref) in enumerate(
            ((fwd_refs, of_ref), (bwd_refs, ob_ref))):
        keep = same_chunk & ((ci <= ri) if d == 0 else (ci >= ri))
        tri = jnp.where(keep, 1.0, 0.0).astype(BF16)
        logit = _dot(r_ref[...].astype(BF16), wg_ref[:, d * hw:(d + 1) * hw]) + bg_ref[:, d * hw:(d + 1) * hw]
        g = _log_sigmoid(logit) / GLA_NORMALIZER
        bc = _cumsum_rows(tri, g)
        edge = C - 1 if d == 0 else 0
        totals = [bc[j * C + edge:j * C + edge + 1, :] for j in range(G)]
        tot_full = jnp.concatenate([jnp.broadcast_to(t, (C, hw)) for t in totals], axis=0)
        e_pos = jnp.exp(bc)
        e_neg = jnp.exp(-bc)
        e_rem = jnp.exp(tot_full - bc)
        e_tot = [jnp.exp(t) for t in totals]
        order = range(G) if d == 0 else range(G - 1, -1, -1)
        for h in range(H_B):
            sl = slice(h * DK_B, (h + 1) * DK_B)
            vl = slice(h * DV_B, (h + 1) * DV_B)
            qh = q_ref[:, sl].astype(F32) * (DK_B ** -0.5)
            kh = k_ref[:, sl].astype(F32)
            v_ref = v0_ref if h < hpv else v1_ref
            vh = v_ref[:, (h % hpv) * DV_B:(h % hpv + 1) * DV_B].astype(BF16)
            qi = (qh * e_pos[:, sl]).astype(BF16)
            ki = (kh * e_neg[:, sl]).astype(BF16)
            ks = kh * e_rem[:, sl]
            att = jnp.where(keep, _dot_nt(qi, ki), 0.0).astype(BF16)
            o_intra = _dot(att, vh)
            S = s_scr[d, h]
            for j in order:
                rows = slice(j * C, (j + 1) * C)
                o_ref[rows, vl] = (o_intra[rows, :] + _dot(qi[rows, :], S.astype(BF16))).astype(o_ref.dtype)
                stacked = jnp.concatenate(
                    [ks[rows, :], jnp.broadcast_to(e_tot[j][:, sl], (C, DK_B))], axis=0)
                tr = stacked.T
                S = tr[:, C:C + 1] * S + _dot(tr[:, 0:C].astype(BF16), vh[rows, :])
            s_scr[d, h] = S

    if sfin_ref is not None:
        @pl.when(step == nsteps - 1)
        def _():
            sfin_ref[...] = s_scr[...]


def _gla(proj, r_proj, wg_full, bg_full, s0, state_buf, layer, batch, T):
    C = GLA_GROUP * GLA_CHUNK
    nc = T // C
    m = batch * T
    qw, vw, rw = H_B * DK_B, H_B * DV_B, 128
    vh = vw // 2

    def fwd(b, c):
        return b * nc + c

    def bwd(b, c):
        return b * nc + (nc - 1 - c)

    def specs(f):
        return [
            pl.BlockSpec((C, qw), lambda b, c: (f(b, c), EV_QB // qw)),
            pl.BlockSpec((C, qw), lambda b, c: (f(b, c), EV_KB // qw)),
            pl.BlockSpec((C, vh), lambda b, c: (f(b, c), EV_VB // vh)),
            pl.BlockSpec((C, vh), lambda b, c: (f(b, c), EV_VB // vh + 1)),
            pl.BlockSpec((C, rw), lambda b, c: (f(b, c), 0)),
        ]

    in_specs = specs(fwd) + specs(bwd) + [
        pl.BlockSpec((rw, 2 * qw), lambda b, c: (0, 0)),
        pl.BlockSpec((1, 2 * qw), lambda b, c: (0, 0)),
    ]
    args = [proj, proj, proj, proj, r_proj] * 2 + [wg_full, bg_full]
    has_s0 = s0 is not None
    state_spec = pl.BlockSpec((None, None, 2, H_B, DK_B, DV_B), lambda b, c: (b, layer, 0, 0, 0, 0))
    out_specs = [pl.BlockSpec((C, vw), lambda b, c: (fwd(b, c), 0)),
                 pl.BlockSpec((C, vw), lambda b, c: (bwd(b, c), 0))]
    out_shape = [jax.ShapeDtypeStruct((m, vw), BF16), jax.ShapeDtypeStruct((m, vw), BF16)]
    aliases = {}
    if has_s0:
        in_specs.append(state_spec)
        args.append(s0)
    else:
        in_specs.append(pl.BlockSpec(memory_space=pl.ANY))
        args.append(state_buf)
        out_specs.append(state_spec)
        out_shape.append(jax.ShapeDtypeStruct(state_buf.shape, F32))
        aliases = {len(args) - 1: 2}
    return pl.pallas_call(
        functools.partial(_gla_kernel, nsteps=nc, has_s0=has_s0),
        grid=(batch, nc),
        in_specs=in_specs,
        out_specs=out_specs,
        out_shape=out_shape,
        input_output_aliases=aliases,
        scratch_shapes=[pltpu.VMEM((2, H_B, DK_B, DV_B), F32)],
        compiler_params=_cparams("arbitrary", "arbitrary"),
        name="gla_scan",
    )(*args)


def _na_bias_kernel(u_ref, o_ref):
    place = pl.program_id(0)
    W = GRID_W
    lane = lax.broadcasted_iota(jnp.int32, (W, 128), 1)
    qcol = lax.broadcasted_iota(jnp.int32, (W, 128), 0)
    kcol = lane % W
    cs = jnp.clip(qcol - NA_KW // 2, 0, W - NA_KW)
    col_ok = (kcol >= cs) & (kcol < cs + NA_KW)
    for t in range(NA_QR):
        first = jnp.where(place == 0, 0, jnp.where(place == 1, t, NA_WR - NA_KH))

        def piece(i, base_shift):
            dr = i - t + (NA_KH - 1) - (NA_WR - NA_KH) * place
            in_rows = (i >= first) & (i < first + NA_KH)
            u = jnp.broadcast_to(u_ref[0, pl.ds(jnp.clip(dr, 0, 2 * NA_KH - 2), 1), :], (W, 128))
            rolled = pltpu.roll(u, base_shift, 1, stride=1, stride_axis=0)
            return rolled + jnp.where(in_rows, 0.0, NEG)

        tiles = []
        for i in range(0, NA_WR, 2):
            tile = jnp.where(lane < W, piece(i, 65), piece(i + 1, 1))
            tiles.append(jnp.where(col_ok, tile, NEG) * LOG2E)
        o_ref[0, 0, t * W:(t + 1) * W, :] = jnp.concatenate(tiles, axis=1)


def _na_bias_expand(u_tab):
    W = GRID_W
    n_tab = u_tab.shape[1]
    return pl.pallas_call(
        _na_bias_kernel,
        grid=(3, H_C),
        in_specs=[pl.BlockSpec((1, n_tab, 128), lambda p, h: (h, 0, 0))],
        out_specs=pl.BlockSpec((1, 1, NA_QR * W, NA_WR * W), lambda p, h: (p, h, 0, 0)),
        out_shape=jax.ShapeDtypeStruct((3, H_C, NA_QR * W, NA_WR * W), F32),
        compiler_params=_cparams("parallel", "parallel"),
        name="na_bias_expand",
    )(u_tab)


def _na_kernel(*refs):
    nblk = NA_WR // NA_QR
    q_ref = refs[0]
    k_refs = refs[1:1 + nblk]
    v_refs = refs[1 + nblk:1 + 2 * nblk]
    bias_ref, ck_ref, cv_ref, gp_ref, o_ref = refs[1 + 2 * nblk:]
    outs = []
    for h in range(H_C):
        sl = slice(h * HEAD_DIM, (h + 1) * HEAD_DIM)
        q = q_ref[:, sl]
        kw = jnp.concatenate([r[:, sl] for r in k_refs], axis=0)
        v_cat = jnp.concatenate([r[:, sl] for r in v_refs] + [cv_ref[:, sl]], axis=0)
        s_nb = _dot_nt(q, kw)
        s_ctx = _dot_nt(q, ck_ref[:, sl])
        p, _ = _softmax_rows([s_nb, s_ctx],
                             [lambda r0, nr, h=h: bias_ref[0, h, r0:r0 + nr, :], None], None,
                             chunked=False)
        outs.append(_pv_normalised(p, v_cat, None))
    y = jnp.concatenate(outs, axis=1)
    o_ref[...] = (y * _silu(gp_ref[...].astype(F32))).astype(BF16)


def _neighbourhood_attention(proj, bias_exp, cache_k, cache_v, layer, batch, T):
    W = GRID_W
    rows = T // W
    assert rows % NA_QR == 0 and rows >= NA_WR and NA_WR % NA_QR == 0
    steps = rows // NA_QR
    nblk = NA_WR // NA_QR
    rq = NA_QR * W
    qw = H_C * HEAD_DIM
    P = cache_k.shape[2]

    def win0(p):
        return jnp.clip(p - 1, 0, steps - nblk)

    def place(p):
        return jnp.where(p == 0, 0, jnp.where(p == steps - 1, 2, 1))

    def win_spec(col, i):
        return pl.BlockSpec((rq, qw), lambda b, p: (b * steps + win0(p) + i, col // qw))

    in_specs = ([pl.BlockSpec((rq, qw), lambda b, p: (b * steps + p, OD_QC // qw))]
                + [win_spec(OD_KC, i) for i in range(nblk)]
                + [win_spec(OD_VC, i) for i in range(nblk)]
                + [pl.BlockSpec((1, H_C, rq, NA_WR * W), lambda b, p: (place(p), 0, 0, 0)),
                   pl.BlockSpec((None, None, P, qw), lambda b, p: (b, layer, 0, 0)),
                   pl.BlockSpec((None, None, P, qw), lambda b, p: (b, layer, 0, 0)),
                   pl.BlockSpec((rq, qw), lambda b, p: (b * steps + p, OD_GPC // qw))])
    return pl.pallas_call(
        _na_kernel,
        grid=(batch, steps),
        in_specs=in_specs,
        out_specs=pl.BlockSpec((rq, qw), lambda b, p: (b * steps + p, 0)),
        out_shape=jax.ShapeDtypeStruct((batch * T, qw), BF16),
        compiler_params=_cparams("parallel", "parallel"),
        name="neighbourhood_attention",
    )(proj, *([proj] * (2 * nblk)), bias_exp, cache_k, cache_v, proj)


def _gmlp_kernel(u_ref, v_ref, gp_ref, g_ref, ws_ref, bt_ref, o_ref, *, chunks):
    gw = u_ref.shape[-1] // G_D
    for ch in range(chunks):
        rs_ = slice(ch * D_CHUNK, (ch + 1) * D_CHUNK)
        v = v_ref[rs_, :].astype(F32)
        mu = jnp.mean(v, axis=-1, keepdims=True)
        var = jnp.mean(jnp.square(v - mu), axis=-1, keepdims=True)
        vn = ((v - mu) * lax.rsqrt(var + 1e-5) * g_ref[...]).astype(BF16)
        for g in range(G_D):
            cl = slice(g * gw, (g + 1) * gw)
            sp = _dot(ws_ref[g], vn[:, cl]) + bt_ref[:, g:g + 1]
            y = u_ref[rs_, cl].astype(F32) * sp
            o_ref[rs_, cl] = (y * _silu(gp_ref[rs_, cl].astype(F32))).astype(BF16)


def _gmlp(proj, gnorm, ws_bf, b_t, tm):
    m = proj.shape[0]
    uw = HALF_W
    return pl.pallas_call(
        functools.partial(_gmlp_kernel, chunks=tm // D_CHUNK),
        grid=(m // tm,),
        in_specs=[
            pl.BlockSpec((tm, uw), lambda i: (i, OD_U // uw)),
            pl.BlockSpec((tm, uw), lambda i: (i, OD_V // uw)),
            pl.BlockSpec((tm, uw), lambda i: (i, OD_GPD // uw)),
            pl.BlockSpec((1, uw), lambda i: (0, 0)),
            pl.BlockSpec(ws_bf.shape, lambda i: (0, 0, 0)),
            pl.BlockSpec(b_t.shape, lambda i: (0, 0)),
        ],
        out_specs=pl.BlockSpec((tm, uw), lambda i: (i, 0)),
        out_shape=jax.ShapeDtypeStruct((m, uw), BF16),
        compiler_params=_cparams("parallel"),
        name="gmlp",
    )(proj, proj, proj, gnorm.reshape(1, uw), ws_bf, b_t)


def _rope_tables(T):
    t = np.arange(T)
    nf = HEAD_DIM // 4
    inv = ROPE_BASE ** (-jnp.arange(nf, dtype=F32) / nf)
    pos = jnp.stack([t // GRID_W, t % GRID_W], axis=-1).astype(F32)
    ang = pos[:, :, None] * inv
    cos, sin = jnp.cos(ang), jnp.sin(ang)
    cos_tab = jnp.concatenate([cos, cos], axis=-1).reshape(T, HEAD_DIM)
    sin_tab = jnp.concatenate([-sin, sin], axis=-1).reshape(T, HEAD_DIM)
    return cos_tab, sin_tab


def kernel(x_prompt, x_sample, cache_attn_k, cache_attn_v, state_gla, cache_na_k, cache_na_v,
           c, c_ctx, ada_w, ada_b, norm_g, w_in_even, w_in_odd, w_out, attn_sink,
           gla_wg, gla_bg, gla_norm_g, na_bias, gmlp_norm_g, gmlp_ws, gmlp_b, final_norm_g):
    B, L, D = x_prompt.shape
    Bd, T, _ = x_sample.shape
    depth = ada_w.shape[0]
    n_even = w_in_even.shape[0]
    n_odd = w_in_odd.shape[0]
    P = cache_attn_k.shape[2]
    Mc, Ml = B * L, Bd * T

    n_rows = 8 * ((1 + Bd + 7) // 8)
    cond = jnp.concatenate([c_ctx[None, :], c, jnp.zeros((n_rows - 1 - Bd, D), F32)], axis=0)
    mod = _modulation(cond, ada_w, ada_b)
    mod = mod.reshape(depth, n_rows, 1, 3 * D)

    tw = 512
    w_even_t = jnp.swapaxes(w_in_even, 1, 2)
    w_even = [_cast_even_t(w_even_t, e) for e in range(n_even)]
    w_rank = [_cast_rows_t(w_even_t, e, EV_RB_SRC, 128) for e in range(n_even)]
    w_odd = [_cast_cols(w_in_odd, o, 0, N_ODD_COLS // tw, tw, q_tiles=H_C * HEAD_DIM // tw)
             for o in range(n_odd)]
    w_o = [_cast_cols(w_out, l, 0, D // tw, tw) for l in range(depth)]
    hw = H_B * DK_B
    wg_full, bg_full = [], []
    for e in range(n_even):
        wg = jnp.zeros((128, 2 * hw), F32)
        wg = wg.at[0:GLA_RANK, 0:hw].set(gla_wg[e, 0])
        wg = wg.at[GLA_RANK:2 * GLA_RANK, hw:2 * hw].set(gla_wg[e, 1])
        wg_full.append(wg.astype(BF16))
        bg_full.append(gla_bg[e].reshape(1, 2 * hw))
    pad_l = 63 - (NA_KW - 1)
    u_tabs = [jnp.pad(na_bias[o], ((0, 0), (0, 0), (pad_l, 128 - pad_l - (2 * NA_KW - 1))), mode="edge")
              for o in range(n_odd)]
    ws_bf = [gmlp_ws[o].astype(BF16) for o in range(n_odd)]
    b_t = [gmlp_b[o].T for o in range(n_odd)]
    cos_tab, sin_tab = _rope_tables(T)
    ck_a = cache_attn_k.reshape(Bd, n_even, P, KV_A * HEAD_DIM).astype(BF16)
    cv_a = cache_attn_v.reshape(Bd, n_even, P, KV_A * HEAD_DIM).astype(BF16)
    ck_c = cache_na_k.reshape(Bd, n_odd, P, H_C * HEAD_DIM).astype(BF16)
    cv_c = cache_na_v.reshape(Bd, n_odd, P, H_C * HEAD_DIM).astype(BF16)

    def run_group(x, is_ctx):
        batch, seq = (B, L) if is_ctx else (Bd, T)
        m = batch * seq
        grp0 = 0 if is_ctx else 1
        rows_per_grp = m if is_ctx else seq
        tm_mm = min(2048, rows_per_grp)
        tm_kv = min(1024, rows_per_grp)
        tm_op = min(512, rows_per_grp)
        side = None
        if is_ctx:
            side = dict(ka=jnp.zeros((B, n_even, L, KV_A * HEAD_DIM), F32),
                        va=jnp.zeros((B, n_even, L, KV_A * HEAD_DIM), F32),
                        st=jnp.zeros((B, n_even, 2, H_B, DK_B, DV_B), F32),
                        kc=jnp.zeros((B, n_odd, L, H_C * HEAD_DIM), F32),
                        vc=jnp.zeros((B, n_odd, L, H_C * HEAD_DIM), F32))
        h = _prenorm(x, norm_g[0], mod[0], grp0, rows_per_grp, tm_op)
        out = None
        for l in range(depth):
            if l % 2 == 0:
                e = l // 2
                r_proj = _in_proj(h, w_rank[e], BF16, tm_mm, 128)
                if is_ctx:
                    proj, side["ka"], side["va"] = _in_proj_kv(
                        h, w_even[e], tm_kv, 512, side["ka"], side["va"], e, EV_KA, EV_VA)
                    ya = _ctx_attention(proj, attn_sink[e], batch, seq, EV_QA, EV_KA, EV_VA, EV_GPA,
                                        H_A, KV_A)
                    o_f, o_b, side["st"] = _gla(proj, r_proj, wg_full[e], bg_full[e], None, side["st"],
                                                e, batch, seq)
                else:
                    proj = _in_proj(h, w_even[e], BF16, tm_mm, 512)
                    ya = _window_attention(proj, attn_sink[e], ck_a, cv_a, e, cos_tab, sin_tab, batch, seq)
                    o_f, o_b = _gla(proj, r_proj, wg_full[e], bg_full[e], state_gla, None, e, batch, seq)
                second = (o_f, o_b, gla_norm_g[e], proj)
            else:
                o = l // 2
                if is_ctx:
                    proj, side["kc"], side["vc"] = _in_proj_kv(
                        h, w_odd[o], tm_kv, 512, side["kc"], side["vc"], o, OD_KC, OD_VC)
                    ya = _ctx_attention(proj, None, batch, seq, OD_QC, OD_KC, OD_VC, OD_GPC, H_C, H_C)
                else:
                    proj = _in_proj(h, w_odd[o], BF16, tm_mm, 1024)
                    bias_exp = _na_bias_expand(u_tabs[o])
                    ya = _neighbourhood_attention(proj, bias_exp, ck_c, cv_c, o, batch, seq)
                second = _gmlp(proj, gmlp_norm_g[o], ws_bf[o], b_t[o], tm_op)
            final = l == depth - 1
            g_next = final_norm_g if final else norm_g[l + 1]
            mod_next = mod[l] if final else mod[l + 1]
            res = _out_proj(ya, second, w_o[l], x, mod[l], g_next, mod_next, grp0, rows_per_grp, tm_op, final)
            if final:
                out = res[0]
            else:
                x, h = res
        return out, side

    y_prompt, side = run_group(x_prompt.reshape(Mc, D), True)
    y_sample, _ = run_group(x_sample.reshape(Ml, D), False)

    return (y_prompt.reshape(B, L, D), y_sample.reshape(Bd, T, D),
            side["ka"].reshape(B, n_even, L, KV_A, HEAD_DIM), side["va"].reshape(B, n_even, L, KV_A, HEAD_DIM),
            side["st"],
            side["kc"].reshape(B, n_odd, L, H_C, HEAD_DIM), side["vc"].reshape(B, n_odd, L, H_C, HEAD_DIM))
```

```python
import functools

import numpy as np
import jax
import jax.numpy as jnp
from jax import lax
from jax.experimental import pallas as pl
from jax.experimental.pallas import tpu as pltpu

F32 = jnp.float32
BF16 = jnp.bfloat16

D_MODEL = 2048
HEAD_DIM = 128
GRID_W = 64
HALF_W = D_MODEL // 2
H_A = 8
KV_A = 2
WINDOW = 128
BAND_BLK = 128
H_B = 4
DK_B = 128
DV_B = 256
GLA_RANK = 16
GLA_NORMALIZER = 16.0
GLA_CHUNK = 64
GLA_GROUP = 4
GLA_STEP_GROUPS = 4
H_C = 8
NA_KH = 8
NA_KW = 16
NA_QR = 4
NA_WR = 12
G_D = 4
D_CHUNK = 128
ROPE_BASE = 10000.0
NEG = -1e30
LOG2E = 1.4426950408889634
Q_FOLD = HEAD_DIM ** -0.5 * LOG2E

EV_QA, EV_KA, EV_VA, EV_QB, EV_KB, EV_VB = 0, 1024, 1280, 1536, 2048, 2560
EV_GP = 3584
EV_GPA, EV_GPB = EV_GP, EV_GP + 1024
EV_RB_SRC = 3584
N_EVEN_COLS = 5632
OUT_TAIL = (256,)
WIN_QB = 2
SM_VREGS = 32
OD_QC, OD_KC, OD_VC, OD_U, OD_V, OD_GPC, OD_GPD = 0, 1024, 2048, 3072, 4096, 5120, 6144
N_ODD_COLS = 7168

VMEM_LIMIT = 52 * 1024 * 1024


def _cparams(*sem):
    return pltpu.CompilerParams(dimension_semantics=sem, vmem_limit_bytes=VMEM_LIMIT)


def _silu(x):
    return x / (1.0 + jnp.exp(-x))


def _dot(a, b):
    return jnp.dot(a, b, preferred_element_type=F32)


def _dot_nt(a, b):
    return lax.dot_general(a, b, (((1,), (1,)), ((), ())), preferred_element_type=F32)


def _rms(x, g, eps=1e-6):
    return x * lax.rsqrt(jnp.mean(x * x, axis=-1, keepdims=True) + eps) * g


def _sm_rows(n_cols):
    return max(16, min(128, (SM_VREGS * 1024 // n_cols) // 16 * 16))


def _softmax_rows(parts, add_fns, extra_fn, chunked):
    m_rows = parts[0].shape[0]
    nr = _sm_rows(sum(s.shape[1] for s in parts)) if chunked else m_rows
    extras, ps = [], []
    for r0 in range(0, m_rows, nr):
        rows = slice(r0, r0 + nr)
        ss = []
        for s, fn in zip(parts, add_fns):
            x = s[rows, :]
            if fn is not None:
                x = x + fn(r0, nr)
            ss.append(x)
        m = functools.reduce(jnp.maximum, [jnp.max(x, axis=-1, keepdims=True) for x in ss])
        if extra_fn is not None:
            m = jnp.maximum(m, extra_fn(r0))
            extras.append(jnp.exp2(extra_fn(r0) - m))
        es = [jnp.exp2((x - m).astype(BF16)) for x in ss]
        ps.append(es[0] if len(es) == 1 else jnp.concatenate(es, axis=1))
    extra_w = jnp.concatenate(extras, axis=0) if extras else None
    return jnp.concatenate(ps, axis=0), extra_w


def _softmax_small(s, extra):
    m = jnp.max(s, axis=-1, keepdims=True)
    if extra is not None:
        m = jnp.maximum(m, extra)
    e = jnp.exp2(s - m)
    den = jnp.sum(e, axis=-1, keepdims=True)
    if extra is not None:
        den = den + jnp.exp2(extra - m)
    return e.astype(BF16), den


def _pv_normalised(p, v, extra_w):
    o2 = _dot(p, jnp.concatenate([v, jnp.ones_like(v)], axis=1))
    den = o2[:, HEAD_DIM:HEAD_DIM + 1]
    if extra_w is not None:
        den = den + extra_w
    return o2[:, 0:HEAD_DIM] / den


def _mod_kernel(cond_ref, w_ref, b_ref, o_ref):
    a = _silu(cond_ref[...]).astype(BF16)
    o_ref[0] = _dot(a, w_ref[0].astype(BF16)) + b_ref[0]


def _modulation(cond, ada_w, ada_b):
    depth, d, n = ada_w.shape
    rows = cond.shape[0]
    tn = 1024
    return pl.pallas_call(
        _mod_kernel,
        grid=(depth, n // tn),
        in_specs=[
            pl.BlockSpec((rows, d), lambda l, j: (0, 0)),
            pl.BlockSpec((1, d, tn), lambda l, j: (l, 0, j)),
            pl.BlockSpec((1, 1, tn), lambda l, j: (l, 0, j)),
        ],
        out_specs=pl.BlockSpec((1, rows, tn), lambda l, j: (l, 0, j)),
        out_shape=jax.ShapeDtypeStruct((depth, rows, n), F32),
        compiler_params=_cparams("parallel", "parallel"),
        name="modulation",
    )(cond, ada_w, ada_b.reshape(depth, 1, n))


def _prenorm_kernel(x_ref, g_ref, mod_ref, h_ref):
    d = x_ref.shape[-1]
    y = _rms(x_ref[...], g_ref[...])
    shift = mod_ref[0, :, 0:d]
    scale = mod_ref[0, :, d:2 * d]
    h_ref[...] = (y * (1.0 + scale) + shift).astype(BF16)


def _prenorm(x, g, mod, grp0, rows_per_grp, tm):
    m, d = x.shape
    tpg = rows_per_grp // tm
    return pl.pallas_call(
        _prenorm_kernel,
        grid=(m // tm,),
        in_specs=[
            pl.BlockSpec((tm, d), lambda i: (i, 0)),
            pl.BlockSpec((1, d), lambda i: (0, 0)),
            pl.BlockSpec((1, 1, 3 * d), lambda i: (grp0 + i // tpg, 0, 0)),
        ],
        out_specs=pl.BlockSpec((tm, d), lambda i: (i, 0)),
        out_shape=jax.ShapeDtypeStruct((m, d), BF16),
        compiler_params=_cparams("parallel"),
        name="prenorm",
    )(x, g.reshape(1, d), mod)


def _cast_kernel(w_ref, o_ref, *, q_tiles):
    scale = jnp.where(pl.program_id(0) < q_tiles, Q_FOLD, 1.0)
    o_ref[...] = (w_ref[...] * scale).astype(BF16)


def _cast_cols(w3, layer, first_tile, n_tiles, tw, q_tiles=0):
    k = w3.shape[1]
    return pl.pallas_call(
        functools.partial(_cast_kernel, q_tiles=q_tiles),
        grid=(n_tiles,),
        in_specs=[pl.BlockSpec((None, k, tw), lambda j: (layer, 0, first_tile + j))],
        out_specs=pl.BlockSpec((k, tw), lambda j: (0, j)),
        out_shape=jax.ShapeDtypeStruct((k, n_tiles * tw), BF16),
        compiler_params=_cparams("parallel"),
        name="weight_cast",
    )(w3)


def _cast_t_kernel(a_ref, b_ref, o_ref, *, q_tiles, gla_q_tile, plain_tiles, skip):
    j = pl.program_id(0)

    @pl.when(j < plain_tiles)
    def _():
        scale = jnp.where(j < q_tiles, Q_FOLD, jnp.where(j == gla_q_tile, DK_B ** -0.5, 1.0))
        o_ref[...] = (a_ref[...] * scale).T.astype(BF16)

    @pl.when(j >= plain_tiles)
    def _():
        x = jnp.concatenate([a_ref[skip:, :], b_ref[...]], axis=0)
        o_ref[...] = x.T.astype(BF16)


def _cast_even_t(w3t, layer):
    n_src, k = w3t.shape[1:]
    tw = 512
    skip = 2 * GLA_RANK
    assert EV_GP % tw == 0 and N_EVEN_COLS % tw == 0 and n_src == N_EVEN_COLS + skip
    return pl.pallas_call(
        functools.partial(_cast_t_kernel, q_tiles=H_A * HEAD_DIM // tw, gla_q_tile=EV_QB // tw,
                          plain_tiles=EV_GP // tw, skip=skip),
        grid=(N_EVEN_COLS // tw,),
        in_specs=[pl.BlockSpec((None, tw, k), lambda j: (layer, j, 0)),
                  pl.BlockSpec((None, skip, k), lambda j: (layer, (tw // skip) * (j + 1), 0))],
        out_specs=pl.BlockSpec((k, tw), lambda j: (0, j)),
        out_shape=jax.ShapeDtypeStruct((k, N_EVEN_COLS), BF16),
        compiler_params=_cparams("parallel"),
        name="weight_cast_even",
    )(w3t, w3t)


def _cast_rows_t_kernel(a_ref, o_ref):
    o_ref[...] = a_ref[...].T.astype(BF16)


def _cast_rows_t(w3t, layer, row0, n_rows):
    k = w3t.shape[2]
    return pl.pallas_call(
        _cast_rows_t_kernel,
        grid=(1,),
        in_specs=[pl.BlockSpec((None, n_rows, k), lambda j: (layer, row0 // n_rows, 0))],
        out_specs=pl.BlockSpec((k, n_rows), lambda j: (0, 0)),
        out_shape=jax.ShapeDtypeStruct((k, n_rows), BF16),
        compiler_params=_cparams("parallel"),
        name="weight_cast_rows",
    )(w3t)


def _matmul_kernel(h_ref, w_ref, o_ref):
    o_ref[...] = _dot(h_ref[...], w_ref[...]).astype(o_ref.dtype)


def _in_proj(h, w, out_dtype, tm, tn):
    m, k = h.shape
    n = w.shape[1]
    return pl.pallas_call(
        _matmul_kernel,
        grid=(m // tm, n // tn),
        in_specs=[
            pl.BlockSpec((tm, k), lambda i, j: (i, 0)),
            pl.BlockSpec((k, tn), lambda i, j: (0, j)),
        ],
        out_specs=pl.BlockSpec((tm, tn), lambda i, j: (i, j)),
        out_shape=jax.ShapeDtypeStruct((m, n), out_dtype),
        compiler_params=_cparams("parallel", "parallel"),
        name="in_proj",
    )(h, w)


def _matmul_kv_kernel(h_ref, w_ref, kin_ref, vin_ref, o_ref, k_ref, v_ref, *, k_col, v_col, width):
    del kin_ref, vin_ref
    tn = w_ref.shape[-1]
    j = pl.program_id(1)
    res = _dot(h_ref[...], w_ref[...])
    o_ref[...] = res.astype(o_ref.dtype)
    for col, ref in ((k_col, k_ref), (v_col, v_ref)):
        for jj in range(col // tn, (col + width - 1) // tn + 1):
            lo, hi = max(col, jj * tn), min(col + width, (jj + 1) * tn)

            @pl.when(j == jj)
            def _(lo=lo, hi=hi, jj=jj, col=col, ref=ref):
                piece = res[:, lo - jj * tn:hi - jj * tn]
                ref[:, :, lo - col:hi - col] = piece.reshape(ref.shape[0], ref.shape[1], hi - lo)


def _in_proj_kv(h, w, tm, tn, k_buf, v_buf, layer, k_col, v_col):
    m, k = h.shape
    n = w.shape[1]
    batch, _, seq, width = k_buf.shape
    assert tm % seq == 0
    kvspec = pl.BlockSpec((tm // seq, None, seq, width), lambda i, j: (i, layer, 0, 0))
    anyspec = pl.BlockSpec(memory_space=pl.ANY)
    return pl.pallas_call(
        functools.partial(_matmul_kv_kernel, k_col=k_col, v_col=v_col, width=width),
        grid=(m // tm, n // tn),
        in_specs=[
            pl.BlockSpec((tm, k), lambda i, j: (i, 0)),
            pl.BlockSpec((k, tn), lambda i, j: (0, j)),
            anyspec, anyspec,
        ],
        out_specs=[pl.BlockSpec((tm, tn), lambda i, j: (i, j)), kvspec, kvspec],
        out_shape=[jax.ShapeDtypeStruct((m, n), F32),
                   jax.ShapeDtypeStruct(k_buf.shape, F32), jax.ShapeDtypeStruct(v_buf.shape, F32)],
        input_output_aliases={2: 1, 3: 2},
        compiler_params=_cparams("arbitrary", "arbitrary"),
        name="in_proj_kv",
    )(h, w, k_buf, v_buf)


def _gla_merge(of_ref, ob_ref, gn_ref, gp0_ref, gp1_ref, rows):
    hpv = H_B // 2
    outs = []
    for h in range(H_B):
        vl = slice(h * DV_B, (h + 1) * DV_B)
        gp_ref = gp0_ref if h < hpv else gp1_ref
        gp = gp_ref[rows, (h % hpv) * DV_B:(h % hpv + 1) * DV_B].astype(F32)
        o = _rms(of_ref[rows, vl].astype(F32) + ob_ref[rows, vl].astype(F32), gn_ref[...])
        outs.append((o * _silu(gp)).astype(BF16))
    return jnp.concatenate(outs, axis=1)


def _outproj_kernel(*refs, final, fuse_gla):
    if fuse_gla:
        ya_ref, of_ref, ob_ref, gn_ref, gp0_ref, gp1_ref = refs[:6]
        refs = refs[6:]
    else:
        ya_ref, yb_ref = refs[:2]
        refs = refs[2:]
    w_ref, x_ref, mod_ref, g_ref, modn_ref = refs[:5]
    out_refs = refs[5:]
    d = x_ref.shape[-1]
    half = ya_ref.shape[-1]
    tm = x_ref.shape[0]
    gate = mod_ref[0, :, 2 * d:3 * d]
    bounds = [0] + [tm - s for s in OUT_TAIL if s < tm] + [tm]
    for r0, r1 in zip(bounds[:-1], bounds[1:]):
        rows = slice(r0, r1)
        if fuse_gla:
            yb = _gla_merge(of_ref, ob_ref, gn_ref, gp0_ref, gp1_ref, rows)
        else:
            yb = yb_ref[rows, :]
        acc = _dot(ya_ref[rows, :], w_ref[0:half, :]) + _dot(yb, w_ref[half:2 * half, :])
        xn = x_ref[rows, :] + gate * acc
        if final:
            out_refs[0][rows, :] = _rms(xn, g_ref[...])
        else:
            out_refs[0][rows, :] = xn
            shift = modn_ref[0, :, 0:d]
            scale = modn_ref[0, :, d:2 * d]
            out_refs[1][rows, :] = (_rms(xn, g_ref[...]) * (1.0 + scale) + shift).astype(BF16)


def _out_proj(ya, second, w, x, mod, g_next, mod_next, grp0, rows_per_grp, tm, final):
    m, d = x.shape
    half = ya.shape[1]
    tpg = rows_per_grp // tm
    modspec = pl.BlockSpec((1, 1, 3 * d), lambda i: (grp0 + i // tpg, 0, 0))
    rowspec = pl.BlockSpec((tm, d), lambda i: (i, 0))
    halfspec = pl.BlockSpec((tm, half), lambda i: (i, 0))
    fuse_gla = isinstance(second, tuple)
    if fuse_gla:
        o_f, o_b, gnorm, proj = second
        gw = half // 2
        mix_specs = [halfspec, halfspec, halfspec,
                     pl.BlockSpec((1, DV_B), lambda i: (0, 0)),
                     pl.BlockSpec((tm, gw), lambda i: (i, EV_GPB // gw)),
                     pl.BlockSpec((tm, gw), lambda i: (i, EV_GPB // gw + 1))]
        mix_args = [ya, o_f, o_b, gnorm.reshape(1, DV_B), proj, proj]
    else:
        mix_specs = [halfspec, halfspec]
        mix_args = [ya, second]
    if final:
        out_shape = [jax.ShapeDtypeStruct((m, d), F32)]
        out_specs = [rowspec]
    else:
        out_shape = [jax.ShapeDtypeStruct((m, d), F32), jax.ShapeDtypeStruct((m, d), BF16)]
        out_specs = [rowspec, rowspec]
    return pl.pallas_call(
        functools.partial(_outproj_kernel, final=final, fuse_gla=fuse_gla),
        grid=(m // tm,),
        in_specs=mix_specs + [
            pl.BlockSpec((2 * half, d), lambda i: (0, 0)),
            rowspec, modspec,
            pl.BlockSpec((1, d), lambda i: (0, 0)),
            modspec,
        ],
        out_specs=out_specs,
        out_shape=out_shape,
        compiler_params=_cparams("parallel"),
        name="out_proj",
    )(*mix_args, w, x, mod, g_next.reshape(1, d), mod_next)


def _ctx_attn_kernel(*refs, n_heads, n_kv, has_sink):
    if has_sink:
        sink_ref, q_ref, k_ref, v_ref, gp0_ref, gp1_ref, o_ref = refs
    else:
        q_ref, k_ref, v_ref, gp0_ref, gp1_ref, o_ref = refs
    L = q_ref.shape[0]
    grp = n_heads // n_kv
    outs = [None] * n_heads
    for kv in range(n_kv):
        heads = list(range(kv * grp, (kv + 1) * grp))
        qg = jnp.concatenate(
            [q_ref[:, h * HEAD_DIM:(h + 1) * HEAD_DIM].astype(BF16) for h in heads], axis=0)
        k = k_ref[:, kv * HEAD_DIM:(kv + 1) * HEAD_DIM].astype(BF16)
        v = v_ref[:, kv * HEAD_DIM:(kv + 1) * HEAD_DIM].astype(BF16)
        s = _dot_nt(qg, k)
        sk = None
        if has_sink:
            sk = jnp.concatenate([jnp.full((L, 1), sink_ref[h] * LOG2E, F32) for h in heads], axis=0)
        p, den = _softmax_small(s, sk)
        o = _dot(p, v) / den
        for i, h in enumerate(heads):
            outs[h] = o[i * L:(i + 1) * L, :]
    _store_gated(o_ref, outs, gp0_ref, gp1_ref)


def _store_gated(o_ref, head_outs, gp0_ref, gp1_ref):
    hw = gp0_ref.shape[-1]
    per = hw // HEAD_DIM
    for i, gp_ref in enumerate((gp0_ref, gp1_ref)):
        y = jnp.concatenate(head_outs[i * per:(i + 1) * per], axis=1)
        o_ref[:, i * hw:(i + 1) * hw] = (y * _silu(gp_ref[...].astype(F32))).astype(BF16)


def _ctx_attention(proj, sink, batch, L, q_col, k_col, v_col, gp_col, n_heads, n_kv):
    qw = n_heads * HEAD_DIM
    kw = n_kv * HEAD_DIM
    gw = qw // 2
    has_sink = sink is not None
    in_specs = [
        pl.BlockSpec((L, qw), lambda b: (b, q_col // qw)),
        pl.BlockSpec((L, kw), lambda b: (b, k_col // kw)),
        pl.BlockSpec((L, kw), lambda b: (b, v_col // kw)),
        pl.BlockSpec((L, gw), lambda b: (b, gp_col // gw)),
        pl.BlockSpec((L, gw), lambda b: (b, gp_col // gw + 1)),
    ]
    args = [proj, proj, proj, proj, proj]
    if has_sink:
        in_specs = [pl.BlockSpec(memory_space=pltpu.SMEM)] + in_specs
        args = [sink] + args
    return pl.pallas_call(
        functools.partial(_ctx_attn_kernel, n_heads=n_heads, n_kv=n_kv, has_sink=has_sink),
        grid=(batch,),
        in_specs=in_specs,
        out_specs=pl.BlockSpec((L, qw), lambda b: (b, 0)),
        out_shape=jax.ShapeDtypeStruct((batch * L, qw), BF16),
        compiler_params=_cparams("parallel"),
        name="ctx_attention",
    )(*args)


def _rope(x, cos, sin_signed):
    lane = lax.broadcasted_iota(jnp.int32, x.shape, 1)
    first = (lane % 64) < 32
    swapped = jnp.where(first, pltpu.roll(x, 96, 1), pltpu.roll(x, 32, 1))
    return x * cos + swapped * sin_signed


def _win_attn_kernel(sink_ref, q_ref, kp_ref, kc_ref, kn_ref, vp_ref, vc_ref, vn_ref,
                     cq_ref, sq_ref, cp_ref, sp_ref, cn_ref, sn_ref, mask_ref,
                     ck_ref, cv_ref, gp0_ref, gp1_ref, o_ref, *, nsteps):
    p = pl.program_id(1)
    blk = BAND_BLK
    grp = H_A // KV_A
    cq, sq = cq_ref[...], sq_ref[...]
    lane3 = lax.broadcasted_iota(jnp.int32, (1, 3 * blk), 1)
    edges = []
    for t in range(WIN_QB):
        edge = jnp.zeros((1, 3 * blk), F32)
        if t == 0:
            edge = edge + jnp.where(lane3 < blk, jnp.where(p > 0, 0.0, NEG), 0.0)
        if t == WIN_QB - 1:
            edge = edge + jnp.where(lane3 >= 2 * blk, jnp.where(p < nsteps - 1, 0.0, NEG), 0.0)
        edges.append(edge)
    outs = [[None] * WIN_QB for _ in range(H_A)]
    for kv in range(KV_A):
        sl = slice(kv * HEAD_DIM, (kv + 1) * HEAD_DIM)
        k_all = jnp.concatenate([
            _rope(kp_ref[:, sl].astype(F32), cp_ref[...], sp_ref[...]),
            _rope(kc_ref[:, sl].astype(F32), cq, sq),
            _rope(kn_ref[:, sl].astype(F32), cn_ref[...], sn_ref[...]),
        ], axis=0).astype(BF16)
        v_all = jnp.concatenate([vp_ref[:, sl], vc_ref[:, sl], vn_ref[:, sl]], axis=0).astype(BF16)
        heads = list(range(kv * grp, (kv + 1) * grp))
        q_rot = [_rope(q_ref[:, h * HEAD_DIM:(h + 1) * HEAD_DIM].astype(F32), cq, sq).astype(BF16)
                 for h in heads]
        ck = ck_ref[:, sl]
        cv = cv_ref[:, sl]
        for t in range(WIN_QB):
            rows = slice(t * blk, (t + 1) * blk)
            band = slice(t * blk, (t + 3) * blk)
            qg = jnp.concatenate([q[rows, :] for q in q_rot], axis=0)
            s_band = _dot_nt(qg, k_all[band, :])
            s_ctx = _dot_nt(qg, ck)
            p, sink_w = _softmax_rows(
                [s_band, s_ctx],
                [lambda r0, nr, t=t: mask_ref[r0 % blk:r0 % blk + nr, :] + edges[t], None],
                lambda r0: sink_ref[heads[r0 // blk]] * LOG2E, chunked=True)
            o = _pv_normalised(p, jnp.concatenate([v_all[band, :], cv], axis=0), sink_w)
            for i, h in enumerate(heads):
                outs[h][t] = o[i * blk:(i + 1) * blk, :]
    _store_gated(o_ref, [jnp.concatenate(o, axis=0) for o in outs], gp0_ref, gp1_ref)


def _window_attention(proj, sink, cache_k, cache_v, layer, cos_tab, sin_tab, batch, T):
    blk = BAND_BLK
    nb = T // blk
    qb = WIN_QB
    assert nb % qb == 0
    nsteps = nb // qb
    rq = qb * blk
    qw = H_A * HEAD_DIM
    kw = KV_A * HEAD_DIM
    gw = qw // 2
    P = cache_k.shape[2]

    def prev(p):
        return jnp.maximum(qb * p - 1, 0)

    def nxt(p):
        return jnp.minimum(qb * p + qb, nb - 1)

    kcol, vcol = EV_KA // kw, EV_VA // kw

    def edge(col, f):
        return pl.BlockSpec((blk, kw), lambda b, p: (b * nb + f(p), col))

    def mid(col):
        return pl.BlockSpec((rq, kw), lambda b, p: (b * nsteps + p, col))

    tab_edge = lambda f: pl.BlockSpec((blk, HEAD_DIM), lambda b, p: (f(p), 0))
    tab_mid = pl.BlockSpec((rq, HEAD_DIM), lambda b, p: (p, 0))
    in_specs = [
        pl.BlockSpec(memory_space=pltpu.SMEM),
        pl.BlockSpec((rq, qw), lambda b, p: (b * nsteps + p, EV_QA // qw)),
        edge(kcol, prev), mid(kcol), edge(kcol, nxt),
        edge(vcol, prev), mid(vcol), edge(vcol, nxt),
        tab_mid, tab_mid, tab_edge(prev), tab_edge(prev), tab_edge(nxt), tab_edge(nxt),
        pl.BlockSpec((blk, 3 * blk), lambda b, p: (0, 0)),
        pl.BlockSpec((None, None, P, kw), lambda b, p: (b, layer, 0, 0)),
        pl.BlockSpec((None, None, P, kw), lambda b, p: (b, layer, 0, 0)),
        pl.BlockSpec((rq, gw), lambda b, p: (b * nsteps + p, EV_GPA // gw)),
        pl.BlockSpec((rq, gw), lambda b, p: (b * nsteps + p, EV_GPA // gw + 1)),
    ]
    return pl.pallas_call(
        functools.partial(_win_attn_kernel, nsteps=nsteps),
        grid=(batch, nsteps),
        in_specs=in_specs,
        out_specs=pl.BlockSpec((rq, qw), lambda b, p: (b * nsteps + p, 0)),
        out_shape=jax.ShapeDtypeStruct((batch * T, qw), BF16),
        compiler_params=_cparams("parallel", "parallel"),
        name="window_attention",
    )(sink, proj, proj, proj, proj, proj, proj, proj,
      cos_tab, sin_tab, cos_tab, sin_tab, cos_tab, sin_tab, _band_mask(),
      cache_k, cache_v, proj, proj)


def _band_mask():
    iq = np.arange(BAND_BLK)[:, None]
    jk = np.arange(3 * BAND_BLK)[None, :]
    rel = iq - jk + BAND_BLK
    return jnp.asarray(np.where(np.abs(rel) <= WINDOW, 0.0, NEG), F32)


def _log_sigmoid(x):
    return jnp.minimum(x, 0.0) - jnp.log1p(jnp.exp(-jnp.abs(x)))


def _cumsum_rows(tri, g):
    g_hi = g.astype(BF16)
    r1 = g - g_hi.astype(F32)
    g_mid = r1.astype(BF16)
    g_lo = (r1 - g_mid.astype(F32)).astype(BF16)
    return _dot(tri, g_hi) + (_dot(tri, g_mid) + _dot(tri, g_lo))


def _gla_kernel(*refs, nsteps, has_s0):
    fwd_refs, bwd_refs = refs[0:5], refs[5:10]
    wg_ref, bg_ref = refs[10:12]
    if has_s0:
        s0_ref, of_ref, ob_ref, s_scr = refs[12:]
        sfin_ref = None
    else:
        of_ref, ob_ref, sfin_ref, s_scr = refs[13:]
    qf = fwd_refs[0]
    step = pl.program_id(1)
    C = GLA_CHUNK
    G = GLA_GROUP
    R = G * C
    n_sub = qf.shape[0] // R

    @pl.when(step == 0)
    def _():
        if has_s0:
            s_scr[...] = s0_ref[...]
        else:
            s_scr[...] = jnp.zeros(s_scr.shape, F32)

    ri = lax.broadcasted_iota(jnp.int32, (R, R), 0)
    ci = lax.broadcasted_iota(jnp.int32, (R, R), 1)
    same_chunk = (ri // C) == (ci // C)
    hw = H_B * DK_B
    hpv = H_B // 2
    for d, ((q_ref, k_ref, v0_ref, v1_ref, r_ref), o_ref) in enumerate(
            ((fwd_refs, of_ref), (bwd_refs, ob_ref))):
        keep = same_chunk & ((ci <= ri) if d == 0 else (ci >= ri))
        tri = jnp.where(keep, 1.0, 0.0).astype(BF16)
        edge = C - 1 if d == 0 else 0
        order = range(G) if d == 0 else range(G - 1, -1, -1)
        states = [s_scr[d, h] for h in range(H_B)]
        for sub in (range(n_sub) if d == 0 else range(n_sub - 1, -1, -1)):
            blk = slice(sub * R, (sub + 1) * R)
            logit = (_dot(r_ref[blk, :].astype(BF16), wg_ref[:, d * hw:(d + 1) * hw])
                     + bg_ref[:, d * hw:(d + 1) * hw])
            g = _log_sigmoid(logit) / GLA_NORMALIZER
            bc = _cumsum_rows(tri, g)
            totals = [bc[j * C + edge:j * C + edge + 1, :] for j in range(G)]
            tot_full = jnp.concatenate([jnp.broadcast_to(t, (C, hw)) for t in totals], axis=0)
            e_pos = jnp.exp(bc)
            e_neg = jnp.exp(-bc)
            e_rem = jnp.exp(tot_full - bc)
            e_tot = [jnp.exp(t) for t in totals]
            for h in range(H_B):
                sl = slice(h * DK_B, (h + 1) * DK_B)
                vl = slice(h * DV_B, (h + 1) * DV_B)
                qh = q_ref[blk, sl].astype(F32)
                kh = k_ref[blk, sl].astype(F32)
                v_ref = v0_ref if h < hpv else v1_ref
                vh = v_ref[blk, (h % hpv) * DV_B:(h % hpv + 1) * DV_B].astype(BF16)
                qi = (qh * e_pos[:, sl]).astype(BF16)
                ki = (kh * e_neg[:, sl]).astype(BF16)
                ks = kh * e_rem[:, sl]
                att = jnp.where(keep, _dot_nt(qi, ki), 0.0).astype(BF16)
                o_intra = _dot(att, vh)
                S = states[h]
                for j in order:
                    rows = slice(j * C, (j + 1) * C)
                    out_rows = slice(sub * R + j * C, sub * R + (j + 1) * C)
                    o_ref[out_rows, vl] = (o_intra[rows, :]
                                           + _dot(qi[rows, :], S.astype(BF16))).astype(o_ref.dtype)
                    stacked = jnp.concatenate(
                        [ks[rows, :], jnp.broadcast_to(e_tot[j][:, sl], (C, DK_B))], axis=0)
                    tr = stacked.T
                    S = tr[:, C:C + 1] * S + _dot(tr[:, 0:C].astype(BF16), vh[rows, :])
                states[h] = S
        for h in range(H_B):
            s_scr[d, h] = states[h]

    if sfin_ref is not None:
        @pl.when(step == nsteps - 1)
        def _():
            sfin_ref[...] = s_scr[...]


def _gla(proj, r_proj, wg_full, bg_full, s0, state_buf, layer, batch, T):
    group_rows = GLA_GROUP * GLA_CHUNK
    C = group_rows * max(1, min(GLA_STEP_GROUPS, T // group_rows))
    assert T % C == 0
    nc = T // C
    m = batch * T
    qw, vw, rw = H_B * DK_B, H_B * DV_B, 128
    vh = vw // 2

    def fwd(b, c):
        return b * nc + c

    def bwd(b, c):
        return b * nc + (nc - 1 - c)

    def specs(f):
        return [
            pl.BlockSpec((C, qw), lambda b, c: (f(b, c), EV_QB // qw)),
            pl.BlockSpec((C, qw), lambda b, c: (f(b, c), EV_KB // qw)),
            pl.BlockSpec((C, vh), lambda b, c: (f(b, c), EV_VB // vh)),
            pl.BlockSpec((C, vh), lambda b, c: (f(b, c), EV_VB // vh + 1)),
            pl.BlockSpec((C, rw), lambda b, c: (f(b, c), 0)),
        ]

    in_specs = specs(fwd) + specs(bwd) + [
        pl.BlockSpec((rw, 2 * qw), lambda b, c: (0, 0)),
        pl.BlockSpec((1, 2 * qw), lambda b, c: (0, 0)),
    ]
    args = [proj, proj, proj, proj, r_proj] * 2 + [wg_full, bg_full]
    has_s0 = s0 is not None
    state_spec = pl.BlockSpec((None, None, 2, H_B, DK_B, DV_B), lambda b, c: (b, layer, 0, 0, 0, 0))
    out_specs = [pl.BlockSpec((C, vw), lambda b, c: (fwd(b, c), 0)),
                 pl.BlockSpec((C, vw), lambda b, c: (bwd(b, c), 0))]
    out_shape = [jax.ShapeDtypeStruct((m, vw), BF16), jax.ShapeDtypeStruct((m, vw), BF16)]
    aliases = {}
    if has_s0:
        in_specs.append(state_spec)
        args.append(s0)
    else:
        in_specs.append(pl.BlockSpec(memory_space=pl.ANY))
        args.append(state_buf)
        out_specs.append(state_spec)
        out_shape.append(jax.ShapeDtypeStruct(state_buf.shape, F32))
        aliases = {len(args) - 1: 2}
    return pl.pallas_call(
        functools.partial(_gla_kernel, nsteps=nc, has_s0=has_s0),
        grid=(batch, nc),
        in_specs=in_specs,
        out_specs=out_specs,
        out_shape=out_shape,
        input_output_aliases=aliases,
        scratch_shapes=[pltpu.VMEM((2, H_B, DK_B, DV_B), F32)],
        compiler_params=_cparams("arbitrary", "arbitrary"),
        name="gla_scan",
    )(*args)


def _na_bias_kernel(u_ref, o_ref):
    place = pl.program_id(0)
    W = GRID_W
    lane = lax.broadcasted_iota(jnp.int32, (W, 128), 1)
    qcol = lax.broadcasted_iota(jnp.int32, (W, 128), 0)
    kcol = lane % W
    cs = jnp.clip(qcol - NA_KW // 2, 0, W - NA_KW)
    col_ok = (kcol >= cs) & (kcol < cs + NA_KW)
    for t in range(NA_QR):
        first = jnp.where(place == 0, 0, jnp.where(place == 1, t, NA_WR - NA_KH))

        def piece(i, base_shift):
            dr = i - t + (NA_KH - 1) - (NA_WR - NA_KH) * place
            in_rows = (i >= first) & (i < first + NA_KH)
            u = jnp.broadcast_to(u_ref[0, pl.ds(jnp.clip(dr, 0, 2 * NA_KH - 2), 1), :], (W, 128))
            rolled = pltpu.roll(u, base_shift, 1, stride=1, stride_axis=0)
            return rolled + jnp.where(in_rows, 0.0, NEG)

        tiles = []
        for i in range(0, NA_WR, 2):
            tile = jnp.where(lane < W, piece(i, 65), piece(i + 1, 1))
            tiles.append(jnp.where(col_ok, tile, NEG) * LOG2E)
        o_ref[0, 0, t * W:(t + 1) * W, :] = jnp.concatenate(tiles, axis=1)


def _na_bias_expand(u_tab):
    W = GRID_W
    n_tab = u_tab.shape[1]
    return pl.pallas_call(
        _na_bias_kernel,
        grid=(3, H_C),
        in_specs=[pl.BlockSpec((1, n_tab, 128), lambda p, h: (h, 0, 0))],
        out_specs=pl.BlockSpec((1, 1, NA_QR * W, NA_WR * W), lambda p, h: (p, h, 0, 0)),
        out_shape=jax.ShapeDtypeStruct((3, H_C, NA_QR * W, NA_WR * W), F32),
        compiler_params=_cparams("parallel", "parallel"),
        name="na_bias_expand",
    )(u_tab)


def _na_kernel(*refs):
    nblk = NA_WR // NA_QR
    q_ref = refs[0]
    k_refs = refs[1:1 + nblk]
    v_refs = refs[1 + nblk:1 + 2 * nblk]
    bias_ref, ck_ref, cv_ref, gp_ref, o_ref = refs[1 + 2 * nblk:]
    outs = []
    for h in range(H_C):
        sl = slice(h * HEAD_DIM, (h + 1) * HEAD_DIM)
        q = q_ref[:, sl]
        kw = jnp.concatenate([r[:, sl] for r in k_refs], axis=0)
        v_cat = jnp.concatenate([r[:, sl] for r in v_refs] + [cv_ref[:, sl]], axis=0)
        s_nb = _dot_nt(q, kw)
        s_ctx = _dot_nt(q, ck_ref[:, sl])
        p, _ = _softmax_rows([s_nb, s_ctx],
                             [lambda r0, nr, h=h: bias_ref[0, h, r0:r0 + nr, :], None], None,
                             chunked=False)
        outs.append(_pv_normalised(p, v_cat, None))
    y = jnp.concatenate(outs, axis=1)
    o_ref[...] = (y * _silu(gp_ref[...].astype(F32))).astype(BF16)


def _neighbourhood_attention(proj, bias_exp, cache_k, cache_v, layer, batch, T):
    W = GRID_W
    rows = T // W
    assert rows % NA_QR == 0 and rows >= NA_WR and NA_WR % NA_QR == 0
    steps = rows // NA_QR
    nblk = NA_WR // NA_QR
    rq = NA_QR * W
    qw = H_C * HEAD_DIM
    P = cache_k.shape[2]

    def win0(p):
        return jnp.clip(p - 1, 0, steps - nblk)

    def place(p):
        return jnp.where(p == 0, 0, jnp.where(p == steps - 1, 2, 1))

    def win_spec(col, i):
        return pl.BlockSpec((rq, qw), lambda b, p: (b * steps + win0(p) + i, col // qw))

    in_specs = ([pl.BlockSpec((rq, qw), lambda b, p: (b * steps + p, OD_QC // qw))]
                + [win_spec(OD_KC, i) for i in range(nblk)]
                + [win_spec(OD_VC, i) for i in range(nblk)]
                + [pl.BlockSpec((1, H_C, rq, NA_WR * W), lambda b, p: (place(p), 0, 0, 0)),
                   pl.BlockSpec((None, None, P, qw), lambda b, p: (b, layer, 0, 0)),
                   pl.BlockSpec((None, None, P, qw), lambda b, p: (b, layer, 0, 0)),
                   pl.BlockSpec((rq, qw), lambda b, p: (b * steps + p, OD_GPC // qw))])
    return pl.pallas_call(
        _na_kernel,
        grid=(batch, steps),
        in_specs=in_specs,
        out_specs=pl.BlockSpec((rq, qw), lambda b, p: (b * steps + p, 0)),
        out_shape=jax.ShapeDtypeStruct((batch * T, qw), BF16),
        compiler_params=_cparams("parallel", "parallel"),
        name="neighbourhood_attention",
    )(proj, *([proj] * (2 * nblk)), bias_exp, cache_k, cache_v, proj)


def _gmlp_kernel(u_ref, v_ref, gp_ref, g_ref, ws_ref, bt_ref, o_ref, *, chunks):
    gw = u_ref.shape[-1] // G_D
    for ch in range(chunks):
        rs_ = slice(ch * D_CHUNK, (ch + 1) * D_CHUNK)
        v = v_ref[rs_, :].astype(F32)
        mu = jnp.mean(v, axis=-1, keepdims=True)
        var = jnp.mean(jnp.square(v - mu), axis=-1, keepdims=True)
        vn = ((v - mu) * lax.rsqrt(var + 1e-5) * g_ref[...]).astype(BF16)
        for g in range(G_D):
            cl = slice(g * gw, (g + 1) * gw)
            sp = _dot(ws_ref[g], vn[:, cl]) + bt_ref[:, g:g + 1]
            y = u_ref[rs_, cl].astype(F32) * sp
            o_ref[rs_, cl] = (y * _silu(gp_ref[rs_, cl].astype(F32))).astype(BF16)


def _gmlp(proj, gnorm, ws_bf, b_t, tm):
    m = proj.shape[0]
    uw = HALF_W
    return pl.pallas_call(
        functools.partial(_gmlp_kernel, chunks=tm // D_CHUNK),
        grid=(m // tm,),
        in_specs=[
            pl.BlockSpec((tm, uw), lambda i: (i, OD_U // uw)),
            pl.BlockSpec((tm, uw), lambda i: (i, OD_V // uw)),
            pl.BlockSpec((tm, uw), lambda i: (i, OD_GPD // uw)),
            pl.BlockSpec((1, uw), lambda i: (0, 0)),
            pl.BlockSpec(ws_bf.shape, lambda i: (0, 0, 0)),
            pl.BlockSpec(b_t.shape, lambda i: (0, 0)),
        ],
        out_specs=pl.BlockSpec((tm, uw), lambda i: (i, 0)),
        out_shape=jax.ShapeDtypeStruct((m, uw), BF16),
        compiler_params=_cparams("parallel"),
        name="gmlp",
    )(proj, proj, proj, gnorm.reshape(1, uw), ws_bf, b_t)


def _rope_tables(T):
    t = np.arange(T)
    nf = HEAD_DIM // 4
    inv = ROPE_BASE ** (-jnp.arange(nf, dtype=F32) / nf)
    pos = jnp.stack([t // GRID_W, t % GRID_W], axis=-1).astype(F32)
    ang = pos[:, :, None] * inv
    cos, sin = jnp.cos(ang), jnp.sin(ang)
    cos_tab = jnp.concatenate([cos, cos], axis=-1).reshape(T, HEAD_DIM)
    sin_tab = jnp.concatenate([-sin, sin], axis=-1).reshape(T, HEAD_DIM)
    return cos_tab, sin_tab


def kernel(x_prompt, x_sample, cache_attn_k, cache_attn_v, state_gla, cache_na_k, cache_na_v,
           c, c_ctx, ada_w, ada_b, norm_g, w_in_even, w_in_odd, w_out, attn_sink,
           gla_wg, gla_bg, gla_norm_g, na_bias, gmlp_norm_g, gmlp_ws, gmlp_b, final_norm_g):
    B, L, D = x_prompt.shape
    Bd, T, _ = x_sample.shape
    depth = ada_w.shape[0]
    n_even = w_in_even.shape[0]
    n_odd = w_in_odd.shape[0]
    P = cache_attn_k.shape[2]
    Mc, Ml = B * L, Bd * T

    n_rows = 8 * ((1 + Bd + 7) // 8)
    cond = jnp.concatenate([c_ctx[None, :], c, jnp.zeros((n_rows - 1 - Bd, D), F32)], axis=0)
    mod = _modulation(cond, ada_w, ada_b)
    mod = mod.reshape(depth, n_rows, 1, 3 * D)

    tw = 512
    w_even_t = jnp.swapaxes(w_in_even, 1, 2)
    w_even = [_cast_even_t(w_even_t, e) for e in range(n_even)]
    w_rank = [_cast_rows_t(w_even_t, e, EV_RB_SRC, 128) for e in range(n_even)]
    w_odd = [_cast_cols(w_in_odd, o, 0, N_ODD_COLS // tw, tw, q_tiles=H_C * HEAD_DIM // tw)
             for o in range(n_odd)]
    w_o = [_cast_cols(w_out, l, 0, D // tw, tw) for l in range(depth)]
    hw = H_B * DK_B
    wg_full, bg_full = [], []
    for e in range(n_even):
        wg = jnp.zeros((128, 2 * hw), F32)
        wg = wg.at[0:GLA_RANK, 0:hw].set(gla_wg[e, 0])
        wg = wg.at[GLA_RANK:2 * GLA_RANK, hw:2 * hw].set(gla_wg[e, 1])
        wg_full.append(wg.astype(BF16))
        bg_full.append(gla_bg[e].reshape(1, 2 * hw))
    pad_l = 63 - (NA_KW - 1)
    u_tabs = [jnp.pad(na_bias[o], ((0, 0), (0, 0), (pad_l, 128 - pad_l - (2 * NA_KW - 1))), mode="edge")
              for o in range(n_odd)]
    ws_bf = [gmlp_ws[o].astype(BF16) for o in range(n_odd)]
    b_t = [gmlp_b[o].T for o in range(n_odd)]
    cos_tab, sin_tab = _rope_tables(T)
    ck_a = cache_attn_k.reshape(Bd, n_even, P, KV_A * HEAD_DIM).astype(BF16)
    cv_a = cache_attn_v.reshape(Bd, n_even, P, KV_A * HEAD_DIM).astype(BF16)
    ck_c = cache_na_k.reshape(Bd, n_odd, P, H_C * HEAD_DIM).astype(BF16)
    cv_c = cache_na_v.reshape(Bd, n_odd, P, H_C * HEAD_DIM).astype(BF16)

    def run_group(x, is_ctx):
        batch, seq = (B, L) if is_ctx else (Bd, T)
        m = batch * seq
        grp0 = 0 if is_ctx else 1
        rows_per_grp = m if is_ctx else seq
        tm_mm = min(2048, rows_per_grp)
        tm_kv = min(1024, rows_per_grp)
        tm_op = min(512, rows_per_grp)
        side = None
        if is_ctx:
            side = dict(ka=jnp.zeros((B, n_even, L, KV_A * HEAD_DIM), F32),
                        va=jnp.zeros((B, n_even, L, KV_A * HEAD_DIM), F32),
                        st=jnp.zeros((B, n_even, 2, H_B, DK_B, DV_B), F32),
                        kc=jnp.zeros((B, n_odd, L, H_C * HEAD_DIM), F32),
                        vc=jnp.zeros((B, n_odd, L, H_C * HEAD_DIM), F32))
        h = _prenorm(x, norm_g[0], mod[0], grp0, rows_per_grp, tm_op)
        out = None
        for l in range(depth):
            if l % 2 == 0:
                e = l // 2
                r_proj = _in_proj(h, w_rank[e], BF16, tm_mm, 128)
                if is_ctx:
                    proj, side["ka"], side["va"] = _in_proj_kv(
                        h, w_even[e], tm_kv, 512, side["ka"], side["va"], e, EV_KA, EV_VA)
                    ya = _ctx_attention(proj, attn_sink[e], batch, seq, EV_QA, EV_KA, EV_VA, EV_GPA,
                                        H_A, KV_A)
                    o_f, o_b, side["st"] = _gla(proj, r_proj, wg_full[e], bg_full[e], None, side["st"],
                                                e, batch, seq)
                else:
                    proj = _in_proj(h, w_even[e], BF16, min(1024, rows_per_grp), N_EVEN_COLS // 2)
                    ya = _window_attention(proj, attn_sink[e], ck_a, cv_a, e, cos_tab, sin_tab, batch, seq)
                    o_f, o_b = _gla(proj, r_proj, wg_full[e], bg_full[e], state_gla, None, e, batch, seq)
                second = (o_f, o_b, gla_norm_g[e], proj)
            else:
                o = l // 2
                if is_ctx:
                    proj, side["kc"], side["vc"] = _in_proj_kv(
                        h, w_odd[o], tm_kv, 512, side["kc"], side["vc"], o, OD_KC, OD_VC)
                    ya = _ctx_attention(proj, None, batch, seq, OD_QC, OD_KC, OD_VC, OD_GPC, H_C, H_C)
                else:
                    proj = _in_proj(h, w_odd[o], BF16, tm_mm, 1024)
                    bias_exp = _na_bias_expand(u_tabs[o])
                    ya = _neighbourhood_attention(proj, bias_exp, ck_c, cv_c, o, batch, seq)
                second = _gmlp(proj, gmlp_norm_g[o], ws_bf[o], b_t[o], tm_op)
            final = l == depth - 1
            g_next = final_norm_g if final else norm_g[l + 1]
            mod_next = mod[l] if final else mod[l + 1]
            res = _out_proj(ya, second, w_o[l], x, mod[l], g_next, mod_next, grp0, rows_per_grp, tm_op, final)
            if final:
                out = res[0]
            else:
                x, h = res
        return out, side

    y_prompt, side = run_group(x_prompt.reshape(Mc, D), True)
    y_sample, _ = run_group(x_sample.reshape(Ml, D), False)

    return (y_prompt.reshape(B, L, D), y_sample.reshape(Bd, T, D),
            side["ka"].reshape(B, n_even, L, KV_A, HEAD_DIM), side["va"].reshape(B, n_even, L, KV_A, HEAD_DIM),
            side["st"],
            side["kc"].reshape(B, n_odd, L, H_C, HEAD_DIM), side["vc"].reshape(B, n_odd, L, H_C, HEAD_DIM))
```

```python
import functools

import numpy as np
import jax
import jax.numpy as jnp
from jax import lax
from jax.experimental import pallas as pl
from jax.experimental.pallas import tpu as pltpu

F32 = jnp.float32
BF16 = jnp.bfloat16

D_MODEL = 2048
HEAD_DIM = 128
GRID_W = 64
HALF_W = D_MODEL // 2
H_A = 8
KV_A = 2
WINDOW = 128
BAND_BLK = 128
H_B = 4
DK_B = 128
DV_B = 256
GLA_RANK = 16
GLA_NORMALIZER = 16.0
GLA_CHUNK = 64
GLA_GROUP = 4
GLA_STEP_GROUPS = 4
H_C = 8
NA_KH = 8
NA_KW = 16
NA_QR = 4
NA_WR = 12
G_D = 4
D_CHUNK = 128
ROPE_BASE = 10000.0
NEG = -1e30
LOG2E = 1.4426950408889634
Q_FOLD = HEAD_DIM ** -0.5 * LOG2E

EV_QA, EV_KA, EV_VA, EV_QB, EV_KB, EV_VB = 0, 1024, 1280, 1536, 2048, 2560
EV_GP = 3584
EV_GPA, EV_GPB = EV_GP, EV_GP + 1024
EV_RB_SRC = 3584
N_EVEN_COLS = 5632
OUT_TAIL = (256,)
WIN_QB = 2
NORM_ROWS = 16
SM_VREGS = 32
OD_QC, OD_KC, OD_VC, OD_U, OD_V, OD_GPC, OD_GPD = 0, 1024, 2048, 3072, 4096, 5120, 6144
N_ODD_COLS = 7168

VMEM_LIMIT = 52 * 1024 * 1024


def _cparams(*sem):
    return pltpu.CompilerParams(dimension_semantics=sem, vmem_limit_bytes=VMEM_LIMIT)


def _silu(x):
    return x / (1.0 + jnp.exp(-x))


def _dot(a, b):
    return jnp.dot(a, b, preferred_element_type=F32)


def _dot_nt(a, b):
    return lax.dot_general(a, b, (((1,), (1,)), ((), ())), preferred_element_type=F32)


def _rms(x, g, eps=1e-6):
    return x * lax.rsqrt(jnp.mean(x * x, axis=-1, keepdims=True) + eps) * g


def _sm_rows(n_cols):
    return max(16, min(128, (SM_VREGS * 1024 // n_cols) // 16 * 16))


def _softmax_rows(parts, add_fns, extra_fn, chunked):
    m_rows = parts[0].shape[0]
    nr = _sm_rows(sum(s.shape[1] for s in parts)) if chunked else m_rows
    extras, ps = [], []
    for r0 in range(0, m_rows, nr):
        rows = slice(r0, r0 + nr)
        ss = []
        for s, fn in zip(parts, add_fns):
            x = s[rows, :]
            if fn is not None:
                x = x + fn(r0, nr)
            ss.append(x)
        m = functools.reduce(jnp.maximum, [jnp.max(x, axis=-1, keepdims=True) for x in ss])
        if extra_fn is not None:
            m = jnp.maximum(m, extra_fn(r0))
            extras.append(jnp.exp2(extra_fn(r0) - m))
        es = [jnp.exp2((x - m).astype(BF16)) for x in ss]
        ps.append(es[0] if len(es) == 1 else jnp.concatenate(es, axis=1))
    extra_w = jnp.concatenate(extras, axis=0) if extras else None
    return jnp.concatenate(ps, axis=0), extra_w


def _softmax_small(s, extra):
    m = jnp.max(s, axis=-1, keepdims=True)
    if extra is not None:
        m = jnp.maximum(m, extra)
    e = jnp.exp2(s - m)
    den = jnp.sum(e, axis=-1, keepdims=True)
    if extra is not None:
        den = den + jnp.exp2(extra - m)
    return e.astype(BF16), den


def _pv_normalised(p, v, extra_w):
    o2 = _dot(p, jnp.concatenate([v, jnp.ones_like(v)], axis=1))
    den = o2[:, HEAD_DIM:HEAD_DIM + 1]
    if extra_w is not None:
        den = den + extra_w
    return o2[:, 0:HEAD_DIM] / den


def _mod_kernel(cond_ref, w_ref, b_ref, o_ref):
    a = _silu(cond_ref[...]).astype(BF16)
    o_ref[0] = _dot(a, w_ref[0].astype(BF16)) + b_ref[0]


def _modulation(cond, ada_w, ada_b):
    depth, d, n = ada_w.shape
    rows = cond.shape[0]
    tn = 1024
    return pl.pallas_call(
        _mod_kernel,
        grid=(depth, n // tn),
        in_specs=[
            pl.BlockSpec((rows, d), lambda l, j: (0, 0)),
            pl.BlockSpec((1, d, tn), lambda l, j: (l, 0, j)),
            pl.BlockSpec((1, 1, tn), lambda l, j: (l, 0, j)),
        ],
        out_specs=pl.BlockSpec((1, rows, tn), lambda l, j: (l, 0, j)),
        out_shape=jax.ShapeDtypeStruct((depth, rows, n), F32),
        compiler_params=_cparams("parallel", "parallel"),
        name="modulation",
    )(cond, ada_w, ada_b.reshape(depth, 1, n))


def _prenorm_kernel(x_ref, g_ref, mod_ref, h_ref, r_scr):
    d = x_ref.shape[-1]
    n_chunks = x_ref.shape[0] // NORM_ROWS
    shift = mod_ref[0, :, 0:d]
    gain = g_ref[...] * (1.0 + mod_ref[0, :, d:2 * d])

    def rows_of(c):
        return pl.ds(pl.multiple_of(c * NORM_ROWS, NORM_ROWS), NORM_ROWS)

    def stats(c, carry):
        x = x_ref[rows_of(c), :]
        r_scr[rows_of(c), :] = lax.rsqrt(jnp.mean(x * x, axis=-1, keepdims=True) + 1e-6)
        return carry

    def apply(c, carry):
        h_ref[rows_of(c), :] = (x_ref[rows_of(c), :] * r_scr[rows_of(c), :] * gain + shift).astype(BF16)
        return carry

    lax.fori_loop(0, n_chunks, stats, 0, unroll=8)
    lax.fori_loop(0, n_chunks, apply, 0, unroll=4)


def _prenorm(x, g, mod, grp0, rows_per_grp, tm):
    m, d = x.shape
    tpg = rows_per_grp // tm
    return pl.pallas_call(
        _prenorm_kernel,
        grid=(m // tm,),
        in_specs=[
            pl.BlockSpec((tm, d), lambda i: (i, 0)),
            pl.BlockSpec((1, d), lambda i: (0, 0)),
            pl.BlockSpec((1, 1, 3 * d), lambda i: (grp0 + i // tpg, 0, 0)),
        ],
        out_specs=pl.BlockSpec((tm, d), lambda i: (i, 0)),
        out_shape=jax.ShapeDtypeStruct((m, d), BF16),
        scratch_shapes=[pltpu.VMEM((tm, 1), F32)],
        compiler_params=_cparams("parallel"),
        name="prenorm",
    )(x, g.reshape(1, d), mod)


def _cast_kernel(w_ref, o_ref, *, q_tiles):
    scale = jnp.where(pl.program_id(0) < q_tiles, Q_FOLD, 1.0)
    o_ref[...] = (w_ref[...] * scale).astype(BF16)


def _cast_cols(w3, layer, first_tile, n_tiles, tw, q_tiles=0):
    k = w3.shape[1]
    return pl.pallas_call(
        functools.partial(_cast_kernel, q_tiles=q_tiles),
        grid=(n_tiles,),
        in_specs=[pl.BlockSpec((None, k, tw), lambda j: (layer, 0, first_tile + j))],
        out_specs=pl.BlockSpec((k, tw), lambda j: (0, j)),
        out_shape=jax.ShapeDtypeStruct((k, n_tiles * tw), BF16),
        compiler_params=_cparams("parallel"),
        name="weight_cast",
    )(w3)


def _cast_t_kernel(a_ref, b_ref, o_ref, *, q_tiles, gla_q_tile, plain_tiles, skip):
    j = pl.program_id(0)

    @pl.when(j < plain_tiles)
    def _():
        scale = jnp.where(j < q_tiles, Q_FOLD, jnp.where(j == gla_q_tile, DK_B ** -0.5, 1.0))
        o_ref[...] = (a_ref[...] * scale).T.astype(BF16)

    @pl.when(j >= plain_tiles)
    def _():
        x = jnp.concatenate([a_ref[skip:, :], b_ref[...]], axis=0)
        o_ref[...] = x.T.astype(BF16)


def _cast_even_t(w3t, layer):
    n_src, k = w3t.shape[1:]
    tw = 512
    skip = 2 * GLA_RANK
    assert EV_GP % tw == 0 and N_EVEN_COLS % tw == 0 and n_src == N_EVEN_COLS + skip
    return pl.pallas_call(
        functools.partial(_cast_t_kernel, q_tiles=H_A * HEAD_DIM // tw, gla_q_tile=EV_QB // tw,
                          plain_tiles=EV_GP // tw, skip=skip),
        grid=(N_EVEN_COLS // tw,),
        in_specs=[pl.BlockSpec((None, tw, k), lambda j: (layer, j, 0)),
                  pl.BlockSpec((None, skip, k), lambda j: (layer, (tw // skip) * (j + 1), 0))],
        out_specs=pl.BlockSpec((k, tw), lambda j: (0, j)),
        out_shape=jax.ShapeDtypeStruct((k, N_EVEN_COLS), BF16),
        compiler_params=_cparams("parallel"),
        name="weight_cast_even",
    )(w3t, w3t)


def _cast_rows_t_kernel(a_ref, o_ref):
    o_ref[...] = a_ref[...].T.astype(BF16)


def _cast_rows_t(w3t, layer, row0, n_rows):
    k = w3t.shape[2]
    return pl.pallas_call(
        _cast_rows_t_kernel,
        grid=(1,),
        in_specs=[pl.BlockSpec((None, n_rows, k), lambda j: (layer, row0 // n_rows, 0))],
        out_specs=pl.BlockSpec((k, n_rows), lambda j: (0, 0)),
        out_shape=jax.ShapeDtypeStruct((k, n_rows), BF16),
        compiler_params=_cparams("parallel"),
        name="weight_cast_rows",
    )(w3t)


def _matmul_kernel(h_ref, w_ref, o_ref):
    o_ref[...] = _dot(h_ref[...], w_ref[...]).astype(o_ref.dtype)


def _in_proj(h, w, out_dtype, tm, tn):
    m, k = h.shape
    n = w.shape[1]
    return pl.pallas_call(
        _matmul_kernel,
        grid=(m // tm, n // tn),
        in_specs=[
            pl.BlockSpec((tm, k), lambda i, j: (i, 0)),
            pl.BlockSpec((k, tn), lambda i, j: (0, j)),
        ],
        out_specs=pl.BlockSpec((tm, tn), lambda i, j: (i, j)),
        out_shape=jax.ShapeDtypeStruct((m, n), out_dtype),
        compiler_params=_cparams("parallel", "parallel"),
        name="in_proj",
    )(h, w)


def _matmul_kv_kernel(h_ref, w_ref, kin_ref, vin_ref, o_ref, k_ref, v_ref, *, k_col, v_col, width):
    del kin_ref, vin_ref
    tn = w_ref.shape[-1]
    j = pl.program_id(1)
    res = _dot(h_ref[...], w_ref[...])
    o_ref[...] = res.astype(o_ref.dtype)
    for col, ref in ((k_col, k_ref), (v_col, v_ref)):
        for jj in range(col // tn, (col + width - 1) // tn + 1):
            lo, hi = max(col, jj * tn), min(col + width, (jj + 1) * tn)

            @pl.when(j == jj)
            def _(lo=lo, hi=hi, jj=jj, col=col, ref=ref):
                piece = res[:, lo - jj * tn:hi - jj * tn]
                ref[:, :, lo - col:hi - col] = piece.reshape(ref.shape[0], ref.shape[1], hi - lo)


def _in_proj_kv(h, w, tm, tn, k_buf, v_buf, layer, k_col, v_col):
    m, k = h.shape
    n = w.shape[1]
    batch, _, seq, width = k_buf.shape
    assert tm % seq == 0
    kvspec = pl.BlockSpec((tm // seq, None, seq, width), lambda i, j: (i, layer, 0, 0))
    anyspec = pl.BlockSpec(memory_space=pl.ANY)
    return pl.pallas_call(
        functools.partial(_matmul_kv_kernel, k_col=k_col, v_col=v_col, width=width),
        grid=(m // tm, n // tn),
        in_specs=[
            pl.BlockSpec((tm, k), lambda i, j: (i, 0)),
            pl.BlockSpec((k, tn), lambda i, j: (0, j)),
            anyspec, anyspec,
        ],
        out_specs=[pl.BlockSpec((tm, tn), lambda i, j: (i, j)), kvspec, kvspec],
        out_shape=[jax.ShapeDtypeStruct((m, n), F32),
                   jax.ShapeDtypeStruct(k_buf.shape, F32), jax.ShapeDtypeStruct(v_buf.shape, F32)],
        input_output_aliases={2: 1, 3: 2},
        compiler_params=_cparams("arbitrary", "arbitrary"),
        name="in_proj_kv",
    )(h, w, k_buf, v_buf)


def _gla_merge(of_ref, ob_ref, gn_ref, gp0_ref, gp1_ref, rows):
    hpv = H_B // 2
    outs = []
    for h in range(H_B):
        vl = slice(h * DV_B, (h + 1) * DV_B)
        gp_ref = gp0_ref if h < hpv else gp1_ref
        gp = gp_ref[rows, (h % hpv) * DV_B:(h % hpv + 1) * DV_B].astype(F32)
        o = _rms(of_ref[rows, vl].astype(F32) + ob_ref[rows, vl].astype(F32), gn_ref[...])
        outs.append((o * _silu(gp)).astype(BF16))
    return jnp.concatenate(outs, axis=1)


def _outproj_kernel(*refs, final, fuse_gla):
    if fuse_gla:
        ya_ref, of_ref, ob_ref, gn_ref, gp0_ref, gp1_ref = refs[:6]
        refs = refs[6:]
    else:
        ya_ref, yb_ref = refs[:2]
        refs = refs[2:]
    w_ref, x_ref, mod_ref, g_ref, modn_ref = refs[:5]
    out_refs = refs[5:]
    d = x_ref.shape[-1]
    half = ya_ref.shape[-1]
    tm = x_ref.shape[0]
    gate = mod_ref[0, :, 2 * d:3 * d]
    bounds = [0] + [tm - s for s in OUT_TAIL if s < tm] + [tm]
    for r0, r1 in zip(bounds[:-1], bounds[1:]):
        rows = slice(r0, r1)
        if fuse_gla:
            yb = _gla_merge(of_ref, ob_ref, gn_ref, gp0_ref, gp1_ref, rows)
        else:
            yb = yb_ref[rows, :]
        acc = _dot(ya_ref[rows, :], w_ref[0:half, :]) + _dot(yb, w_ref[half:2 * half, :])
        xn = x_ref[rows, :] + gate * acc
        if final:
            out_refs[0][rows, :] = _rms(xn, g_ref[...])
        else:
            out_refs[0][rows, :] = xn
            shift = modn_ref[0, :, 0:d]
            scale = modn_ref[0, :, d:2 * d]
            out_refs[1][rows, :] = (_rms(xn, g_ref[...]) * (1.0 + scale) + shift).astype(BF16)


def _out_proj(ya, second, w, x, mod, g_next, mod_next, grp0, rows_per_grp, tm, final):
    m, d = x.shape
    half = ya.shape[1]
    tpg = rows_per_grp // tm
    modspec = pl.BlockSpec((1, 1, 3 * d), lambda i: (grp0 + i // tpg, 0, 0))
    rowspec = pl.BlockSpec((tm, d), lambda i: (i, 0))
    halfspec = pl.BlockSpec((tm, half), lambda i: (i, 0))
    fuse_gla = isinstance(second, tuple)
    if fuse_gla:
        o_f, o_b, gnorm, proj = second
        gw = half // 2
        mix_specs = [halfspec, halfspec, halfspec,
                     pl.BlockSpec((1, DV_B), lambda i: (0, 0)),
                     pl.BlockSpec((tm, gw), lambda i: (i, EV_GPB // gw)),
                     pl.BlockSpec((tm, gw), lambda i: (i, EV_GPB // gw + 1))]
        mix_args = [ya, o_f, o_b, gnorm.reshape(1, DV_B), proj, proj]
    else:
        mix_specs = [halfspec, halfspec]
        mix_args = [ya, second]
    if final:
        out_shape = [jax.ShapeDtypeStruct((m, d), F32)]
        out_specs = [rowspec]
    else:
        out_shape = [jax.ShapeDtypeStruct((m, d), F32), jax.ShapeDtypeStruct((m, d), BF16)]
        out_specs = [rowspec, rowspec]
    return pl.pallas_call(
        functools.partial(_outproj_kernel, final=final, fuse_gla=fuse_gla),
        grid=(m // tm,),
        in_specs=mix_specs + [
            pl.BlockSpec((2 * half, d), lambda i: (0, 0)),
            rowspec, modspec,
            pl.BlockSpec((1, d), lambda i: (0, 0)),
            modspec,
        ],
        out_specs=out_specs,
        out_shape=out_shape,
        compiler_params=_cparams("parallel"),
        name="out_proj",
    )(*mix_args, w, x, mod, g_next.reshape(1, d), mod_next)


def _ctx_attn_kernel(*refs, n_heads, n_kv, has_sink):
    if has_sink:
        sink_ref, q_ref, k_ref, v_ref, gp0_ref, gp1_ref, o_ref = refs
    else:
        q_ref, k_ref, v_ref, gp0_ref, gp1_ref, o_ref = refs
    L = q_ref.shape[0]
    grp = n_heads // n_kv
    outs = [None] * n_heads
    for kv in range(n_kv):
        heads = list(range(kv * grp, (kv + 1) * grp))
        qg = jnp.concatenate(
            [q_ref[:, h * HEAD_DIM:(h + 1) * HEAD_DIM].astype(BF16) for h in heads], axis=0)
        k = k_ref[:, kv * HEAD_DIM:(kv + 1) * HEAD_DIM].astype(BF16)
        v = v_ref[:, kv * HEAD_DIM:(kv + 1) * HEAD_DIM].astype(BF16)
        s = _dot_nt(qg, k)
        sk = None
        if has_sink:
            sk = jnp.concatenate([jnp.full((L, 1), sink_ref[h] * LOG2E, F32) for h in heads], axis=0)
        p, den = _softmax_small(s, sk)
        o = _dot(p, v) / den
        for i, h in enumerate(heads):
            outs[h] = o[i * L:(i + 1) * L, :]
    _store_gated(o_ref, outs, gp0_ref, gp1_ref)


def _store_gated(o_ref, head_outs, gp0_ref, gp1_ref):
    hw = gp0_ref.shape[-1]
    per = hw // HEAD_DIM
    for i, gp_ref in enumerate((gp0_ref, gp1_ref)):
        y = jnp.concatenate(head_outs[i * per:(i + 1) * per], axis=1)
        o_ref[:, i * hw:(i + 1) * hw] = (y * _silu(gp_ref[...].astype(F32))).astype(BF16)


def _ctx_attention(proj, sink, batch, L, q_col, k_col, v_col, gp_col, n_heads, n_kv):
    qw = n_heads * HEAD_DIM
    kw = n_kv * HEAD_DIM
    gw = qw // 2
    has_sink = sink is not None
    in_specs = [
        pl.BlockSpec((L, qw), lambda b: (b, q_col // qw)),
        pl.BlockSpec((L, kw), lambda b: (b, k_col // kw)),
        pl.BlockSpec((L, kw), lambda b: (b, v_col // kw)),
        pl.BlockSpec((L, gw), lambda b: (b, gp_col // gw)),
        pl.BlockSpec((L, gw), lambda b: (b, gp_col // gw + 1)),
    ]
    args = [proj, proj, proj, proj, proj]
    if has_sink:
        in_specs = [pl.BlockSpec(memory_space=pltpu.SMEM)] + in_specs
        args = [sink] + args
    return pl.pallas_call(
        functools.partial(_ctx_attn_kernel, n_heads=n_heads, n_kv=n_kv, has_sink=has_sink),
        grid=(batch,),
        in_specs=in_specs,
        out_specs=pl.BlockSpec((L, qw), lambda b: (b, 0)),
        out_shape=jax.ShapeDtypeStruct((batch * L, qw), BF16),
        compiler_params=_cparams("parallel"),
        name="ctx_attention",
    )(*args)


def _rope(x, cos, sin_signed):
    lane = lax.broadcasted_iota(jnp.int32, x.shape, 1)
    first = (lane % 64) < 32
    swapped = jnp.where(first, pltpu.roll(x, 96, 1), pltpu.roll(x, 32, 1))
    return x * cos + swapped * sin_signed


def _win_attn_kernel(sink_ref, q_ref, kp_ref, kc_ref, kn_ref, vp_ref, vc_ref, vn_ref,
                     cq_ref, sq_ref, cp_ref, sp_ref, cn_ref, sn_ref, mask_ref,
                     ck_ref, cv_ref, gp0_ref, gp1_ref, o_ref, *, nsteps):
    p = pl.program_id(1)
    blk = BAND_BLK
    grp = H_A // KV_A
    cq, sq = cq_ref[...], sq_ref[...]
    lane3 = lax.broadcasted_iota(jnp.int32, (1, 3 * blk), 1)
    edges = []
    for t in range(WIN_QB):
        edge = jnp.zeros((1, 3 * blk), F32)
        if t == 0:
            edge = edge + jnp.where(lane3 < blk, jnp.where(p > 0, 0.0, NEG), 0.0)
        if t == WIN_QB - 1:
            edge = edge + jnp.where(lane3 >= 2 * blk, jnp.where(p < nsteps - 1, 0.0, NEG), 0.0)
        edges.append(edge)
    outs = [[None] * WIN_QB for _ in range(H_A)]
    for kv in range(KV_A):
        sl = slice(kv * HEAD_DIM, (kv + 1) * HEAD_DIM)
        k_all = jnp.concatenate([
            _rope(kp_ref[:, sl].astype(F32), cp_ref[...], sp_ref[...]),
            _rope(kc_ref[:, sl].astype(F32), cq, sq),
            _rope(kn_ref[:, sl].astype(F32), cn_ref[...], sn_ref[...]),
        ], axis=0).astype(BF16)
        v_all = jnp.concatenate([vp_ref[:, sl], vc_ref[:, sl], vn_ref[:, sl]], axis=0).astype(BF16)
        heads = list(range(kv * grp, (kv + 1) * grp))
        q_rot = [_rope(q_ref[:, h * HEAD_DIM:(h + 1) * HEAD_DIM].astype(F32), cq, sq).astype(BF16)
                 for h in heads]
        ck = ck_ref[:, sl]
        cv = cv_ref[:, sl]
        for t in range(WIN_QB):
            rows = slice(t * blk, (t + 1) * blk)
            band = slice(t * blk, (t + 3) * blk)
            qg = jnp.concatenate([q[rows, :] for q in q_rot], axis=0)
            s_band = _dot_nt(qg, k_all[band, :])
            s_ctx = _dot_nt(qg, ck)
            p, sink_w = _softmax_rows(
                [s_band, s_ctx],
                [lambda r0, nr, t=t: mask_ref[r0 % blk:r0 % blk + nr, :] + edges[t], None],
                lambda r0: sink_ref[heads[r0 // blk]] * LOG2E, chunked=True)
            o = _pv_normalised(p, jnp.concatenate([v_all[band, :], cv], axis=0), sink_w)
            for i, h in enumerate(heads):
                outs[h][t] = o[i * blk:(i + 1) * blk, :]
    _store_gated(o_ref, [jnp.concatenate(o, axis=0) for o in outs], gp0_ref, gp1_ref)


def _window_attention(proj, sink, cache_k, cache_v, layer, cos_tab, sin_tab, batch, T):
    blk = BAND_BLK
    nb = T // blk
    qb = WIN_QB
    assert nb % qb == 0
    nsteps = nb // qb
    rq = qb * blk
    qw = H_A * HEAD_DIM
    kw = KV_A * HEAD_DIM
    gw = qw // 2
    P = cache_k.shape[2]

    def prev(p):
        return jnp.maximum(qb * p - 1, 0)

    def nxt(p):
        return jnp.minimum(qb * p + qb, nb - 1)

    kcol, vcol = EV_KA // kw, EV_VA // kw

    def edge(col, f):
        return pl.BlockSpec((blk, kw), lambda b, p: (b * nb + f(p), col))

    def mid(col):
        return pl.BlockSpec((rq, kw), lambda b, p: (b * nsteps + p, col))

    tab_edge = lambda f: pl.BlockSpec((blk, HEAD_DIM), lambda b, p: (f(p), 0))
    tab_mid = pl.BlockSpec((rq, HEAD_DIM), lambda b, p: (p, 0))
    in_specs = [
        pl.BlockSpec(memory_space=pltpu.SMEM),
        pl.BlockSpec((rq, qw), lambda b, p: (b * nsteps + p, EV_QA // qw)),
        edge(kcol, prev), mid(kcol), edge(kcol, nxt),
        edge(vcol, prev), mid(vcol), edge(vcol, nxt),
        tab_mid, tab_mid, tab_edge(prev), tab_edge(prev), tab_edge(nxt), tab_edge(nxt),
        pl.BlockSpec((blk, 3 * blk), lambda b, p: (0, 0)),
        pl.BlockSpec((None, None, P, kw), lambda b, p: (b, layer, 0, 0)),
        pl.BlockSpec((None, None, P, kw), lambda b, p: (b, layer, 0, 0)),
        pl.BlockSpec((rq, gw), lambda b, p: (b * nsteps + p, EV_GPA // gw)),
        pl.BlockSpec((rq, gw), lambda b, p: (b * nsteps + p, EV_GPA // gw + 1)),
    ]
    return pl.pallas_call(
        functools.partial(_win_attn_kernel, nsteps=nsteps),
        grid=(batch, nsteps),
        in_specs=in_specs,
        out_specs=pl.BlockSpec((rq, qw), lambda b, p: (b * nsteps + p, 0)),
        out_shape=jax.ShapeDtypeStruct((batch * T, qw), BF16),
        compiler_params=_cparams("parallel", "parallel"),
        name="window_attention",
    )(sink, proj, proj, proj, proj, proj, proj, proj,
      cos_tab, sin_tab, cos_tab, sin_tab, cos_tab, sin_tab, _band_mask(),
      cache_k, cache_v, proj, proj)


def _band_mask():
    iq = np.arange(BAND_BLK)[:, None]
    jk = np.arange(3 * BAND_BLK)[None, :]
    rel = iq - jk + BAND_BLK
    return jnp.asarray(np.where(np.abs(rel) <= WINDOW, 0.0, NEG), F32)


def _log_sigmoid(x):
    return jnp.minimum(x, 0.0) - jnp.log1p(jnp.exp(-jnp.abs(x)))


def _cumsum_rows(tri, g):
    g_hi = g.astype(BF16)
    r1 = g - g_hi.astype(F32)
    g_mid = r1.astype(BF16)
    g_lo = (r1 - g_mid.astype(F32)).astype(BF16)
    return _dot(tri, g_hi) + (_dot(tri, g_mid) + _dot(tri, g_lo))


def _gla_kernel(*refs, nsteps, has_s0):
    fwd_refs, bwd_refs = refs[0:5], refs[5:10]
    wg_ref, bg_ref = refs[10:12]
    if has_s0:
        s0_ref, of_ref, ob_ref, s_scr = refs[12:]
        sfin_ref = None
    else:
        of_ref, ob_ref, sfin_ref, s_scr = refs[13:]
    qf = fwd_refs[0]
    step = pl.program_id(1)
    C = GLA_CHUNK
    G = GLA_GROUP
    R = G * C
    n_sub = qf.shape[0] // R

    @pl.when(step == 0)
    def _():
        if has_s0:
            s_scr[...] = s0_ref[...]
        else:
            s_scr[...] = jnp.zeros(s_scr.shape, F32)

    ri = lax.broadcasted_iota(jnp.int32, (R, R), 0)
    ci = lax.broadcasted_iota(jnp.int32, (R, R), 1)
    same_chunk = (ri // C) == (ci // C)
    hw = H_B * DK_B
    hpv = H_B // 2
    for d, ((q_ref, k_ref, v0_ref, v1_ref, r_ref), o_ref) in enumerate(
            ((fwd_refs, of_ref), (bwd_refs, ob_ref))):
        keep = same_chunk & ((ci <= ri) if d == 0 else (ci >= ri))
        tri = jnp.where(keep, 1.0, 0.0).astype(BF16)
        edge = C - 1 if d == 0 else 0
        order = range(G) if d == 0 else range(G - 1, -1, -1)
        states = [s_scr[d, h] for h in range(H_B)]
        for sub in (range(n_sub) if d == 0 else range(n_sub - 1, -1, -1)):
            blk = slice(sub * R, (sub + 1) * R)
            logit = (_dot(r_ref[blk, :].astype(BF16), wg_ref[:, d * hw:(d + 1) * hw])
                     + bg_ref[:, d * hw:(d + 1) * hw])
            g = _log_sigmoid(logit) / GLA_NORMALIZER
            bc = _cumsum_rows(tri, g)
            totals = [bc[j * C + edge:j * C + edge + 1, :] for j in range(G)]
            tot_full = jnp.concatenate([jnp.broadcast_to(t, (C, hw)) for t in totals], axis=0)
            e_pos = jnp.exp(bc)
            e_neg = jnp.exp(-bc)
            e_rem = jnp.exp(tot_full - bc)
            e_tot = [jnp.exp(t) for t in totals]
            for h in range(H_B):
                sl = slice(h * DK_B, (h + 1) * DK_B)
                vl = slice(h * DV_B, (h + 1) * DV_B)
                qh = q_ref[blk, sl].astype(F32)
                kh = k_ref[blk, sl].astype(F32)
                v_ref = v0_ref if h < hpv else v1_ref
                vh = v_ref[blk, (h % hpv) * DV_B:(h % hpv + 1) * DV_B].astype(BF16)
                qi = (qh * e_pos[:, sl]).astype(BF16)
                ki = (kh * e_neg[:, sl]).astype(BF16)
                ks = kh * e_rem[:, sl]
                att = jnp.where(keep, _dot_nt(qi, ki), 0.0).astype(BF16)
                o_intra = _dot(att, vh)
                S = states[h]
                for j in order:
                    rows = slice(j * C, (j + 1) * C)
                    out_rows = slice(sub * R + j * C, sub * R + (j + 1) * C)
                    o_ref[out_rows, vl] = (o_intra[rows, :]
                                           + _dot(qi[rows, :], S.astype(BF16))).astype(o_ref.dtype)
                    stacked = jnp.concatenate(
                        [ks[rows, :], jnp.broadcast_to(e_tot[j][:, sl], (C, DK_B))], axis=0)
                    tr = stacked.T
                    S = tr[:, C:C + 1] * S + _dot(tr[:, 0:C].astype(BF16), vh[rows, :])
                states[h] = S
        for h in range(H_B):
            s_scr[d, h] = states[h]

    if sfin_ref is not None:
        @pl.when(step == nsteps - 1)
        def _():
            sfin_ref[...] = s_scr[...]


def _gla(proj, r_proj, wg_full, bg_full, s0, state_buf, layer, batch, T):
    group_rows = GLA_GROUP * GLA_CHUNK
    C = group_rows * max(1, min(GLA_STEP_GROUPS, T // group_rows))
    assert T % C == 0
    nc = T // C
    m = batch * T
    qw, vw, rw = H_B * DK_B, H_B * DV_B, 128
    vh = vw // 2

    def fwd(b, c):
        return b * nc + c

    def bwd(b, c):
        return b * nc + (nc - 1 - c)

    def specs(f):
        return [
            pl.BlockSpec((C, qw), lambda b, c: (f(b, c), EV_QB // qw)),
            pl.BlockSpec((C, qw), lambda b, c: (f(b, c), EV_KB // qw)),
            pl.BlockSpec((C, vh), lambda b, c: (f(b, c), EV_VB // vh)),
            pl.BlockSpec((C, vh), lambda b, c: (f(b, c), EV_VB // vh + 1)),
            pl.BlockSpec((C, rw), lambda b, c: (f(b, c), 0)),
        ]

    in_specs = specs(fwd) + specs(bwd) + [
        pl.BlockSpec((rw, 2 * qw), lambda b, c: (0, 0)),
        pl.BlockSpec((1, 2 * qw), lambda b, c: (0, 0)),
    ]
    args = [proj, proj, proj, proj, r_proj] * 2 + [wg_full, bg_full]
    has_s0 = s0 is not None
    state_spec = pl.BlockSpec((None, None, 2, H_B, DK_B, DV_B), lambda b, c: (b, layer, 0, 0, 0, 0))
    out_specs = [pl.BlockSpec((C, vw), lambda b, c: (fwd(b, c), 0)),
                 pl.BlockSpec((C, vw), lambda b, c: (bwd(b, c), 0))]
    out_shape = [jax.ShapeDtypeStruct((m, vw), BF16), jax.ShapeDtypeStruct((m, vw), BF16)]
    aliases = {}
    if has_s0:
        in_specs.append(state_spec)
        args.append(s0)
    else:
        in_specs.append(pl.BlockSpec(memory_space=pl.ANY))
        args.append(state_buf)
        out_specs.append(state_spec)
        out_shape.append(jax.ShapeDtypeStruct(state_buf.shape, F32))
        aliases = {len(args) - 1: 2}
    return pl.pallas_call(
        functools.partial(_gla_kernel, nsteps=nc, has_s0=has_s0),
        grid=(batch, nc),
        in_specs=in_specs,
        out_specs=out_specs,
        out_shape=out_shape,
        input_output_aliases=aliases,
        scratch_shapes=[pltpu.VMEM((2, H_B, DK_B, DV_B), F32)],
        compiler_params=_cparams("arbitrary", "arbitrary"),
        name="gla_scan",
    )(*args)


def _na_bias_kernel(u_ref, o_ref):
    place = pl.program_id(0)
    W = GRID_W
    lane = lax.broadcasted_iota(jnp.int32, (W, 128), 1)
    qcol = lax.broadcasted_iota(jnp.int32, (W, 128), 0)
    kcol = lane % W
    cs = jnp.clip(qcol - NA_KW // 2, 0, W - NA_KW)
    col_ok = (kcol >= cs) & (kcol < cs + NA_KW)
    for t in range(NA_QR):
        first = jnp.where(place == 0, 0, jnp.where(place == 1, t, NA_WR - NA_KH))

        def piece(i, base_shift):
            dr = i - t + (NA_KH - 1) - (NA_WR - NA_KH) * place
            in_rows = (i >= first) & (i < first + NA_KH)
            u = jnp.broadcast_to(u_ref[0, pl.ds(jnp.clip(dr, 0, 2 * NA_KH - 2), 1), :], (W, 128))
            rolled = pltpu.roll(u, base_shift, 1, stride=1, stride_axis=0)
            return rolled + jnp.where(in_rows, 0.0, NEG)

        tiles = []
        for i in range(0, NA_WR, 2):
            tile = jnp.where(lane < W, piece(i, 65), piece(i + 1, 1))
            tiles.append(jnp.where(col_ok, tile, NEG) * LOG2E)
        o_ref[0, 0, t * W:(t + 1) * W, :] = jnp.concatenate(tiles, axis=1)


def _na_bias_expand(u_tab):
    W = GRID_W
    n_tab = u_tab.shape[1]
    return pl.pallas_call(
        _na_bias_kernel,
        grid=(3, H_C),
        in_specs=[pl.BlockSpec((1, n_tab, 128), lambda p, h: (h, 0, 0))],
        out_specs=pl.BlockSpec((1, 1, NA_QR * W, NA_WR * W), lambda p, h: (p, h, 0, 0)),
        out_shape=jax.ShapeDtypeStruct((3, H_C, NA_QR * W, NA_WR * W), F32),
        compiler_params=_cparams("parallel", "parallel"),
        name="na_bias_expand",
    )(u_tab)


def _na_kernel(*refs):
    nblk = NA_WR // NA_QR
    q_ref = refs[0]
    k_refs = refs[1:1 + nblk]
    v_refs = refs[1 + nblk:1 + 2 * nblk]
    bias_ref, ck_ref, cv_ref, gp_ref, o_ref = refs[1 + 2 * nblk:]
    outs = []
    for h in range(H_C):
        sl = slice(h * HEAD_DIM, (h + 1) * HEAD_DIM)
        q = q_ref[:, sl]
        kw = jnp.concatenate([r[:, sl] for r in k_refs], axis=0)
        v_cat = jnp.concatenate([r[:, sl] for r in v_refs] + [cv_ref[:, sl]], axis=0)
        s_nb = _dot_nt(q, kw)
        s_ctx = _dot_nt(q, ck_ref[:, sl])
        p, _ = _softmax_rows([s_nb, s_ctx],
                             [lambda r0, nr, h=h: bias_ref[0, h, r0:r0 + nr, :], None], None,
                             chunked=False)
        outs.append(_pv_normalised(p, v_cat, None))
    y = jnp.concatenate(outs, axis=1)
    o_ref[...] = (y * _silu(gp_ref[...].astype(F32))).astype(BF16)


def _neighbourhood_attention(proj, bias_exp, cache_k, cache_v, layer, batch, T):
    W = GRID_W
    rows = T // W
    assert rows % NA_QR == 0 and rows >= NA_WR and NA_WR % NA_QR == 0
    steps = rows // NA_QR
    nblk = NA_WR // NA_QR
    rq = NA_QR * W
    qw = H_C * HEAD_DIM
    P = cache_k.shape[2]

    def win0(p):
        return jnp.clip(p - 1, 0, steps - nblk)

    def place(p):
        return jnp.where(p == 0, 0, jnp.where(p == steps - 1, 2, 1))

    def win_spec(col, i):
        return pl.BlockSpec((rq, qw), lambda b, p: (b * steps + win0(p) + i, col // qw))

    in_specs = ([pl.BlockSpec((rq, qw), lambda b, p: (b * steps + p, OD_QC // qw))]
                + [win_spec(OD_KC, i) for i in range(nblk)]
                + [win_spec(OD_VC, i) for i in range(nblk)]
                + [pl.BlockSpec((1, H_C, rq, NA_WR * W), lambda b, p: (place(p), 0, 0, 0)),
                   pl.BlockSpec((None, None, P, qw), lambda b, p: (b, layer, 0, 0)),
                   pl.BlockSpec((None, None, P, qw), lambda b, p: (b, layer, 0, 0)),
                   pl.BlockSpec((rq, qw), lambda b, p: (b * steps + p, OD_GPC // qw))])
    return pl.pallas_call(
        _na_kernel,
        grid=(batch, steps),
        in_specs=in_specs,
        out_specs=pl.BlockSpec((rq, qw), lambda b, p: (b * steps + p, 0)),
        out_shape=jax.ShapeDtypeStruct((batch * T, qw), BF16),
        compiler_params=_cparams("parallel", "parallel"),
        name="neighbourhood_attention",
    )(proj, *([proj] * (2 * nblk)), bias_exp, cache_k, cache_v, proj)


def _gmlp_kernel(u_ref, v_ref, gp_ref, g_ref, ws_ref, bt_ref, o_ref, *, chunks):
    gw = u_ref.shape[-1] // G_D
    for ch in range(chunks):
        rs_ = slice(ch * D_CHUNK, (ch + 1) * D_CHUNK)
        v = v_ref[rs_, :].astype(F32)
        mu = jnp.mean(v, axis=-1, keepdims=True)
        var = jnp.mean(jnp.square(v - mu), axis=-1, keepdims=True)
        vn = ((v - mu) * lax.rsqrt(var + 1e-5) * g_ref[...]).astype(BF16)
        for g in range(G_D):
            cl = slice(g * gw, (g + 1) * gw)
            sp = _dot(ws_ref[g], vn[:, cl]) + bt_ref[:, g:g + 1]
            y = u_ref[rs_, cl].astype(F32) * sp
            o_ref[rs_, cl] = (y * _silu(gp_ref[rs_, cl].astype(F32))).astype(BF16)


def _gmlp(proj, gnorm, ws_bf, b_t, tm):
    m = proj.shape[0]
    uw = HALF_W
    return pl.pallas_call(
        functools.partial(_gmlp_kernel, chunks=tm // D_CHUNK),
        grid=(m // tm,),
        in_specs=[
            pl.BlockSpec((tm, uw), lambda i: (i, OD_U // uw)),
            pl.BlockSpec((tm, uw), lambda i: (i, OD_V // uw)),
            pl.BlockSpec((tm, uw), lambda i: (i, OD_GPD // uw)),
            pl.BlockSpec((1, uw), lambda i: (0, 0)),
            pl.BlockSpec(ws_bf.shape, lambda i: (0, 0, 0)),
            pl.BlockSpec(b_t.shape, lambda i: (0, 0)),
        ],
        out_specs=pl.BlockSpec((tm, uw), lambda i: (i, 0)),
        out_shape=jax.ShapeDtypeStruct((m, uw), BF16),
        compiler_params=_cparams("parallel"),
        name="gmlp",
    )(proj, proj, proj, gnorm.reshape(1, uw), ws_bf, b_t)


def _rope_tables(T):
    t = np.arange(T)
    nf = HEAD_DIM // 4
    inv = ROPE_BASE ** (-jnp.arange(nf, dtype=F32) / nf)
    pos = jnp.stack([t // GRID_W, t % GRID_W], axis=-1).astype(F32)
    ang = pos[:, :, None] * inv
    cos, sin = jnp.cos(ang), jnp.sin(ang)
    cos_tab = jnp.concatenate([cos, cos], axis=-1).reshape(T, HEAD_DIM)
    sin_tab = jnp.concatenate([-sin, sin], axis=-1).reshape(T, HEAD_DIM)
    return cos_tab, sin_tab


def kernel(x_prompt, x_sample, cache_attn_k, cache_attn_v, state_gla, cache_na_k, cache_na_v,
           c, c_ctx, ada_w, ada_b, norm_g, w_in_even, w_in_odd, w_out, attn_sink,
           gla_wg, gla_bg, gla_norm_g, na_bias, gmlp_norm_g, gmlp_ws, gmlp_b, final_norm_g):
    B, L, D = x_prompt.shape
    Bd, T, _ = x_sample.shape
    depth = ada_w.shape[0]
    n_even = w_in_even.shape[0]
    n_odd = w_in_odd.shape[0]
    P = cache_attn_k.shape[2]
    Mc, Ml = B * L, Bd * T

    n_rows = 8 * ((1 + Bd + 7) // 8)
    cond = jnp.concatenate([c_ctx[None, :], c, jnp.zeros((n_rows - 1 - Bd, D), F32)], axis=0)
    mod = _modulation(cond, ada_w, ada_b)
    mod = mod.reshape(depth, n_rows, 1, 3 * D)

    tw = 512
    w_even_t = jnp.swapaxes(w_in_even, 1, 2)
    w_even = [_cast_even_t(w_even_t, e) for e in range(n_even)]
    w_rank = [_cast_rows_t(w_even_t, e, EV_RB_SRC, 128) for e in range(n_even)]
    w_odd = [_cast_cols(w_in_odd, o, 0, N_ODD_COLS // tw, tw, q_tiles=H_C * HEAD_DIM // tw)
             for o in range(n_odd)]
    w_o = [_cast_cols(w_out, l, 0, D // tw, tw) for l in range(depth)]
    hw = H_B * DK_B
    wg_full, bg_full = [], []
    for e in range(n_even):
        wg = jnp.zeros((128, 2 * hw), F32)
        wg = wg.at[0:GLA_RANK, 0:hw].set(gla_wg[e, 0])
        wg = wg.at[GLA_RANK:2 * GLA_RANK, hw:2 * hw].set(gla_wg[e, 1])
        wg_full.append(wg.astype(BF16))
        bg_full.append(gla_bg[e].reshape(1, 2 * hw))
    pad_l = 63 - (NA_KW - 1)
    u_tabs = [jnp.pad(na_bias[o], ((0, 0), (0, 0), (pad_l, 128 - pad_l - (2 * NA_KW - 1))), mode="edge")
              for o in range(n_odd)]
    ws_bf = [gmlp_ws[o].astype(BF16) for o in range(n_odd)]
    b_t = [gmlp_b[o].T for o in range(n_odd)]
    cos_tab, sin_tab = _rope_tables(T)
    ck_a = cache_attn_k.reshape(Bd, n_even, P, KV_A * HEAD_DIM).astype(BF16)
    cv_a = cache_attn_v.reshape(Bd, n_even, P, KV_A * HEAD_DIM).astype(BF16)
    ck_c = cache_na_k.reshape(Bd, n_odd, P, H_C * HEAD_DIM).astype(BF16)
    cv_c = cache_na_v.reshape(Bd, n_odd, P, H_C * HEAD_DIM).astype(BF16)

    def run_group(x, is_ctx):
        batch, seq = (B, L) if is_ctx else (Bd, T)
        m = batch * seq
        grp0 = 0 if is_ctx else 1
        rows_per_grp = m if is_ctx else seq
        tm_mm = min(2048, rows_per_grp)
        tm_kv = min(1024, rows_per_grp)
        tm_op = min(512, rows_per_grp)
        side = None
        if is_ctx:
            side = dict(ka=jnp.zeros((B, n_even, L, KV_A * HEAD_DIM), F32),
                        va=jnp.zeros((B, n_even, L, KV_A * HEAD_DIM), F32),
                        st=jnp.zeros((B, n_even, 2, H_B, DK_B, DV_B), F32),
                        kc=jnp.zeros((B, n_odd, L, H_C * HEAD_DIM), F32),
                        vc=jnp.zeros((B, n_odd, L, H_C * HEAD_DIM), F32))
        h = _prenorm(x, norm_g[0], mod[0], grp0, rows_per_grp, tm_op)
        out = None
        for l in range(depth):
            if l % 2 == 0:
                e = l // 2
                r_proj = _in_proj(h, w_rank[e], BF16, tm_mm, 128)
                if is_ctx:
                    proj, side["ka"], side["va"] = _in_proj_kv(
                        h, w_even[e], tm_kv, 512, side["ka"], side["va"], e, EV_KA, EV_VA)
                    ya = _ctx_attention(proj, attn_sink[e], batch, seq, EV_QA, EV_KA, EV_VA, EV_GPA,
                                        H_A, KV_A)
                    o_f, o_b, side["st"] = _gla(proj, r_proj, wg_full[e], bg_full[e], None, side["st"],
                                                e, batch, seq)
                else:
                    proj = _in_proj(h, w_even[e], BF16, min(1024, rows_per_grp), N_EVEN_COLS // 2)
                    ya = _window_attention(proj, attn_sink[e], ck_a, cv_a, e, cos_tab, sin_tab, batch, seq)
                    o_f, o_b = _gla(proj, r_proj, wg_full[e], bg_full[e], state_gla, None, e, batch, seq)
                second = (o_f, o_b, gla_norm_g[e], proj)
            else:
                o = l // 2
                if is_ctx:
                    proj, side["kc"], side["vc"] = _in_proj_kv(
                        h, w_odd[o], tm_kv, 1024, side["kc"], side["vc"], o, OD_KC, OD_VC)
                    ya = _ctx_attention(proj, None, batch, seq, OD_QC, OD_KC, OD_VC, OD_GPC, H_C, H_C)
                else:
                    proj = _in_proj(h, w_odd[o], BF16, tm_mm, 1024)
                    bias_exp = _na_bias_expand(u_tabs[o])
                    ya = _neighbourhood_attention(proj, bias_exp, ck_c, cv_c, o, batch, seq)
                second = _gmlp(proj, gmlp_norm_g[o], ws_bf[o], b_t[o], tm_op)
            final = l == depth - 1
            g_next = final_norm_g if final else norm_g[l + 1]
            mod_next = mod[l] if final else mod[l + 1]
            res = _out_proj(ya, second, w_o[l], x, mod[l], g_next, mod_next, grp0, rows_per_grp, tm_op, final)
            if final:
                out = res[0]
            else:
                x, h = res
        return out, side

    y_prompt, side = run_group(x_prompt.reshape(Mc, D), True)
    y_sample, _ = run_group(x_sample.reshape(Ml, D), False)

    return (y_prompt.reshape(B, L, D), y_sample.reshape(Bd, T, D),
            side["ka"].reshape(B, n_even, L, KV_A, HEAD_DIM), side["va"].reshape(B, n_even, L, KV_A, HEAD_DIM),
            side["st"],
            side["kc"].reshape(B, n_odd, L, H_C, HEAD_DIM), side["vc"].reshape(B, n_odd, L, H_C, HEAD_DIM))
```

```python
import functools

import numpy as np
import jax
import jax.numpy as jnp
from jax import lax
from jax.experimental import pallas as pl
from jax.experimental.pallas import tpu as pltpu

F32 = jnp.float32
BF16 = jnp.bfloat16

D_MODEL = 2048
HEAD_DIM = 128
GRID_W = 64
HALF_W = D_MODEL // 2
H_A = 8
KV_A = 2
WINDOW = 128
BAND_BLK = 128
H_B = 4
DK_B = 128
DV_B = 256
GLA_RANK = 16
GLA_NORMALIZER = 16.0
GLA_CHUNK = 64
GLA_GROUP = 4
GLA_STEP_GROUPS = 4
H_C = 8
NA_KH = 8
NA_KW = 16
NA_QR = 4
NA_WR = 12
G_D = 4
D_CHUNK = 128
ROPE_BASE = 10000.0
NEG = -1e30
LOG2E = 1.4426950408889634
Q_FOLD = HEAD_DIM ** -0.5 * LOG2E

EV_QA, EV_KA, EV_VA, EV_QB, EV_KB, EV_VB = 0, 1024, 1280, 1536, 2048, 2560
EV_GP = 3584
EV_GPA, EV_GPB = EV_GP, EV_GP + 1024
EV_RB_SRC = 3584
N_EVEN_COLS = 5632
OUT_TAIL = (256,)
WIN_QB = 4
SM_VREGS = 32
OD_QC, OD_KC, OD_VC, OD_U, OD_V, OD_GPC, OD_GPD = 0, 1024, 2048, 3072, 4096, 5120, 6144
N_ODD_COLS = 7168

VMEM_LIMIT = 52 * 1024 * 1024


def _cparams(*sem):
    return pltpu.CompilerParams(dimension_semantics=sem, vmem_limit_bytes=VMEM_LIMIT)


def _silu(x):
    return x / (1.0 + jnp.exp(-x))


def _dot(a, b):
    return jnp.dot(a, b, preferred_element_type=F32)


def _dot_nt(a, b):
    return lax.dot_general(a, b, (((1,), (1,)), ((), ())), preferred_element_type=F32)


def _rms(x, g, eps=1e-6):
    return x * lax.rsqrt(jnp.mean(x * x, axis=-1, keepdims=True) + eps) * g


def _sm_rows(n_cols):
    return max(16, min(128, (SM_VREGS * 1024 // n_cols) // 16 * 16))


def _softmax_rows(parts, add_fns, extra_fn, chunked):
    m_rows = parts[0].shape[0]
    nr = _sm_rows(sum(s.shape[1] for s in parts)) if chunked else m_rows
    extras, ps = [], []
    for r0 in range(0, m_rows, nr):
        rows = slice(r0, r0 + nr)
        ss = []
        for s, fn in zip(parts, add_fns):
            x = s[rows, :]
            if fn is not None:
                x = x + fn(r0, nr)
            ss.append(x)
        m = functools.reduce(jnp.maximum, [jnp.max(x, axis=-1, keepdims=True) for x in ss])
        if extra_fn is not None:
            m = jnp.maximum(m, extra_fn(r0))
            extras.append(jnp.exp2(extra_fn(r0) - m))
        es = [jnp.exp2((x - m).astype(BF16)) for x in ss]
        ps.append(es[0] if len(es) == 1 else jnp.concatenate(es, axis=1))
    extra_w = jnp.concatenate(extras, axis=0) if extras else None
    return jnp.concatenate(ps, axis=0), extra_w


def _softmax_small(s, extra):
    m = jnp.max(s, axis=-1, keepdims=True)
    if extra is not None:
        m = jnp.maximum(m, extra)
    e = jnp.exp2(s - m)
    den = jnp.sum(e, axis=-1, keepdims=True)
    if extra is not None:
        den = den + jnp.exp2(extra - m)
    return e.astype(BF16), den


def _pv_normalised(p, v, extra_w):
    o2 = _dot(p, jnp.concatenate([v, jnp.ones_like(v)], axis=1))
    den = o2[:, HEAD_DIM:HEAD_DIM + 1]
    if extra_w is not None:
        den = den + extra_w
    return o2[:, 0:HEAD_DIM] / den


def _mod_kernel(cond_ref, w_ref, b_ref, o_ref):
    a = _silu(cond_ref[...]).astype(BF16)
    o_ref[0] = _dot(a, w_ref[0].astype(BF16)) + b_ref[0]


def _modulation(cond, ada_w, ada_b):
    depth, d, n = ada_w.shape
    rows = cond.shape[0]
    tn = 1024
    return pl.pallas_call(
        _mod_kernel,
        grid=(depth, n // tn),
        in_specs=[
            pl.BlockSpec((rows, d), lambda l, j: (0, 0)),
            pl.BlockSpec((1, d, tn), lambda l, j: (l, 0, j)),
            pl.BlockSpec((1, 1, tn), lambda l, j: (l, 0, j)),
        ],
        out_specs=pl.BlockSpec((1, rows, tn), lambda l, j: (l, 0, j)),
        out_shape=jax.ShapeDtypeStruct((depth, rows, n), F32),
        compiler_params=_cparams("parallel", "parallel"),
        name="modulation",
    )(cond, ada_w, ada_b.reshape(depth, 1, n))


def _prenorm_kernel(x_ref, g_ref, mod_ref, h_ref):
    d = x_ref.shape[-1]
    y = _rms(x_ref[...], g_ref[...])
    shift = mod_ref[0, :, 0:d]
    scale = mod_ref[0, :, d:2 * d]
    h_ref[...] = (y * (1.0 + scale) + shift).astype(BF16)


def _prenorm(x, g, mod, grp0, rows_per_grp, tm):
    m, d = x.shape
    tpg = rows_per_grp // tm
    return pl.pallas_call(
        _prenorm_kernel,
        grid=(m // tm,),
        in_specs=[
            pl.BlockSpec((tm, d), lambda i: (i, 0)),
            pl.BlockSpec((1, d), lambda i: (0, 0)),
            pl.BlockSpec((1, 1, 3 * d), lambda i: (grp0 + i // tpg, 0, 0)),
        ],
        out_specs=pl.BlockSpec((tm, d), lambda i: (i, 0)),
        out_shape=jax.ShapeDtypeStruct((m, d), BF16),
        compiler_params=_cparams("parallel"),
        name="prenorm",
    )(x, g.reshape(1, d), mod)


def _cast_kernel(w_ref, o_ref, *, q_tiles):
    scale = jnp.where(pl.program_id(0) < q_tiles, Q_FOLD, 1.0)
    o_ref[...] = (w_ref[...] * scale).astype(BF16)


def _cast_cols(w3, layer, first_tile, n_tiles, tw, q_tiles=0):
    k = w3.shape[1]
    return pl.pallas_call(
        functools.partial(_cast_kernel, q_tiles=q_tiles),
        grid=(n_tiles,),
        in_specs=[pl.BlockSpec((None, k, tw), lambda j: (layer, 0, first_tile + j))],
        out_specs=pl.BlockSpec((k, tw), lambda j: (0, j)),
        out_shape=jax.ShapeDtypeStruct((k, n_tiles * tw), BF16),
        compiler_params=_cparams("parallel"),
        name="weight_cast",
    )(w3)


def _cast_t_kernel(a_ref, b_ref, o_ref, *, q_tiles, gla_q_tile, plain_tiles, skip):
    j = pl.program_id(0)

    @pl.when(j < plain_tiles)
    def _():
        scale = jnp.where(j < q_tiles, Q_FOLD, jnp.where(j == gla_q_tile, DK_B ** -0.5, 1.0))
        o_ref[...] = (a_ref[...] * scale).T.astype(BF16)

    @pl.when(j >= plain_tiles)
    def _():
        x = jnp.concatenate([a_ref[skip:, :], b_ref[...]], axis=0)
        o_ref[...] = x.T.astype(BF16)


def _cast_even_t(w3t, layer):
    n_src, k = w3t.shape[1:]
    tw = 512
    skip = 2 * GLA_RANK
    assert EV_GP % tw == 0 and N_EVEN_COLS % tw == 0 and n_src == N_EVEN_COLS + skip
    return pl.pallas_call(
        functools.partial(_cast_t_kernel, q_tiles=H_A * HEAD_DIM // tw, gla_q_tile=EV_QB // tw,
                          plain_tiles=EV_GP // tw, skip=skip),
        grid=(N_EVEN_COLS // tw,),
        in_specs=[pl.BlockSpec((None, tw, k), lambda j: (layer, j, 0)),
                  pl.BlockSpec((None, skip, k), lambda j: (layer, (tw // skip) * (j + 1), 0))],
        out_specs=pl.BlockSpec((k, tw), lambda j: (0, j)),
        out_shape=jax.ShapeDtypeStruct((k, N_EVEN_COLS), BF16),
        compiler_params=_cparams("parallel"),
        name="weight_cast_even",
    )(w3t, w3t)


def _cast_rows_t_kernel(a_ref, o_ref):
    o_ref[...] = a_ref[...].T.astype(BF16)


def _cast_rows_t(w3t, layer, row0, n_rows):
    k = w3t.shape[2]
    return pl.pallas_call(
        _cast_rows_t_kernel,
        grid=(1,),
        in_specs=[pl.BlockSpec((None, n_rows, k), lambda j: (layer, row0 // n_rows, 0))],
        out_specs=pl.BlockSpec((k, n_rows), lambda j: (0, 0)),
        out_shape=jax.ShapeDtypeStruct((k, n_rows), BF16),
        compiler_params=_cparams("parallel"),
        name="weight_cast_rows",
    )(w3t)


def _matmul_kernel(h_ref, w_ref, o_ref):
    o_ref[...] = _dot(h_ref[...], w_ref[...]).astype(o_ref.dtype)


def _in_proj(h, w, out_dtype, tm, tn):
    m, k = h.shape
    n = w.shape[1]
    return pl.pallas_call(
        _matmul_kernel,
        grid=(m // tm, n // tn),
        in_specs=[
            pl.BlockSpec((tm, k), lambda i, j: (i, 0)),
            pl.BlockSpec((k, tn), lambda i, j: (0, j)),
        ],
        out_specs=pl.BlockSpec((tm, tn), lambda i, j: (i, j)),
        out_shape=jax.ShapeDtypeStruct((m, n), out_dtype),
        compiler_params=_cparams("parallel", "parallel"),
        name="in_proj",
    )(h, w)


def _matmul_kv_kernel(h_ref, w_ref, kin_ref, vin_ref, o_ref, k_ref, v_ref, *, k_col, v_col, width):
    del kin_ref, vin_ref
    tn = w_ref.shape[-1]
    j = pl.program_id(1)
    res = _dot(h_ref[...], w_ref[...])
    o_ref[...] = res.astype(o_ref.dtype)
    for col, ref in ((k_col, k_ref), (v_col, v_ref)):
        for jj in range(col // tn, (col + width - 1) // tn + 1):
            lo, hi = max(col, jj * tn), min(col + width, (jj + 1) * tn)

            @pl.when(j == jj)
            def _(lo=lo, hi=hi, jj=jj, col=col, ref=ref):
                piece = res[:, lo - jj * tn:hi - jj * tn]
                ref[:, :, lo - col:hi - col] = piece.reshape(ref.shape[0], ref.shape[1], hi - lo)


def _in_proj_kv(h, w, tm, tn, k_buf, v_buf, layer, k_col, v_col):
    m, k = h.shape
    n = w.shape[1]
    batch, _, seq, width = k_buf.shape
    assert tm % seq == 0
    kvspec = pl.BlockSpec((tm // seq, None, seq, width), lambda i, j: (i, layer, 0, 0))
    anyspec = pl.BlockSpec(memory_space=pl.ANY)
    return pl.pallas_call(
        functools.partial(_matmul_kv_kernel, k_col=k_col, v_col=v_col, width=width),
        grid=(m // tm, n // tn),
        in_specs=[
            pl.BlockSpec((tm, k), lambda i, j: (i, 0)),
            pl.BlockSpec((k, tn), lambda i, j: (0, j)),
            anyspec, anyspec,
        ],
        out_specs=[pl.BlockSpec((tm, tn), lambda i, j: (i, j)), kvspec, kvspec],
        out_shape=[jax.ShapeDtypeStruct((m, n), F32),
                   jax.ShapeDtypeStruct(k_buf.shape, F32), jax.ShapeDtypeStruct(v_buf.shape, F32)],
        input_output_aliases={2: 1, 3: 2},
        compiler_params=_cparams("arbitrary", "arbitrary"),
        name="in_proj_kv",
    )(h, w, k_buf, v_buf)


def _gla_merge(of_ref, ob_ref, gn_ref, gp0_ref, gp1_ref, rows):
    hpv = H_B // 2
    outs = []
    for h in range(H_B):
        vl = slice(h * DV_B, (h + 1) * DV_B)
        gp_ref = gp0_ref if h < hpv else gp1_ref
        gp = gp_ref[rows, (h % hpv) * DV_B:(h % hpv + 1) * DV_B].astype(F32)
        o = _rms(of_ref[rows, vl].astype(F32) + ob_ref[rows, vl].astype(F32), gn_ref[...])
        outs.append((o * _silu(gp)).astype(BF16))
    return jnp.concatenate(outs, axis=1)


def _outproj_kernel(*refs, final, fuse_gla):
    if fuse_gla:
        ya_ref, of_ref, ob_ref, gn_ref, gp0_ref, gp1_ref = refs[:6]
        refs = refs[6:]
    else:
        ya_ref, yb_ref = refs[:2]
        refs = refs[2:]
    w_ref, x_ref, mod_ref, g_ref, modn_ref = refs[:5]
    out_refs = refs[5:]
    d = x_ref.shape[-1]
    half = ya_ref.shape[-1]
    tm = x_ref.shape[0]
    gate = mod_ref[0, :, 2 * d:3 * d]
    bounds = [0] + [tm - s for s in OUT_TAIL if s < tm] + [tm]
    for r0, r1 in zip(bounds[:-1], bounds[1:]):
        rows = slice(r0, r1)
        if fuse_gla:
            yb = _gla_merge(of_ref, ob_ref, gn_ref, gp0_ref, gp1_ref, rows)
        else:
            yb = yb_ref[rows, :]
        acc = _dot(ya_ref[rows, :], w_ref[0:half, :]) + _dot(yb, w_ref[half:2 * half, :])
        xn = x_ref[rows, :] + gate * acc
        if final:
            out_refs[0][rows, :] = _rms(xn, g_ref[...])
        else:
            out_refs[0][rows, :] = xn
            shift = modn_ref[0, :, 0:d]
            scale = modn_ref[0, :, d:2 * d]
            out_refs[1][rows, :] = (_rms(xn, g_ref[...]) * (1.0 + scale) + shift).astype(BF16)


def _out_proj(ya, second, w, x, mod, g_next, mod_next, grp0, rows_per_grp, tm, final):
    m, d = x.shape
    half = ya.shape[1]
    tpg = rows_per_grp // tm
    modspec = pl.BlockSpec((1, 1, 3 * d), lambda i: (grp0 + i // tpg, 0, 0))
    rowspec = pl.BlockSpec((tm, d), lambda i: (i, 0))
    halfspec = pl.BlockSpec((tm, half), lambda i: (i, 0))
    fuse_gla = isinstance(second, tuple)
    if fuse_gla:
        o_f, o_b, gnorm, proj = second
        gw = half // 2
        mix_specs = [halfspec, halfspec, halfspec,
                     pl.BlockSpec((1, DV_B), lambda i: (0, 0)),
                     pl.BlockSpec((tm, gw), lambda i: (i, EV_GPB // gw)),
                     pl.BlockSpec((tm, gw), lambda i: (i, EV_GPB // gw + 1))]
        mix_args = [ya, o_f, o_b, gnorm.reshape(1, DV_B), proj, proj]
    else:
        mix_specs = [halfspec, halfspec]
        mix_args = [ya, second]
    if final:
        out_shape = [jax.ShapeDtypeStruct((m, d), F32)]
        out_specs = [rowspec]
    else:
        out_shape = [jax.ShapeDtypeStruct((m, d), F32), jax.ShapeDtypeStruct((m, d), BF16)]
        out_specs = [rowspec, rowspec]
    return pl.pallas_call(
        functools.partial(_outproj_kernel, final=final, fuse_gla=fuse_gla),
        grid=(m // tm,),
        in_specs=mix_specs + [
            pl.BlockSpec((2 * half, d), lambda i: (0, 0)),
            rowspec, modspec,
            pl.BlockSpec((1, d), lambda i: (0, 0)),
            modspec,
        ],
        out_specs=out_specs,
        out_shape=out_shape,
        compiler_params=_cparams("parallel"),
        name="out_proj",
    )(*mix_args, w, x, mod, g_next.reshape(1, d), mod_next)


def _ctx_attn_kernel(*refs, n_heads, n_kv, has_sink):
    if has_sink:
        sink_ref, q_ref, k_ref, v_ref, gp0_ref, gp1_ref, o_ref = refs
    else:
        q_ref, k_ref, v_ref, gp0_ref, gp1_ref, o_ref = refs
    L = q_ref.shape[0]
    grp = n_heads // n_kv
    outs = [None] * n_heads
    for kv in range(n_kv):
        heads = list(range(kv * grp, (kv + 1) * grp))
        qg = jnp.concatenate(
            [q_ref[:, h * HEAD_DIM:(h + 1) * HEAD_DIM].astype(BF16) for h in heads], axis=0)
        k = k_ref[:, kv * HEAD_DIM:(kv + 1) * HEAD_DIM].astype(BF16)
        v = v_ref[:, kv * HEAD_DIM:(kv + 1) * HEAD_DIM].astype(BF16)
        s = _dot_nt(qg, k)
        sk = None
        if has_sink:
            sk = jnp.concatenate([jnp.full((L, 1), sink_ref[h] * LOG2E, F32) for h in heads], axis=0)
        p, den = _softmax_small(s, sk)
        o = _dot(p, v) / den
        for i, h in enumerate(heads):
            outs[h] = o[i * L:(i + 1) * L, :]
    _store_gated(o_ref, outs, gp0_ref, gp1_ref)


def _store_gated(o_ref, head_outs, gp0_ref, gp1_ref):
    hw = gp0_ref.shape[-1]
    per = hw // HEAD_DIM
    for i, gp_ref in enumerate((gp0_ref, gp1_ref)):
        y = jnp.concatenate(head_outs[i * per:(i + 1) * per], axis=1)
        o_ref[:, i * hw:(i + 1) * hw] = (y * _silu(gp_ref[...].astype(F32))).astype(BF16)


def _ctx_attention(proj, sink, batch, L, q_col, k_col, v_col, gp_col, n_heads, n_kv):
    qw = n_heads * HEAD_DIM
    kw = n_kv * HEAD_DIM
    gw = qw // 2
    has_sink = sink is not None
    in_specs = [
        pl.BlockSpec((L, qw), lambda b: (b, q_col // qw)),
        pl.BlockSpec((L, kw), lambda b: (b, k_col // kw)),
        pl.BlockSpec((L, kw), lambda b: (b, v_col // kw)),
        pl.BlockSpec((L, gw), lambda b: (b, gp_col // gw)),
        pl.BlockSpec((L, gw), lambda b: (b, gp_col // gw + 1)),
    ]
    args = [proj, proj, proj, proj, proj]
    if has_sink:
        in_specs = [pl.BlockSpec(memory_space=pltpu.SMEM)] + in_specs
        args = [sink] + args
    return pl.pallas_call(
        functools.partial(_ctx_attn_kernel, n_heads=n_heads, n_kv=n_kv, has_sink=has_sink),
        grid=(batch,),
        in_specs=in_specs,
        out_specs=pl.BlockSpec((L, qw), lambda b: (b, 0)),
        out_shape=jax.ShapeDtypeStruct((batch * L, qw), BF16),
        compiler_params=_cparams("parallel"),
        name="ctx_attention",
    )(*args)


def _rope(x, cos, sin_signed):
    lane = lax.broadcasted_iota(jnp.int32, x.shape, 1)
    first = (lane % 64) < 32
    swapped = jnp.where(first, pltpu.roll(x, 96, 1), pltpu.roll(x, 32, 1))
    return x * cos + swapped * sin_signed


def _win_attn_kernel(sink_ref, q_ref, kp_ref, kc_ref, kn_ref, vp_ref, vc_ref, vn_ref,
                     cq_ref, sq_ref, cp_ref, sp_ref, cn_ref, sn_ref, mask_ref,
                     ck_ref, cv_ref, gp0_ref, gp1_ref, o_ref, *, nsteps):
    p = pl.program_id(1)
    blk = BAND_BLK
    grp = H_A // KV_A
    cq, sq = cq_ref[...], sq_ref[...]
    lane3 = lax.broadcasted_iota(jnp.int32, (1, 3 * blk), 1)
    edges = []
    for t in range(WIN_QB):
        edge = jnp.zeros((1, 3 * blk), F32)
        if t == 0:
            edge = edge + jnp.where(lane3 < blk, jnp.where(p > 0, 0.0, NEG), 0.0)
        if t == WIN_QB - 1:
            edge = edge + jnp.where(lane3 >= 2 * blk, jnp.where(p < nsteps - 1, 0.0, NEG), 0.0)
        edges.append(edge)
    outs = [[None] * WIN_QB for _ in range(H_A)]
    for kv in range(KV_A):
        sl = slice(kv * HEAD_DIM, (kv + 1) * HEAD_DIM)
        k_all = jnp.concatenate([
            _rope(kp_ref[:, sl].astype(F32), cp_ref[...], sp_ref[...]),
            _rope(kc_ref[:, sl].astype(F32), cq, sq),
            _rope(kn_ref[:, sl].astype(F32), cn_ref[...], sn_ref[...]),
        ], axis=0).astype(BF16)
        v_all = jnp.concatenate([vp_ref[:, sl], vc_ref[:, sl], vn_ref[:, sl]], axis=0).astype(BF16)
        heads = list(range(kv * grp, (kv + 1) * grp))
        q_rot = [_rope(q_ref[:, h * HEAD_DIM:(h + 1) * HEAD_DIM].astype(F32), cq, sq).astype(BF16)
                 for h in heads]
        ck = ck_ref[:, sl]
        cv = cv_ref[:, sl]
        for t in range(WIN_QB):
            rows = slice(t * blk, (t + 1) * blk)
            band = slice(t * blk, (t + 3) * blk)
            qg = jnp.concatenate([q[rows, :] for q in q_rot], axis=0)
            s_band = _dot_nt(qg, k_all[band, :])
            s_ctx = _dot_nt(qg, ck)
            p, sink_w = _softmax_rows(
                [s_band, s_ctx],
                [lambda r0, nr, t=t: mask_ref[r0 % blk:r0 % blk + nr, :] + edges[t], None],
                lambda r0: sink_ref[heads[r0 // blk]] * LOG2E, chunked=True)
            o = _pv_normalised(p, jnp.concatenate([v_all[band, :], cv], axis=0), sink_w)
            for i, h in enumerate(heads):
                outs[h][t] = o[i * blk:(i + 1) * blk, :]
    _store_gated(o_ref, [jnp.concatenate(o, axis=0) for o in outs], gp0_ref, gp1_ref)


def _window_attention(proj, sink, cache_k, cache_v, layer, cos_tab, sin_tab, batch, T):
    blk = BAND_BLK
    nb = T // blk
    qb = WIN_QB
    assert nb % qb == 0
    nsteps = nb // qb
    rq = qb * blk
    qw = H_A * HEAD_DIM
    kw = KV_A * HEAD_DIM
    gw = qw // 2
    P = cache_k.shape[2]

    def prev(p):
        return jnp.maximum(qb * p - 1, 0)

    def nxt(p):
        return jnp.minimum(qb * p + qb, nb - 1)

    kcol, vcol = EV_KA // kw, EV_VA // kw

    def edge(col, f):
        return pl.BlockSpec((blk, kw), lambda b, p: (b * nb + f(p), col))

    def mid(col):
        return pl.BlockSpec((rq, kw), lambda b, p: (b * nsteps + p, col))

    tab_edge = lambda f: pl.BlockSpec((blk, HEAD_DIM), lambda b, p: (f(p), 0))
    tab_mid = pl.BlockSpec((rq, HEAD_DIM), lambda b, p: (p, 0))
    in_specs = [
        pl.BlockSpec(memory_space=pltpu.SMEM),
        pl.BlockSpec((rq, qw), lambda b, p: (b * nsteps + p, EV_QA // qw)),
        edge(kcol, prev), mid(kcol), edge(kcol, nxt),
        edge(vcol, prev), mid(vcol), edge(vcol, nxt),
        tab_mid, tab_mid, tab_edge(prev), tab_edge(prev), tab_edge(nxt), tab_edge(nxt),
        pl.BlockSpec((blk, 3 * blk), lambda b, p: (0, 0)),
        pl.BlockSpec((None, None, P, kw), lambda b, p: (b, layer, 0, 0)),
        pl.BlockSpec((None, None, P, kw), lambda b, p: (b, layer, 0, 0)),
        pl.BlockSpec((rq, gw), lambda b, p: (b * nsteps + p, EV_GPA // gw)),
        pl.BlockSpec((rq, gw), lambda b, p: (b * nsteps + p, EV_GPA // gw + 1)),
    ]
    return pl.pallas_call(
        functools.partial(_win_attn_kernel, nsteps=nsteps),
        grid=(batch, nsteps),
        in_specs=in_specs,
        out_specs=pl.BlockSpec((rq, qw), lambda b, p: (b * nsteps + p, 0)),
        out_shape=jax.ShapeDtypeStruct((batch * T, qw), BF16),
        compiler_params=_cparams("parallel", "parallel"),
        name="window_attention",
    )(sink, proj, proj, proj, proj, proj, proj, proj,
      cos_tab, sin_tab, cos_tab, sin_tab, cos_tab, sin_tab, _band_mask(),
      cache_k, cache_v, proj, proj)


def _band_mask():
    iq = np.arange(BAND_BLK)[:, None]
    jk = np.arange(3 * BAND_BLK)[None, :]
    rel = iq - jk + BAND_BLK
    return jnp.asarray(np.where(np.abs(rel) <= WINDOW, 0.0, NEG), F32)


def _log_sigmoid(x):
    return jnp.minimum(x, 0.0) - jnp.log1p(jnp.exp(-jnp.abs(x)))


def _cumsum_rows(tri, g):
    g_hi = g.astype(BF16)
    r1 = g - g_hi.astype(F32)
    g_mid = r1.astype(BF16)
    g_lo = (r1 - g_mid.astype(F32)).astype(BF16)
    return _dot(tri, g_hi) + (_dot(tri, g_mid) + _dot(tri, g_lo))


def _gla_kernel(*refs, nsteps, has_s0):
    fwd_refs, bwd_refs = refs[0:5], refs[5:10]
    wg_ref, bg_ref = refs[10:12]
    if has_s0:
        s0_ref, of_ref, ob_ref, s_scr = refs[12:]
        sfin_ref = None
    else:
        of_ref, ob_ref, sfin_ref, s_scr = refs[13:]
    qf = fwd_refs[0]
    step = pl.program_id(1)
    C = GLA_CHUNK
    G = GLA_GROUP
    R = G * C
    n_sub = qf.shape[0] // R

    @pl.when(step == 0)
    def _():
        if has_s0:
            s_scr[...] = s0_ref[...]
        else:
            s_scr[...] = jnp.zeros(s_scr.shape, F32)

    ri = lax.broadcasted_iota(jnp.int32, (R, R), 0)
    ci = lax.broadcasted_iota(jnp.int32, (R, R), 1)
    same_chunk = (ri // C) == (ci // C)
    hw = H_B * DK_B
    hpv = H_B // 2
    for d, ((q_ref, k_ref, v0_ref, v1_ref, r_ref), o_ref) in enumerate(
            ((fwd_refs, of_ref), (bwd_refs, ob_ref))):
        keep = same_chunk & ((ci <= ri) if d == 0 else (ci >= ri))
        tri = jnp.where(keep, 1.0, 0.0).astype(BF16)
        edge = C - 1 if d == 0 else 0
        order = range(G) if d == 0 else range(G - 1, -1, -1)
        states = [s_scr[d, h] for h in range(H_B)]
        for sub in (range(n_sub) if d == 0 else range(n_sub - 1, -1, -1)):
            blk = slice(sub * R, (sub + 1) * R)
            logit = (_dot(r_ref[blk, :].astype(BF16), wg_ref[:, d * hw:(d + 1) * hw])
                     + bg_ref[:, d * hw:(d + 1) * hw])
            g = _log_sigmoid(logit) / GLA_NORMALIZER
            bc = _cumsum_rows(tri, g)
            totals = [bc[j * C + edge:j * C + edge + 1, :] for j in range(G)]
            tot_full = jnp.concatenate([jnp.broadcast_to(t, (C, hw)) for t in totals], axis=0)
            e_pos = jnp.exp(bc)
            e_neg = jnp.exp(-bc)
            e_rem = jnp.exp(tot_full - bc)
            e_tot = [jnp.exp(t) for t in totals]
            for h in range(H_B):
                sl = slice(h * DK_B, (h + 1) * DK_B)
                vl = slice(h * DV_B, (h + 1) * DV_B)
                qh = q_ref[blk, sl].astype(F32)
                kh = k_ref[blk, sl].astype(F32)
                v_ref = v0_ref if h < hpv else v1_ref
                vh = v_ref[blk, (h % hpv) * DV_B:(h % hpv + 1) * DV_B].astype(BF16)
                qi = (qh * e_pos[:, sl]).astype(BF16)
                ki = (kh * e_neg[:, sl]).astype(BF16)
                ks = kh * e_rem[:, sl]
                att = jnp.where(keep, _dot_nt(qi, ki), 0.0).astype(BF16)
                o_intra = _dot(att, vh)
                S = states[h]
                for j in order:
                    rows = slice(j * C, (j + 1) * C)
                    out_rows = slice(sub * R + j * C, sub * R + (j + 1) * C)
                    o_ref[out_rows, vl] = (o_intra[rows, :]
                                           + _dot(qi[rows, :], S.astype(BF16))).astype(o_ref.dtype)
                    stacked = jnp.concatenate(
                        [ks[rows, :], jnp.broadcast_to(e_tot[j][:, sl], (C, DK_B))], axis=0)
                    tr = stacked.T
                    S = tr[:, C:C + 1] * S + _dot(tr[:, 0:C].astype(BF16), vh[rows, :])
                states[h] = S
        for h in range(H_B):
            s_scr[d, h] = states[h]

    if sfin_ref is not None:
        @pl.when(step == nsteps - 1)
        def _():
            sfin_ref[...] = s_scr[...]


def _gla(proj, r_proj, wg_full, bg_full, s0, state_buf, layer, batch, T):
    group_rows = GLA_GROUP * GLA_CHUNK
    C = group_rows * max(1, min(GLA_STEP_GROUPS, T // group_rows))
    assert T % C == 0
    nc = T // C
    m = batch * T
    qw, vw, rw = H_B * DK_B, H_B * DV_B, 128
    vh = vw // 2

    def fwd(b, c):
        return b * nc + c

    def bwd(b, c):
        return b * nc + (nc - 1 - c)

    def specs(f):
        return [
            pl.BlockSpec((C, qw), lambda b, c: (f(b, c), EV_QB // qw)),
            pl.BlockSpec((C, qw), lambda b, c: (f(b, c), EV_KB // qw)),
            pl.BlockSpec((C, vh), lambda b, c: (f(b, c), EV_VB // vh)),
            pl.BlockSpec((C, vh), lambda b, c: (f(b, c), EV_VB // vh + 1)),
            pl.BlockSpec((C, rw), lambda b, c: (f(b, c), 0)),
        ]

    in_specs = specs(fwd) + specs(bwd) + [
        pl.BlockSpec((rw, 2 * qw), lambda b, c: (0, 0)),
        pl.BlockSpec((1, 2 * qw), lambda b, c: (0, 0)),
    ]
    args = [proj, proj, proj, proj, r_proj] * 2 + [wg_full, bg_full]
    has_s0 = s0 is not None
    state_spec = pl.BlockSpec((None, None, 2, H_B, DK_B, DV_B), lambda b, c: (b, layer, 0, 0, 0, 0))
    out_specs = [pl.BlockSpec((C, vw), lambda b, c: (fwd(b, c), 0)),
                 pl.BlockSpec((C, vw), lambda b, c: (bwd(b, c), 0))]
    out_shape = [jax.ShapeDtypeStruct((m, vw), BF16), jax.ShapeDtypeStruct((m, vw), BF16)]
    aliases = {}
    if has_s0:
        in_specs.append(state_spec)
        args.append(s0)
    else:
        in_specs.append(pl.BlockSpec(memory_space=pl.ANY))
        args.append(state_buf)
        out_specs.append(state_spec)
        out_shape.append(jax.ShapeDtypeStruct(state_buf.shape, F32))
        aliases = {len(args) - 1: 2}
    return pl.pallas_call(
        functools.partial(_gla_kernel, nsteps=nc, has_s0=has_s0),
        grid=(batch, nc),
        in_specs=in_specs,
        out_specs=out_specs,
        out_shape=out_shape,
        input_output_aliases=aliases,
        scratch_shapes=[pltpu.VMEM((2, H_B, DK_B, DV_B), F32)],
        compiler_params=_cparams("arbitrary", "arbitrary"),
        name="gla_scan",
    )(*args)


def _na_bias_kernel(u_ref, o_ref):
    place = pl.program_id(0)
    W = GRID_W
    lane = lax.broadcasted_iota(jnp.int32, (W, 128), 1)
    qcol = lax.broadcasted_iota(jnp.int32, (W, 128), 0)
    kcol = lane % W
    cs = jnp.clip(qcol - NA_KW // 2, 0, W - NA_KW)
    col_ok = (kcol >= cs) & (kcol < cs + NA_KW)
    for t in range(NA_QR):
        first = jnp.where(place == 0, 0, jnp.where(place == 1, t, NA_WR - NA_KH))

        def piece(i, base_shift):
            dr = i - t + (NA_KH - 1) - (NA_WR - NA_KH) * place
            in_rows = (i >= first) & (i < first + NA_KH)
            u = jnp.broadcast_to(u_ref[0, pl.ds(jnp.clip(dr, 0, 2 * NA_KH - 2), 1), :], (W, 128))
            rolled = pltpu.roll(u, base_shift, 1, stride=1, stride_axis=0)
            return rolled + jnp.where(in_rows, 0.0, NEG)

        tiles = []
        for i in range(0, NA_WR, 2):
            tile = jnp.where(lane < W, piece(i, 65), piece(i + 1, 1))
            tiles.append(jnp.where(col_ok, tile, NEG) * LOG2E)
        o_ref[0, 0, t * W:(t + 1) * W, :] = jnp.concatenate(tiles, axis=1)


def _na_bias_expand(u_tab):
    W = GRID_W
    n_tab = u_tab.shape[1]
    return pl.pallas_call(
        _na_bias_kernel,
        grid=(3, H_C),
        in_specs=[pl.BlockSpec((1, n_tab, 128), lambda p, h: (h, 0, 0))],
        out_specs=pl.BlockSpec((1, 1, NA_QR * W, NA_WR * W), lambda p, h: (p, h, 0, 0)),
        out_shape=jax.ShapeDtypeStruct((3, H_C, NA_QR * W, NA_WR * W), F32),
        compiler_params=_cparams("parallel", "parallel"),
        name="na_bias_expand",
    )(u_tab)


def _na_kernel(*refs):
    nblk = NA_WR // NA_QR
    q_ref = refs[0]
    k_refs = refs[1:1 + nblk]
    v_refs = refs[1 + nblk:1 + 2 * nblk]
    bias_ref, ck_ref, cv_ref, gp_ref, o_ref = refs[1 + 2 * nblk:]
    outs = []
    for h in range(H_C):
        sl = slice(h * HEAD_DIM, (h + 1) * HEAD_DIM)
        q = q_ref[:, sl]
        kw = jnp.concatenate([r[:, sl] for r in k_refs], axis=0)
        v_cat = jnp.concatenate([r[:, sl] for r in v_refs] + [cv_ref[:, sl]], axis=0)
        s_nb = _dot_nt(q, kw)
        s_ctx = _dot_nt(q, ck_ref[:, sl])
        p, _ = _softmax_rows([s_nb, s_ctx],
                             [lambda r0, nr, h=h: bias_ref[0, h, r0:r0 + nr, :], None], None,
                             chunked=False)
        outs.append(_pv_normalised(p, v_cat, None))
    y = jnp.concatenate(outs, axis=1)
    o_ref[...] = (y * _silu(gp_ref[...].astype(F32))).astype(BF16)


def _neighbourhood_attention(proj, bias_exp, cache_k, cache_v, layer, batch, T):
    W = GRID_W
    rows = T // W
    assert rows % NA_QR == 0 and rows >= NA_WR and NA_WR % NA_QR == 0
    steps = rows // NA_QR
    nblk = NA_WR // NA_QR
    rq = NA_QR * W
    qw = H_C * HEAD_DIM
    P = cache_k.shape[2]

    def win0(p):
        return jnp.clip(p - 1, 0, steps - nblk)

    def place(p):
        return jnp.where(p == 0, 0, jnp.where(p == steps - 1, 2, 1))

    def win_spec(col, i):
        return pl.BlockSpec((rq, qw), lambda b, p: (b * steps + win0(p) + i, col // qw))

    in_specs = ([pl.BlockSpec((rq, qw), lambda b, p: (b * steps + p, OD_QC // qw))]
                + [win_spec(OD_KC, i) for i in range(nblk)]
                + [win_spec(OD_VC, i) for i in range(nblk)]
                + [pl.BlockSpec((1, H_C, rq, NA_WR * W), lambda b, p: (place(p), 0, 0, 0)),
                   pl.BlockSpec((None, None, P, qw), lambda b, p: (b, layer, 0, 0)),
                   pl.BlockSpec((None, None, P, qw), lambda b, p: (b, layer, 0, 0)),
                   pl.BlockSpec((rq, qw), lambda b, p: (b * steps + p, OD_GPC // qw))])
    return pl.pallas_call(
        _na_kernel,
        grid=(batch, steps),
        in_specs=in_specs,
        out_specs=pl.BlockSpec((rq, qw), lambda b, p: (b * steps + p, 0)),
        out_shape=jax.ShapeDtypeStruct((batch * T, qw), BF16),
        compiler_params=_cparams("parallel", "parallel"),
        name="neighbourhood_attention",
    )(proj, *([proj] * (2 * nblk)), bias_exp, cache_k, cache_v, proj)


def _gmlp_kernel(u_ref, v_ref, gp_ref, g_ref, ws_ref, bt_ref, o_ref, *, chunks):
    gw = u_ref.shape[-1] // G_D
    for ch in range(chunks):
        rs_ = slice(ch * D_CHUNK, (ch + 1) * D_CHUNK)
        v = v_ref[rs_, :].astype(F32)
        mu = jnp.mean(v, axis=-1, keepdims=True)
        var = jnp.mean(jnp.square(v - mu), axis=-1, keepdims=True)
        vn = ((v - mu) * lax.rsqrt(var + 1e-5) * g_ref[...]).astype(BF16)
        for g in range(G_D):
            cl = slice(g * gw, (g + 1) * gw)
            sp = _dot(ws_ref[g], vn[:, cl]) + bt_ref[:, g:g + 1]
            y = u_ref[rs_, cl].astype(F32) * sp
            o_ref[rs_, cl] = (y * _silu(gp_ref[rs_, cl].astype(F32))).astype(BF16)


def _gmlp(proj, gnorm, ws_bf, b_t, tm):
    m = proj.shape[0]
    uw = HALF_W
    return pl.pallas_call(
        functools.partial(_gmlp_kernel, chunks=tm // D_CHUNK),
        grid=(m // tm,),
        in_specs=[
            pl.BlockSpec((tm, uw), lambda i: (i, OD_U // uw)),
            pl.BlockSpec((tm, uw), lambda i: (i, OD_V // uw)),
            pl.BlockSpec((tm, uw), lambda i: (i, OD_GPD // uw)),
            pl.BlockSpec((1, uw), lambda i: (0, 0)),
            pl.BlockSpec(ws_bf.shape, lambda i: (0, 0, 0)),
            pl.BlockSpec(b_t.shape, lambda i: (0, 0)),
        ],
        out_specs=pl.BlockSpec((tm, uw), lambda i: (i, 0)),
        out_shape=jax.ShapeDtypeStruct((m, uw), BF16),
        compiler_params=_cparams("parallel"),
        name="gmlp",
    )(proj, proj, proj, gnorm.reshape(1, uw), ws_bf, b_t)


def _rope_tables(T):
    t = np.arange(T)
    nf = HEAD_DIM // 4
    inv = ROPE_BASE ** (-jnp.arange(nf, dtype=F32) / nf)
    pos = jnp.stack([t // GRID_W, t % GRID_W], axis=-1).astype(F32)
    ang = pos[:, :, None] * inv
    cos, sin = jnp.cos(ang), jnp.sin(ang)
    cos_tab = jnp.concatenate([cos, cos], axis=-1).reshape(T, HEAD_DIM)
    sin_tab = jnp.concatenate([-sin, sin], axis=-1).reshape(T, HEAD_DIM)
    return cos_tab, sin_tab


def kernel(x_prompt, x_sample, cache_attn_k, cache_attn_v, state_gla, cache_na_k, cache_na_v,
           c, c_ctx, ada_w, ada_b, norm_g, w_in_even, w_in_odd, w_out, attn_sink,
           gla_wg, gla_bg, gla_norm_g, na_bias, gmlp_norm_g, gmlp_ws, gmlp_b, final_norm_g):
    B, L, D = x_prompt.shape
    Bd, T, _ = x_sample.shape
    depth = ada_w.shape[0]
    n_even = w_in_even.shape[0]
    n_odd = w_in_odd.shape[0]
    P = cache_attn_k.shape[2]
    Mc, Ml = B * L, Bd * T

    n_rows = 8 * ((1 + Bd + 7) // 8)
    cond = jnp.concatenate([c_ctx[None, :], c, jnp.zeros((n_rows - 1 - Bd, D), F32)], axis=0)
    mod = _modulation(cond, ada_w, ada_b)
    mod = mod.reshape(depth, n_rows, 1, 3 * D)

    tw = 512
    w_even_t = jnp.swapaxes(w_in_even, 1, 2)
    w_even = [_cast_even_t(w_even_t, e) for e in range(n_even)]
    w_rank = [_cast_rows_t(w_even_t, e, EV_RB_SRC, 128) for e in range(n_even)]
    w_odd = [_cast_cols(w_in_odd, o, 0, N_ODD_COLS // tw, tw, q_tiles=H_C * HEAD_DIM // tw)
             for o in range(n_odd)]
    w_o = [_cast_cols(w_out, l, 0, D // tw, tw) for l in range(depth)]
    hw = H_B * DK_B
    wg_full, bg_full = [], []
    for e in range(n_even):
        wg = jnp.zeros((128, 2 * hw), F32)
        wg = wg.at[0:GLA_RANK, 0:hw].set(gla_wg[e, 0])
        wg = wg.at[GLA_RANK:2 * GLA_RANK, hw:2 * hw].set(gla_wg[e, 1])
        wg_full.append(wg.astype(BF16))
        bg_full.append(gla_bg[e].reshape(1, 2 * hw))
    pad_l = 63 - (NA_KW - 1)
    u_tabs = [jnp.pad(na_bias[o], ((0, 0), (0, 0), (pad_l, 128 - pad_l - (2 * NA_KW - 1))), mode="edge")
              for o in range(n_odd)]
    ws_bf = [gmlp_ws[o].astype(BF16) for o in range(n_odd)]
    b_t = [gmlp_b[o].T for o in range(n_odd)]
    cos_tab, sin_tab = _rope_tables(T)
    ck_a = cache_attn_k.reshape(Bd, n_even, P, KV_A * HEAD_DIM).astype(BF16)
    cv_a = cache_attn_v.reshape(Bd, n_even, P, KV_A * HEAD_DIM).astype(BF16)
    ck_c = cache_na_k.reshape(Bd, n_odd, P, H_C * HEAD_DIM).astype(BF16)
    cv_c = cache_na_v.reshape(Bd, n_odd, P, H_C * HEAD_DIM).astype(BF16)

    def run_group(x, is_ctx):
        batch, seq = (B, L) if is_ctx else (Bd, T)
        m = batch * seq
        grp0 = 0 if is_ctx else 1
        rows_per_grp = m if is_ctx else seq
        tm_mm = min(2048, rows_per_grp)
        tm_kv = min(1024, rows_per_grp)
        tm_op = min(512, rows_per_grp)
        side = None
        if is_ctx:
            side = dict(ka=jnp.zeros((B, n_even, L, KV_A * HEAD_DIM), F32),
                        va=jnp.zeros((B, n_even, L, KV_A * HEAD_DIM), F32),
                        st=jnp.zeros((B, n_even, 2, H_B, DK_B, DV_B), F32),
                        kc=jnp.zeros((B, n_odd, L, H_C * HEAD_DIM), F32),
                        vc=jnp.zeros((B, n_odd, L, H_C * HEAD_DIM), F32))
        h = _prenorm(x, norm_g[0], mod[0], grp0, rows_per_grp, tm_op)
        out = None
        for l in range(depth):
            if l % 2 == 0:
                e = l // 2
                r_proj = _in_proj(h, w_rank[e], BF16, tm_mm, 128)
                if is_ctx:
                    proj, side["ka"], side["va"] = _in_proj_kv(
                        h, w_even[e], tm_mm, 512, side["ka"], side["va"], e, EV_KA, EV_VA)
                    ya = _ctx_attention(proj, attn_sink[e], batch, seq, EV_QA, EV_KA, EV_VA, EV_GPA,
                                        H_A, KV_A)
                    o_f, o_b, side["st"] = _gla(proj, r_proj, wg_full[e], bg_full[e], None, side["st"],
                                                e, batch, seq)
                else:
                    proj = _in_proj(h, w_even[e], BF16, min(1024, rows_per_grp), N_EVEN_COLS // 2)
                    ya = _window_attention(proj, attn_sink[e], ck_a, cv_a, e, cos_tab, sin_tab, batch, seq)
                    o_f, o_b = _gla(proj, r_proj, wg_full[e], bg_full[e], state_gla, None, e, batch, seq)
                second = (o_f, o_b, gla_norm_g[e], proj)
            else:
                o = l // 2
                if is_ctx:
                    proj, side["kc"], side["vc"] = _in_proj_kv(
                        h, w_odd[o], tm_kv, 1024, side["kc"], side["vc"], o, OD_KC, OD_VC)
                    ya = _ctx_attention(proj, None, batch, seq, OD_QC, OD_KC, OD_VC, OD_GPC, H_C, H_C)
                else:
                    proj = _in_proj(h, w_odd[o], BF16, tm_mm, 1024)
                    bias_exp = _na_bias_expand(u_tabs[o])
                    ya = _neighbourhood_attention(proj, bias_exp, ck_c, cv_c, o, batch, seq)
                second = _gmlp(proj, gmlp_norm_g[o], ws_bf[o], b_t[o], tm_op)
            final = l == depth - 1
            g_next = final_norm_g if final else norm_g[l + 1]
            mod_next = mod[l] if final else mod[l + 1]
            res = _out_proj(ya, second, w_o[l], x, mod[l], g_next, mod_next, grp0, rows_per_grp, tm_op, final)
            if final:
                out = res[0]
            else:
                x, h = res
        return out, side

    y_prompt, side = run_group(x_prompt.reshape(Mc, D), True)
    y_sample, _ = run_group(x_sample.reshape(Ml, D), False)

    return (y_prompt.reshape(B, L, D), y_sample.reshape(Bd, T, D),
            side["ka"].reshape(B, n_even, L, KV_A, HEAD_DIM), side["va"].reshape(B, n_even, L, KV_A, HEAD_DIM),
            side["st"],
            side["kc"].reshape(B, n_odd, L, H_C, HEAD_DIM), side["vc"].reshape(B, n_odd, L, H_C, HEAD_DIM))
```

```python
import functools

import numpy as np
import jax
import jax.numpy as jnp
from jax import lax
from jax.experimental import pallas as pl
from jax.experimental.pallas import tpu as pltpu

F32 = jnp.float32
BF16 = jnp.bfloat16

D_MODEL = 2048
HEAD_DIM = 128
GRID_W = 64
HALF_W = D_MODEL // 2
H_A = 8
KV_A = 2
WINDOW = 128
BAND_BLK = 128
H_B = 4
DK_B = 128
DV_B = 256
GLA_RANK = 16
GLA_NORMALIZER = 16.0
GLA_CHUNK = 64
GLA_GROUP = 4
GLA_STEP_GROUPS = 4
H_C = 8
NA_KH = 8
NA_KW = 16
NA_QR = 4
NA_WR = 12
G_D = 4
D_CHUNK = 128
ROPE_BASE = 10000.0
NEG = -1e30
LOG2E = 1.4426950408889634
Q_FOLD = HEAD_DIM ** -0.5 * LOG2E

EV_QA, EV_KA, EV_VA, EV_QB, EV_KB, EV_VB = 0, 1024, 1280, 1536, 2048, 2560
EV_GP = 3584
EV_GPA, EV_GPB = EV_GP, EV_GP + 1024
EV_RB_SRC = 3584
N_EVEN_COLS = 5632
OUT_TAIL = (256,)
WIN_QB = 8
SM_VREGS = 32
OD_QC, OD_KC, OD_VC, OD_U, OD_V, OD_GPC, OD_GPD = 0, 1024, 2048, 3072, 4096, 5120, 6144
N_ODD_COLS = 7168

VMEM_LIMIT = 52 * 1024 * 1024


def _cparams(*sem):
    return pltpu.CompilerParams(dimension_semantics=sem, vmem_limit_bytes=VMEM_LIMIT)


def _silu(x):
    return x / (1.0 + jnp.exp(-x))


def _dot(a, b):
    return jnp.dot(a, b, preferred_element_type=F32)


def _dot_nt(a, b):
    return lax.dot_general(a, b, (((1,), (1,)), ((), ())), preferred_element_type=F32)


def _rms(x, g, eps=1e-6):
    return x * lax.rsqrt(jnp.mean(x * x, axis=-1, keepdims=True) + eps) * g


def _sm_rows(n_cols):
    return max(16, min(128, (SM_VREGS * 1024 // n_cols) // 16 * 16))


def _softmax_rows(parts, add_fns, extra_fn, chunked):
    m_rows = parts[0].shape[0]
    nr = _sm_rows(sum(s.shape[1] for s in parts)) if chunked else m_rows
    extras, ps = [], []
    for r0 in range(0, m_rows, nr):
        rows = slice(r0, r0 + nr)
        ss = []
        for s, fn in zip(parts, add_fns):
            x = s[rows, :]
            if fn is not None:
                x = x + fn(r0, nr)
            ss.append(x)
        m = functools.reduce(jnp.maximum, [jnp.max(x, axis=-1, keepdims=True) for x in ss])
        if extra_fn is not None:
            m = jnp.maximum(m, extra_fn(r0))
            extras.append(jnp.exp2(extra_fn(r0) - m))
        es = [jnp.exp2((x - m).astype(BF16)) for x in ss]
        ps.append(es[0] if len(es) == 1 else jnp.concatenate(es, axis=1))
    extra_w = jnp.concatenate(extras, axis=0) if extras else None
    return jnp.concatenate(ps, axis=0), extra_w


def _softmax_small(s, extra):
    m = jnp.max(s, axis=-1, keepdims=True)
    if extra is not None:
        m = jnp.maximum(m, extra)
    e = jnp.exp2(s - m)
    den = jnp.sum(e, axis=-1, keepdims=True)
    if extra is not None:
        den = den + jnp.exp2(extra - m)
    return e.astype(BF16), den


def _pv_normalised(p, v, extra_w):
    o2 = _dot(p, jnp.concatenate([v, jnp.ones_like(v)], axis=1))
    den = o2[:, HEAD_DIM:HEAD_DIM + 1]
    if extra_w is not None:
        den = den + extra_w
    return o2[:, 0:HEAD_DIM] / den


def _mod_kernel(cond_ref, w_ref, b_ref, o_ref):
    a = _silu(cond_ref[...]).astype(BF16)
    o_ref[0] = _dot(a, w_ref[0].astype(BF16)) + b_ref[0]


def _modulation(cond, ada_w, ada_b):
    depth, d, n = ada_w.shape
    rows = cond.shape[0]
    tn = 1024
    return pl.pallas_call(
        _mod_kernel,
        grid=(depth, n // tn),
        in_specs=[
            pl.BlockSpec((rows, d), lambda l, j: (0, 0)),
            pl.BlockSpec((1, d, tn), lambda l, j: (l, 0, j)),
            pl.BlockSpec((1, 1, tn), lambda l, j: (l, 0, j)),
        ],
        out_specs=pl.BlockSpec((1, rows, tn), lambda l, j: (l, 0, j)),
        out_shape=jax.ShapeDtypeStruct((depth, rows, n), F32),
        compiler_params=_cparams("parallel", "parallel"),
        name="modulation",
    )(cond, ada_w, ada_b.reshape(depth, 1, n))


def _prenorm_kernel(x_ref, g_ref, mod_ref, h_ref):
    d = x_ref.shape[-1]
    y = _rms(x_ref[...], g_ref[...])
    shift = mod_ref[0, :, 0:d]
    scale = mod_ref[0, :, d:2 * d]
    h_ref[...] = (y * (1.0 + scale) + shift).astype(BF16)


def _prenorm(x, g, mod, grp0, rows_per_grp, tm):
    m, d = x.shape
    tpg = rows_per_grp // tm
    return pl.pallas_call(
        _prenorm_kernel,
        grid=(m // tm,),
        in_specs=[
            pl.BlockSpec((tm, d), lambda i: (i, 0)),
            pl.BlockSpec((1, d), lambda i: (0, 0)),
            pl.BlockSpec((1, 1, 3 * d), lambda i: (grp0 + i // tpg, 0, 0)),
        ],
        out_specs=pl.BlockSpec((tm, d), lambda i: (i, 0)),
        out_shape=jax.ShapeDtypeStruct((m, d), BF16),
        compiler_params=_cparams("parallel"),
        name="prenorm",
    )(x, g.reshape(1, d), mod)


def _cast_kernel(w_ref, o_ref, *, q_tiles):
    scale = jnp.where(pl.program_id(0) < q_tiles, Q_FOLD, 1.0)
    o_ref[...] = (w_ref[...] * scale).astype(BF16)


def _cast_cols(w3, layer, first_tile, n_tiles, tw, q_tiles=0):
    k = w3.shape[1]
    return pl.pallas_call(
        functools.partial(_cast_kernel, q_tiles=q_tiles),
        grid=(n_tiles,),
        in_specs=[pl.BlockSpec((None, k, tw), lambda j: (layer, 0, first_tile + j))],
        out_specs=pl.BlockSpec((k, tw), lambda j: (0, j)),
        out_shape=jax.ShapeDtypeStruct((k, n_tiles * tw), BF16),
        compiler_params=_cparams("parallel"),
        name="weight_cast",
    )(w3)


def _cast_t_kernel(a_ref, b_ref, o_ref, *, q_tiles, gla_q_tile, plain_tiles, skip):
    j = pl.program_id(0)

    @pl.when(j < plain_tiles)
    def _():
        scale = jnp.where(j < q_tiles, Q_FOLD, jnp.where(j == gla_q_tile, DK_B ** -0.5, 1.0))
        o_ref[...] = (a_ref[...] * scale).T.astype(BF16)

    @pl.when(j >= plain_tiles)
    def _():
        x = jnp.concatenate([a_ref[skip:, :], b_ref[...]], axis=0)
        o_ref[...] = x.T.astype(BF16)


def _cast_even_t(w3t, layer):
    n_src, k = w3t.shape[1:]
    tw = 512
    skip = 2 * GLA_RANK
    assert EV_GP % tw == 0 and N_EVEN_COLS % tw == 0 and n_src == N_EVEN_COLS + skip
    return pl.pallas_call(
        functools.partial(_cast_t_kernel, q_tiles=H_A * HEAD_DIM // tw, gla_q_tile=EV_QB // tw,
                          plain_tiles=EV_GP // tw, skip=skip),
        grid=(N_EVEN_COLS // tw,),
        in_specs=[pl.BlockSpec((None, tw, k), lambda j: (layer, j, 0)),
                  pl.BlockSpec((None, skip, k), lambda j: (layer, (tw // skip) * (j + 1), 0))],
        out_specs=pl.BlockSpec((k, tw), lambda j: (0, j)),
        out_shape=jax.ShapeDtypeStruct((k, N_EVEN_COLS), BF16),
        compiler_params=_cparams("parallel"),
        name="weight_cast_even",
    )(w3t, w3t)


def _cast_rows_t_kernel(a_ref, o_ref):
    o_ref[...] = a_ref[...].T.astype(BF16)


def _cast_rows_t(w3t, layer, row0, n_rows):
    k = w3t.shape[2]
    return pl.pallas_call(
        _cast_rows_t_kernel,
        grid=(1,),
        in_specs=[pl.BlockSpec((None, n_rows, k), lambda j: (layer, row0 // n_rows, 0))],
        out_specs=pl.BlockSpec((k, n_rows), lambda j: (0, 0)),
        out_shape=jax.ShapeDtypeStruct((k, n_rows), BF16),
        compiler_params=_cparams("parallel"),
        name="weight_cast_rows",
    )(w3t)


def _matmul_kernel(h_ref, w_ref, o_ref):
    o_ref[...] = _dot(h_ref[...], w_ref[...]).astype(o_ref.dtype)


def _in_proj(h, w, out_dtype, tm, tn):
    m, k = h.shape
    n = w.shape[1]
    return pl.pallas_call(
        _matmul_kernel,
        grid=(m // tm, n // tn),
        in_specs=[
            pl.BlockSpec((tm, k), lambda i, j: (i, 0)),
            pl.BlockSpec((k, tn), lambda i, j: (0, j)),
        ],
        out_specs=pl.BlockSpec((tm, tn), lambda i, j: (i, j)),
        out_shape=jax.ShapeDtypeStruct((m, n), out_dtype),
        compiler_params=_cparams("parallel", "parallel"),
        name="in_proj",
    )(h, w)


def _matmul_kv_kernel(h_ref, w_ref, kin_ref, vin_ref, o_ref, k_ref, v_ref, *, k_col, v_col, width):
    del kin_ref, vin_ref
    tn = w_ref.shape[-1]
    j = pl.program_id(1)
    res = _dot(h_ref[...], w_ref[...])
    o_ref[...] = res.astype(o_ref.dtype)
    for col, ref in ((k_col, k_ref), (v_col, v_ref)):
        for jj in range(col // tn, (col + width - 1) // tn + 1):
            lo, hi = max(col, jj * tn), min(col + width, (jj + 1) * tn)

            @pl.when(j == jj)
            def _(lo=lo, hi=hi, jj=jj, col=col, ref=ref):
                piece = res[:, lo - jj * tn:hi - jj * tn]
                ref[:, :, lo - col:hi - col] = piece.reshape(ref.shape[0], ref.shape[1], hi - lo)


def _in_proj_kv(h, w, tm, tn, k_buf, v_buf, layer, k_col, v_col):
    m, k = h.shape
    n = w.shape[1]
    batch, _, seq, width = k_buf.shape
    assert tm % seq == 0
    kvspec = pl.BlockSpec((tm // seq, None, seq, width), lambda i, j: (i, layer, 0, 0))
    anyspec = pl.BlockSpec(memory_space=pl.ANY)
    return pl.pallas_call(
        functools.partial(_matmul_kv_kernel, k_col=k_col, v_col=v_col, width=width),
        grid=(m // tm, n // tn),
        in_specs=[
            pl.BlockSpec((tm, k), lambda i, j: (i, 0)),
            pl.BlockSpec((k, tn), lambda i, j: (0, j)),
            anyspec, anyspec,
        ],
        out_specs=[pl.BlockSpec((tm, tn), lambda i, j: (i, j)), kvspec, kvspec],
        out_shape=[jax.ShapeDtypeStruct((m, n), F32),
                   jax.ShapeDtypeStruct(k_buf.shape, F32), jax.ShapeDtypeStruct(v_buf.shape, F32)],
        input_output_aliases={2: 1, 3: 2},
        compiler_params=_cparams("arbitrary", "arbitrary"),
        name="in_proj_kv",
    )(h, w, k_buf, v_buf)


def _gla_merge(of_ref, ob_ref, gn_ref, gp0_ref, gp1_ref, rows):
    hpv = H_B // 2
    outs = []
    for h in range(H_B):
        vl = slice(h * DV_B, (h + 1) * DV_B)
        gp_ref = gp0_ref if h < hpv else gp1_ref
        gp = gp_ref[rows, (h % hpv) * DV_B:(h % hpv + 1) * DV_B].astype(F32)
        o = _rms(of_ref[rows, vl].astype(F32) + ob_ref[rows, vl].astype(F32), gn_ref[...])
        outs.append((o * _silu(gp)).astype(BF16))
    return jnp.concatenate(outs, axis=1)


def _outproj_kernel(*refs, final, fuse_gla):
    if fuse_gla:
        ya_ref, of_ref, ob_ref, gn_ref, gp0_ref, gp1_ref = refs[:6]
        refs = refs[6:]
    else:
        ya_ref, yb_ref = refs[:2]
        refs = refs[2:]
    w_ref, x_ref, mod_ref, g_ref, modn_ref = refs[:5]
    out_refs = refs[5:]
    d = x_ref.shape[-1]
    half = ya_ref.shape[-1]
    tm = x_ref.shape[0]
    gate = mod_ref[0, :, 2 * d:3 * d]
    bounds = [0] + [tm - s for s in OUT_TAIL if s < tm] + [tm]
    for r0, r1 in zip(bounds[:-1], bounds[1:]):
        rows = slice(r0, r1)
        if fuse_gla:
            yb = _gla_merge(of_ref, ob_ref, gn_ref, gp0_ref, gp1_ref, rows)
        else:
            yb = yb_ref[rows, :]
        acc = _dot(ya_ref[rows, :], w_ref[0:half, :]) + _dot(yb, w_ref[half:2 * half, :])
        xn = x_ref[rows, :] + gate * acc
        if final:
            out_refs[0][rows, :] = _rms(xn, g_ref[...])
        else:
            out_refs[0][rows, :] = xn
            shift = modn_ref[0, :, 0:d]
            scale = modn_ref[0, :, d:2 * d]
            out_refs[1][rows, :] = (_rms(xn, g_ref[...]) * (1.0 + scale) + shift).astype(BF16)


def _out_proj(ya, second, w, x, mod, g_next, mod_next, grp0, rows_per_grp, tm, final):
    m, d = x.shape
    half = ya.shape[1]
    tpg = rows_per_grp // tm
    modspec = pl.BlockSpec((1, 1, 3 * d), lambda i: (grp0 + i // tpg, 0, 0))
    rowspec = pl.BlockSpec((tm, d), lambda i: (i, 0))
    halfspec = pl.BlockSpec((tm, half), lambda i: (i, 0))
    fuse_gla = isinstance(second, tuple)
    if fuse_gla:
        o_f, o_b, gnorm, proj = second
        gw = half // 2
        mix_specs = [halfspec, halfspec, halfspec,
                     pl.BlockSpec((1, DV_B), lambda i: (0, 0)),
                     pl.BlockSpec((tm, gw), lambda i: (i, EV_GPB // gw)),
                     pl.BlockSpec((tm, gw), lambda i: (i, EV_GPB // gw + 1))]
        mix_args = [ya, o_f, o_b, gnorm.reshape(1, DV_B), proj, proj]
    else:
        mix_specs = [halfspec, halfspec]
        mix_args = [ya, second]
    if final:
        out_shape = [jax.ShapeDtypeStruct((m, d), F32)]
        out_specs = [rowspec]
    else:
        out_shape = [jax.ShapeDtypeStruct((m, d), F32), jax.ShapeDtypeStruct((m, d), BF16)]
        out_specs = [rowspec, rowspec]
    return pl.pallas_call(
        functools.partial(_outproj_kernel, final=final, fuse_gla=fuse_gla),
        grid=(m // tm,),
        in_specs=mix_specs + [
            pl.BlockSpec((2 * half, d), lambda i: (0, 0)),
            rowspec, modspec,
            pl.BlockSpec((1, d), lambda i: (0, 0)),
            modspec,
        ],
        out_specs=out_specs,
        out_shape=out_shape,
        compiler_params=_cparams("parallel"),
        name="out_proj",
    )(*mix_args, w, x, mod, g_next.reshape(1, d), mod_next)


def _ctx_attn_kernel(*refs, n_heads, n_kv, has_sink):
    if has_sink:
        sink_ref, q_ref, k_ref, v_ref, gp0_ref, gp1_ref, o_ref = refs
    else:
        q_ref, k_ref, v_ref, gp0_ref, gp1_ref, o_ref = refs
    L = q_ref.shape[0]
    grp = n_heads // n_kv
    outs = [None] * n_heads
    for kv in range(n_kv):
        heads = list(range(kv * grp, (kv + 1) * grp))
        qg = jnp.concatenate(
            [q_ref[:, h * HEAD_DIM:(h + 1) * HEAD_DIM].astype(BF16) for h in heads], axis=0)
        k = k_ref[:, kv * HEAD_DIM:(kv + 1) * HEAD_DIM].astype(BF16)
        v = v_ref[:, kv * HEAD_DIM:(kv + 1) * HEAD_DIM].astype(BF16)
        s = _dot_nt(qg, k)
        sk = None
        if has_sink:
            sk = jnp.concatenate([jnp.full((L, 1), sink_ref[h] * LOG2E, F32) for h in heads], axis=0)
        p, den = _softmax_small(s, sk)
        o = _dot(p, v) / den
        for i, h in enumerate(heads):
            outs[h] = o[i * L:(i + 1) * L, :]
    _store_gated(o_ref, outs, gp0_ref, gp1_ref)


def _store_gated(o_ref, head_outs, gp0_ref, gp1_ref):
    hw = gp0_ref.shape[-1]
    per = hw // HEAD_DIM
    for i, gp_ref in enumerate((gp0_ref, gp1_ref)):
        y = jnp.concatenate(head_outs[i * per:(i + 1) * per], axis=1)
        o_ref[:, i * hw:(i + 1) * hw] = (y * _silu(gp_ref[...].astype(F32))).astype(BF16)


def _ctx_attention(proj, sink, batch, L, q_col, k_col, v_col, gp_col, n_heads, n_kv):
    qw = n_heads * HEAD_DIM
    kw = n_kv * HEAD_DIM
    gw = qw // 2
    has_sink = sink is not None
    in_specs = [
        pl.BlockSpec((L, qw), lambda b: (b, q_col // qw)),
        pl.BlockSpec((L, kw), lambda b: (b, k_col // kw)),
        pl.BlockSpec((L, kw), lambda b: (b, v_col // kw)),
        pl.BlockSpec((L, gw), lambda b: (b, gp_col // gw)),
        pl.BlockSpec((L, gw), lambda b: (b, gp_col // gw + 1)),
    ]
    args = [proj, proj, proj, proj, proj]
    if has_sink:
        in_specs = [pl.BlockSpec(memory_space=pltpu.SMEM)] + in_specs
        args = [sink] + args
    return pl.pallas_call(
        functools.partial(_ctx_attn_kernel, n_heads=n_heads, n_kv=n_kv, has_sink=has_sink),
        grid=(batch,),
        in_specs=in_specs,
        out_specs=pl.BlockSpec((L, qw), lambda b: (b, 0)),
        out_shape=jax.ShapeDtypeStruct((batch * L, qw), BF16),
        compiler_params=_cparams("parallel"),
        name="ctx_attention",
    )(*args)


def _rope(x, cos, sin_signed):
    lane = lax.broadcasted_iota(jnp.int32, x.shape, 1)
    first = (lane % 64) < 32
    swapped = jnp.where(first, pltpu.roll(x, 96, 1), pltpu.roll(x, 32, 1))
    return x * cos + swapped * sin_signed


def _win_attn_kernel(sink_ref, q_ref, kp_ref, kc_ref, kn_ref, vp_ref, vc_ref, vn_ref,
                     cq_ref, sq_ref, cp_ref, sp_ref, cn_ref, sn_ref, mask_ref,
                     ck_ref, cv_ref, gp0_ref, gp1_ref, o_ref, *, nsteps):
    p = pl.program_id(1)
    blk = BAND_BLK
    grp = H_A // KV_A
    cq, sq = cq_ref[...], sq_ref[...]
    lane3 = lax.broadcasted_iota(jnp.int32, (1, 3 * blk), 1)
    edges = []
    for t in range(WIN_QB):
        edge = jnp.zeros((1, 3 * blk), F32)
        if t == 0:
            edge = edge + jnp.where(lane3 < blk, jnp.where(p > 0, 0.0, NEG), 0.0)
        if t == WIN_QB - 1:
            edge = edge + jnp.where(lane3 >= 2 * blk, jnp.where(p < nsteps - 1, 0.0, NEG), 0.0)
        edges.append(edge)
    outs = [[None] * WIN_QB for _ in range(H_A)]
    for kv in range(KV_A):
        sl = slice(kv * HEAD_DIM, (kv + 1) * HEAD_DIM)
        k_all = jnp.concatenate([
            _rope(kp_ref[:, sl].astype(F32), cp_ref[...], sp_ref[...]),
            _rope(kc_ref[:, sl].astype(F32), cq, sq),
            _rope(kn_ref[:, sl].astype(F32), cn_ref[...], sn_ref[...]),
        ], axis=0).astype(BF16)
        v_all = jnp.concatenate([vp_ref[:, sl], vc_ref[:, sl], vn_ref[:, sl]], axis=0).astype(BF16)
        heads = list(range(kv * grp, (kv + 1) * grp))
        q_rot = [_rope(q_ref[:, h * HEAD_DIM:(h + 1) * HEAD_DIM].astype(F32), cq, sq).astype(BF16)
                 for h in heads]
        ck = ck_ref[:, sl]
        cv = cv_ref[:, sl]
        for t in range(WIN_QB):
            rows = slice(t * blk, (t + 1) * blk)
            band = slice(t * blk, (t + 3) * blk)
            qg = jnp.concatenate([q[rows, :] for q in q_rot], axis=0)
            s_band = _dot_nt(qg, k_all[band, :])
            s_ctx = _dot_nt(qg, ck)
            p, sink_w = _softmax_rows(
                [s_band, s_ctx],
                [lambda r0, nr, t=t: mask_ref[r0 % blk:r0 % blk + nr, :] + edges[t], None],
                lambda r0: sink_ref[heads[r0 // blk]] * LOG2E, chunked=True)
            o = _pv_normalised(p, jnp.concatenate([v_all[band, :], cv], axis=0), sink_w)
            for i, h in enumerate(heads):
                outs[h][t] = o[i * blk:(i + 1) * blk, :]
    _store_gated(o_ref, [jnp.concatenate(o, axis=0) for o in outs], gp0_ref, gp1_ref)


def _window_attention(proj, sink, cache_k, cache_v, layer, cos_tab, sin_tab, batch, T):
    blk = BAND_BLK
    nb = T // blk
    qb = WIN_QB
    assert nb % qb == 0
    nsteps = nb // qb
    rq = qb * blk
    qw = H_A * HEAD_DIM
    kw = KV_A * HEAD_DIM
    gw = qw // 2
    P = cache_k.shape[2]

    def prev(p):
        return jnp.maximum(qb * p - 1, 0)

    def nxt(p):
        return jnp.minimum(qb * p + qb, nb - 1)

    kcol, vcol = EV_KA // kw, EV_VA // kw

    def edge(col, f):
        return pl.BlockSpec((blk, kw), lambda b, p: (b * nb + f(p), col))

    def mid(col):
        return pl.BlockSpec((rq, kw), lambda b, p: (b * nsteps + p, col))

    tab_edge = lambda f: pl.BlockSpec((blk, HEAD_DIM), lambda b, p: (f(p), 0))
    tab_mid = pl.BlockSpec((rq, HEAD_DIM), lambda b, p: (p, 0))
    in_specs = [
        pl.BlockSpec(memory_space=pltpu.SMEM),
        pl.BlockSpec((rq, qw), lambda b, p: (b * nsteps + p, EV_QA // qw)),
        edge(kcol, prev), mid(kcol), edge(kcol, nxt),
        edge(vcol, prev), mid(vcol), edge(vcol, nxt),
        tab_mid, tab_mid, tab_edge(prev), tab_edge(prev), tab_edge(nxt), tab_edge(nxt),
        pl.BlockSpec((blk, 3 * blk), lambda b, p: (0, 0)),
        pl.BlockSpec((None, None, P, kw), lambda b, p: (b, layer, 0, 0)),
        pl.BlockSpec((None, None, P, kw), lambda b, p: (b, layer, 0, 0)),
        pl.BlockSpec((rq, gw), lambda b, p: (b * nsteps + p, EV_GPA // gw)),
        pl.BlockSpec((rq, gw), lambda b, p: (b * nsteps + p, EV_GPA // gw + 1)),
    ]
    return pl.pallas_call(
        functools.partial(_win_attn_kernel, nsteps=nsteps),
        grid=(batch, nsteps),
        in_specs=in_specs,
        out_specs=pl.BlockSpec((rq, qw), lambda b, p: (b * nsteps + p, 0)),
        out_shape=jax.ShapeDtypeStruct((batch * T, qw), BF16),
        compiler_params=_cparams("parallel", "parallel"),
        name="window_attention",
    )(sink, proj, proj, proj, proj, proj, proj, proj,
      cos_tab, sin_tab, cos_tab, sin_tab, cos_tab, sin_tab, _band_mask(),
      cache_k, cache_v, proj, proj)


def _band_mask():
    iq = np.arange(BAND_BLK)[:, None]
    jk = np.arange(3 * BAND_BLK)[None, :]
    rel = iq - jk + BAND_BLK
    return jnp.asarray(np.where(np.abs(rel) <= WINDOW, 0.0, NEG), F32)


def _log_sigmoid(x):
    return jnp.minimum(x, 0.0) - jnp.log1p(jnp.exp(-jnp.abs(x)))


def _cumsum_rows(tri, g):
    g_hi = g.astype(BF16)
    r1 = g - g_hi.astype(F32)
    g_mid = r1.astype(BF16)
    g_lo = (r1 - g_mid.astype(F32)).astype(BF16)
    return _dot(tri, g_hi) + (_dot(tri, g_mid) + _dot(tri, g_lo))


def _gla_kernel(*refs, nsteps, has_s0):
    fwd_refs, bwd_refs = refs[0:5], refs[5:10]
    wg_ref, bg_ref = refs[10:12]
    if has_s0:
        s0_ref, of_ref, ob_ref, s_scr = refs[12:]
        sfin_ref = None
    else:
        of_ref, ob_ref, sfin_ref, s_scr = refs[13:]
    qf = fwd_refs[0]
    step = pl.program_id(1)
    C = GLA_CHUNK
    G = GLA_GROUP
    R = G * C
    n_sub = qf.shape[0] // R

    @pl.when(step == 0)
    def _():
        if has_s0:
            s_scr[...] = s0_ref[...]
        else:
            s_scr[...] = jnp.zeros(s_scr.shape, F32)

    ri = lax.broadcasted_iota(jnp.int32, (R, R), 0)
    ci = lax.broadcasted_iota(jnp.int32, (R, R), 1)
    same_chunk = (ri // C) == (ci // C)
    hw = H_B * DK_B
    hpv = H_B // 2
    for d, ((q_ref, k_ref, v0_ref, v1_ref, r_ref), o_ref) in enumerate(
            ((fwd_refs, of_ref), (bwd_refs, ob_ref))):
        keep = same_chunk & ((ci <= ri) if d == 0 else (ci >= ri))
        tri = jnp.where(keep, 1.0, 0.0).astype(BF16)
        edge = C - 1 if d == 0 else 0
        order = range(G) if d == 0 else range(G - 1, -1, -1)
        states = [s_scr[d, h] for h in range(H_B)]
        for sub in (range(n_sub) if d == 0 else range(n_sub - 1, -1, -1)):
            blk = slice(sub * R, (sub + 1) * R)
            logit = (_dot(r_ref[blk, :].astype(BF16), wg_ref[:, d * hw:(d + 1) * hw])
                     + bg_ref[:, d * hw:(d + 1) * hw])
            g = _log_sigmoid(logit) / GLA_NORMALIZER
            bc = _cumsum_rows(tri, g)
            totals = [bc[j * C + edge:j * C + edge + 1, :] for j in range(G)]
            tot_full = jnp.concatenate([jnp.broadcast_to(t, (C, hw)) for t in totals], axis=0)
            e_pos = jnp.exp(bc)
            e_neg = jnp.exp(-bc)
            e_rem = jnp.exp(tot_full - bc)
            e_tot = [jnp.exp(t) for t in totals]
            for h in range(H_B):
                sl = slice(h * DK_B, (h + 1) * DK_B)
                vl = slice(h * DV_B, (h + 1) * DV_B)
                qh = q_ref[blk, sl].astype(F32)
                kh = k_ref[blk, sl].astype(F32)
                v_ref = v0_ref if h < hpv else v1_ref
                vh = v_ref[blk, (h % hpv) * DV_B:(h % hpv + 1) * DV_B].astype(BF16)
                qi = (qh * e_pos[:, sl]).astype(BF16)
                ki = (kh * e_neg[:, sl]).astype(BF16)
                ks = kh * e_rem[:, sl]
                att = jnp.where(keep, _dot_nt(qi, ki), 0.0).astype(BF16)
                o_intra = _dot(att, vh)
                S = states[h]
                for j in order:
                    rows = slice(j * C, (j + 1) * C)
                    out_rows = slice(sub * R + j * C, sub * R + (j + 1) * C)
                    o_ref[out_rows, vl] = (o_intra[rows, :]
                                           + _dot(qi[rows, :], S.astype(BF16))).astype(o_ref.dtype)
                    stacked = jnp.concatenate(
                        [ks[rows, :], jnp.broadcast_to(e_tot[j][:, sl], (C, DK_B))], axis=0)
                    tr = stacked.T
                    S = tr[:, C:C + 1] * S + _dot(tr[:, 0:C].astype(BF16), vh[rows, :])
                states[h] = S
        for h in range(H_B):
            s_scr[d, h] = states[h]

    if sfin_ref is not None:
        @pl.when(step == nsteps - 1)
        def _():
            sfin_ref[...] = s_scr[...]


def _gla(proj, r_proj, wg_full, bg_full, s0, state_buf, layer, batch, T):
    group_rows = GLA_GROUP * GLA_CHUNK
    C = group_rows * max(1, min(GLA_STEP_GROUPS, T // group_rows))
    assert T % C == 0
    nc = T // C
    m = batch * T
    qw, vw, rw = H_B * DK_B, H_B * DV_B, 128
    vh = vw // 2

    def fwd(b, c):
        return b * nc + c

    def bwd(b, c):
        return b * nc + (nc - 1 - c)

    def specs(f):
        return [
            pl.BlockSpec((C, qw), lambda b, c: (f(b, c), EV_QB // qw)),
            pl.BlockSpec((C, qw), lambda b, c: (f(b, c), EV_KB // qw)),
            pl.BlockSpec((C, vh), lambda b, c: (f(b, c), EV_VB // vh)),
            pl.BlockSpec((C, vh), lambda b, c: (f(b, c), EV_VB // vh + 1)),
            pl.BlockSpec((C, rw), lambda b, c: (f(b, c), 0)),
        ]

    in_specs = specs(fwd) + specs(bwd) + [
        pl.BlockSpec((rw, 2 * qw), lambda b, c: (0, 0)),
        pl.BlockSpec((1, 2 * qw), lambda b, c: (0, 0)),
    ]
    args = [proj, proj, proj, proj, r_proj] * 2 + [wg_full, bg_full]
    has_s0 = s0 is not None
    state_spec = pl.BlockSpec((None, None, 2, H_B, DK_B, DV_B), lambda b, c: (b, layer, 0, 0, 0, 0))
    out_specs = [pl.BlockSpec((C, vw), lambda b, c: (fwd(b, c), 0)),
                 pl.BlockSpec((C, vw), lambda b, c: (bwd(b, c), 0))]
    out_shape = [jax.ShapeDtypeStruct((m, vw), BF16), jax.ShapeDtypeStruct((m, vw), BF16)]
    aliases = {}
    if has_s0:
        in_specs.append(state_spec)
        args.append(s0)
    else:
        in_specs.append(pl.BlockSpec(memory_space=pl.ANY))
        args.append(state_buf)
        out_specs.append(state_spec)
        out_shape.append(jax.ShapeDtypeStruct(state_buf.shape, F32))
        aliases = {len(args) - 1: 2}
    return pl.pallas_call(
        functools.partial(_gla_kernel, nsteps=nc, has_s0=has_s0),
        grid=(batch, nc),
        in_specs=in_specs,
        out_specs=out_specs,
        out_shape=out_shape,
        input_output_aliases=aliases,
        scratch_shapes=[pltpu.VMEM((2, H_B, DK_B, DV_B), F32)],
        compiler_params=_cparams("arbitrary", "arbitrary"),
        name="gla_scan",
    )(*args)


def _na_bias_kernel(u_ref, o_ref):
    place = pl.program_id(0)
    W = GRID_W
    lane = lax.broadcasted_iota(jnp.int32, (W, 128), 1)
    qcol = lax.broadcasted_iota(jnp.int32, (W, 128), 0)
    kcol = lane % W
    cs = jnp.clip(qcol - NA_KW // 2, 0, W - NA_KW)
    col_ok = (kcol >= cs) & (kcol < cs + NA_KW)
    for t in range(NA_QR):
        first = jnp.where(place == 0, 0, jnp.where(place == 1, t, NA_WR - NA_KH))

        def piece(i, base_shift):
            dr = i - t + (NA_KH - 1) - (NA_WR - NA_KH) * place
            in_rows = (i >= first) & (i < first + NA_KH)
            u = jnp.broadcast_to(u_ref[0, pl.ds(jnp.clip(dr, 0, 2 * NA_KH - 2), 1), :], (W, 128))
            rolled = pltpu.roll(u, base_shift, 1, stride=1, stride_axis=0)
            return rolled + jnp.where(in_rows, 0.0, NEG)

        tiles = []
        for i in range(0, NA_WR, 2):
            tile = jnp.where(lane < W, piece(i, 65), piece(i + 1, 1))
            tiles.append(jnp.where(col_ok, tile, NEG) * LOG2E)
        o_ref[0, 0, t * W:(t + 1) * W, :] = jnp.concatenate(tiles, axis=1)


def _na_bias_expand(u_tab):
    W = GRID_W
    n_tab = u_tab.shape[1]
    return pl.pallas_call(
        _na_bias_kernel,
        grid=(3, H_C),
        in_specs=[pl.BlockSpec((1, n_tab, 128), lambda p, h: (h, 0, 0))],
        out_specs=pl.BlockSpec((1, 1, NA_QR * W, NA_WR * W), lambda p, h: (p, h, 0, 0)),
        out_shape=jax.ShapeDtypeStruct((3, H_C, NA_QR * W, NA_WR * W), F32),
        compiler_params=_cparams("parallel", "parallel"),
        name="na_bias_expand",
    )(u_tab)


def _na_kernel(*refs):
    nblk = NA_WR // NA_QR
    q_ref = refs[0]
    k_refs = refs[1:1 + nblk]
    v_refs = refs[1 + nblk:1 + 2 * nblk]
    bias_ref, ck_ref, cv_ref, gp_ref, o_ref = refs[1 + 2 * nblk:]
    outs = []
    for h in range(H_C):
        sl = slice(h * HEAD_DIM, (h + 1) * HEAD_DIM)
        q = q_ref[:, sl]
        kw = jnp.concatenate([r[:, sl] for r in k_refs], axis=0)
        v_cat = jnp.concatenate([r[:, sl] for r in v_refs] + [cv_ref[:, sl]], axis=0)
        s_nb = _dot_nt(q, kw)
        s_ctx = _dot_nt(q, ck_ref[:, sl])
        p, _ = _softmax_rows([s_nb, s_ctx],
                             [lambda r0, nr, h=h: bias_ref[0, h, r0:r0 + nr, :], None], None,
                             chunked=False)
        outs.append(_pv_normalised(p, v_cat, None))
    y = jnp.concatenate(outs, axis=1)
    o_ref[...] = (y * _silu(gp_ref[...].astype(F32))).astype(BF16)


def _neighbourhood_attention(proj, bias_exp, cache_k, cache_v, layer, batch, T):
    W = GRID_W
    rows = T // W
    assert rows % NA_QR == 0 and rows >= NA_WR and NA_WR % NA_QR == 0
    steps = rows // NA_QR
    nblk = NA_WR // NA_QR
    rq = NA_QR * W
    qw = H_C * HEAD_DIM
    P = cache_k.shape[2]

    def win0(p):
        return jnp.clip(p - 1, 0, steps - nblk)

    def place(p):
        return jnp.where(p == 0, 0, jnp.where(p == steps - 1, 2, 1))

    def win_spec(col, i):
        return pl.BlockSpec((rq, qw), lambda b, p: (b * steps + win0(p) + i, col // qw))

    in_specs = ([pl.BlockSpec((rq, qw), lambda b, p: (b * steps + p, OD_QC // qw))]
                + [win_spec(OD_KC, i) for i in range(nblk)]
                + [win_spec(OD_VC, i) for i in range(nblk)]
                + [pl.BlockSpec((1, H_C, rq, NA_WR * W), lambda b, p: (place(p), 0, 0, 0)),
                   pl.BlockSpec((None, None, P, qw), lambda b, p: (b, layer, 0, 0)),
                   pl.BlockSpec((None, None, P, qw), lambda b, p: (b, layer, 0, 0)),
                   pl.BlockSpec((rq, qw), lambda b, p: (b * steps + p, OD_GPC // qw))])
    return pl.pallas_call(
        _na_kernel,
        grid=(batch, steps),
        in_specs=in_specs,
        out_specs=pl.BlockSpec((rq, qw), lambda b, p: (b * steps + p, 0)),
        out_shape=jax.ShapeDtypeStruct((batch * T, qw), BF16),
        compiler_params=_cparams("parallel", "parallel"),
        name="neighbourhood_attention",
    )(proj, *([proj] * (2 * nblk)), bias_exp, cache_k, cache_v, proj)


def _gmlp_kernel(u_ref, v_ref, gp_ref, g_ref, ws_ref, bt_ref, o_ref, *, chunks):
    gw = u_ref.shape[-1] // G_D
    for ch in range(chunks):
        rs_ = slice(ch * D_CHUNK, (ch + 1) * D_CHUNK)
        v = v_ref[rs_, :].astype(F32)
        mu = jnp.mean(v, axis=-1, keepdims=True)
        var = jnp.mean(jnp.square(v - mu), axis=-1, keepdims=True)
        vn = ((v - mu) * lax.rsqrt(var + 1e-5) * g_ref[...]).astype(BF16)
        for g in range(G_D):
            cl = slice(g * gw, (g + 1) * gw)
            sp = _dot(ws_ref[g], vn[:, cl]) + bt_ref[:, g:g + 1]
            y = u_ref[rs_, cl].astype(F32) * sp
            o_ref[rs_, cl] = (y * _silu(gp_ref[rs_, cl].astype(F32))).astype(BF16)


def _gmlp(proj, gnorm, ws_bf, b_t, tm):
    m = proj.shape[0]
    uw = HALF_W
    return pl.pallas_call(
        functools.partial(_gmlp_kernel, chunks=tm // D_CHUNK),
        grid=(m // tm,),
        in_specs=[
            pl.BlockSpec((tm, uw), lambda i: (i, OD_U // uw)),
            pl.BlockSpec((tm, uw), lambda i: (i, OD_V // uw)),
            pl.BlockSpec((tm, uw), lambda i: (i, OD_GPD // uw)),
            pl.BlockSpec((1, uw), lambda i: (0, 0)),
            pl.BlockSpec(ws_bf.shape, lambda i: (0, 0, 0)),
            pl.BlockSpec(b_t.shape, lambda i: (0, 0)),
        ],
        out_specs=pl.BlockSpec((tm, uw), lambda i: (i, 0)),
        out_shape=jax.ShapeDtypeStruct((m, uw), BF16),
        compiler_params=_cparams("parallel"),
        name="gmlp",
    )(proj, proj, proj, gnorm.reshape(1, uw), ws_bf, b_t)


def _rope_tables(T):
    t = np.arange(T)
    nf = HEAD_DIM // 4
    inv = ROPE_BASE ** (-jnp.arange(nf, dtype=F32) / nf)
    pos = jnp.stack([t // GRID_W, t % GRID_W], axis=-1).astype(F32)
    ang = pos[:, :, None] * inv
    cos, sin = jnp.cos(ang), jnp.sin(ang)
    cos_tab = jnp.concatenate([cos, cos], axis=-1).reshape(T, HEAD_DIM)
    sin_tab = jnp.concatenate([-sin, sin], axis=-1).reshape(T, HEAD_DIM)
    return cos_tab, sin_tab


def kernel(x_prompt, x_sample, cache_attn_k, cache_attn_v, state_gla, cache_na_k, cache_na_v,
           c, c_ctx, ada_w, ada_b, norm_g, w_in_even, w_in_odd, w_out, attn_sink,
           gla_wg, gla_bg, gla_norm_g, na_bias, gmlp_norm_g, gmlp_ws, gmlp_b, final_norm_g):
    B, L, D = x_prompt.shape
    Bd, T, _ = x_sample.shape
    depth = ada_w.shape[0]
    n_even = w_in_even.shape[0]
    n_odd = w_in_odd.shape[0]
    P = cache_attn_k.shape[2]
    Mc, Ml = B * L, Bd * T

    n_rows = 8 * ((1 + Bd + 7) // 8)
    cond = jnp.concatenate([c_ctx[None, :], c, jnp.zeros((n_rows - 1 - Bd, D), F32)], axis=0)
    mod = _modulation(cond, ada_w, ada_b)
    mod = mod.reshape(depth, n_rows, 1, 3 * D)

    tw = 512
    w_even_t = jnp.swapaxes(w_in_even, 1, 2)
    w_even = [_cast_even_t(w_even_t, e) for e in range(n_even)]
    w_rank = [_cast_rows_t(w_even_t, e, EV_RB_SRC, 128) for e in range(n_even)]
    w_odd = [_cast_cols(w_in_odd, o, 0, N_ODD_COLS // tw, tw, q_tiles=H_C * HEAD_DIM // tw)
             for o in range(n_odd)]
    w_o = [_cast_cols(w_out, l, 0, D // tw, tw) for l in range(depth)]
    hw = H_B * DK_B
    wg_full, bg_full = [], []
    for e in range(n_even):
        wg = jnp.zeros((128, 2 * hw), F32)
        wg = wg.at[0:GLA_RANK, 0:hw].set(gla_wg[e, 0])
        wg = wg.at[GLA_RANK:2 * GLA_RANK, hw:2 * hw].set(gla_wg[e, 1])
        wg_full.append(wg.astype(BF16))
        bg_full.append(gla_bg[e].reshape(1, 2 * hw))
    pad_l = 63 - (NA_KW - 1)
    u_tabs = [jnp.pad(na_bias[o], ((0, 0), (0, 0), (pad_l, 128 - pad_l - (2 * NA_KW - 1))), mode="edge")
              for o in range(n_odd)]
    ws_bf = [gmlp_ws[o].astype(BF16) for o in range(n_odd)]
    b_t = [gmlp_b[o].T for o in range(n_odd)]
    cos_tab, sin_tab = _rope_tables(T)
    ck_a = cache_attn_k.reshape(Bd, n_even, P, KV_A * HEAD_DIM).astype(BF16)
    cv_a = cache_attn_v.reshape(Bd, n_even, P, KV_A * HEAD_DIM).astype(BF16)
    ck_c = cache_na_k.reshape(Bd, n_odd, P, H_C * HEAD_DIM).astype(BF16)
    cv_c = cache_na_v.reshape(Bd, n_odd, P, H_C * HEAD_DIM).astype(BF16)

    def run_group(x, is_ctx):
        batch, seq = (B, L) if is_ctx else (Bd, T)
        m = batch * seq
        grp0 = 0 if is_ctx else 1
        rows_per_grp = m if is_ctx else seq
        tm_mm = min(2048, rows_per_grp)
        tm_kv = min(1024, rows_per_grp)
        tm_op = min(512, rows_per_grp)
        side = None
        if is_ctx:
            side = dict(ka=jnp.zeros((B, n_even, L, KV_A * HEAD_DIM), F32),
                        va=jnp.zeros((B, n_even, L, KV_A * HEAD_DIM), F32),
                        st=jnp.zeros((B, n_even, 2, H_B, DK_B, DV_B), F32),
                        kc=jnp.zeros((B, n_odd, L, H_C * HEAD_DIM), F32),
                        vc=jnp.zeros((B, n_odd, L, H_C * HEAD_DIM), F32))
        h = _prenorm(x, norm_g[0], mod[0], grp0, rows_per_grp, tm_op)
        out = None
        for l in range(depth):
            if l % 2 == 0:
                e = l // 2
                r_proj = _in_proj(h, w_rank[e], BF16, tm_mm, 128)
                if is_ctx:
                    proj, side["ka"], side["va"] = _in_proj_kv(
                        h, w_even[e], tm_mm, 512, side["ka"], side["va"], e, EV_KA, EV_VA)
                    ya = _ctx_attention(proj, attn_sink[e], batch, seq, EV_QA, EV_KA, EV_VA, EV_GPA,
                                        H_A, KV_A)
                    o_f, o_b, side["st"] = _gla(proj, r_proj, wg_full[e], bg_full[e], None, side["st"],
                                                e, batch, seq)
                else:
                    proj = _in_proj(h, w_even[e], BF16, min(1024, rows_per_grp), N_EVEN_COLS // 2)
                    ya = _window_attention(proj, attn_sink[e], ck_a, cv_a, e, cos_tab, sin_tab, batch, seq)
                    o_f, o_b = _gla(proj, r_proj, wg_full[e], bg_full[e], state_gla, None, e, batch, seq)
                second = (o_f, o_b, gla_norm_g[e], proj)
            else:
                o = l // 2
                if is_ctx:
                    proj, side["kc"], side["vc"] = _in_proj_kv(
                        h, w_odd[o], tm_kv, 1024, side["kc"], side["vc"], o, OD_KC, OD_VC)
                    ya = _ctx_attention(proj, None, batch, seq, OD_QC, OD_KC, OD_VC, OD_GPC, H_C, H_C)
                else:
                    proj = _in_proj(h, w_odd[o], BF16, tm_mm, 1024)
                    bias_exp = _na_bias_expand(u_tabs[o])
                    ya = _neighbourhood_attention(proj, bias_exp, ck_c, cv_c, o, batch, seq)
                second = _gmlp(proj, gmlp_norm_g[o], ws_bf[o], b_t[o], tm_op)
            final = l == depth - 1
            g_next = final_norm_g if final else norm_g[l + 1]
            mod_next = mod[l] if final else mod[l + 1]
            res = _out_proj(ya, second, w_o[l], x, mod[l], g_next, mod_next, grp0, rows_per_grp, tm_op, final)
            if final:
                out = res[0]
            else:
                x, h = res
        return out, side

    y_prompt, side = run_group(x_prompt.reshape(Mc, D), True)
    y_sample, _ = run_group(x_sample.reshape(Ml, D), False)

    return (y_prompt.reshape(B, L, D), y_sample.reshape(Bd, T, D),
            side["ka"].reshape(B, n_even, L, KV_A, HEAD_DIM), side["va"].reshape(B, n_even, L, KV_A, HEAD_DIM),
            side["st"],
            side["kc"].reshape(B, n_odd, L, H_C, HEAD_DIM), side["vc"].reshape(B, n_odd, L, H_C, HEAD_DIM))
```
